```python
import jax, jax.numpy as jnp
from jax import lax
import numpy as np

D_MODEL = 1024
BATCH = 16
SEQ = 2048
DEPTH = 1

CHUNK = 64
FOX_HEADS = 16
HEAD_DIM = 64
FOX_WIDTH = FOX_HEADS * HEAD_DIM
CONV_GROUPS = 16
CONV_WIDTH = D_MODEL
D_MIX = FOX_WIDTH + CONV_WIDTH
CONV_KERNEL = 31
Q_BLOCK = 128
EPS = 1e-6
NEG_INF = -1e30
IN_COLS = 3 * FOX_WIDTH + FOX_HEADS + FOX_WIDTH + 2 * CONV_WIDTH + CONV_WIDTH

kernel_name = "hybrid_fox_conformer_block"


def rmsnorm(x, g):
    xf = x.astype(jnp.float32)
    y = xf * lax.rsqrt(jnp.mean(xf * xf, axis=-1, keepdims=True) + EPS)
    return (y * g.astype(jnp.float32)).astype(x.dtype)


def layernorm(x, g, b):
    xf = x.astype(jnp.float32)
    mu = jnp.mean(xf, axis=-1, keepdims=True)
    var = jnp.mean(jnp.square(xf - mu), axis=-1, keepdims=True)
    y = (xf - mu) * lax.rsqrt(var + EPS)
    return (y * g.astype(jnp.float32) + b.astype(jnp.float32)).astype(x.dtype)


def fox_attention(q, k, v, log_f):
    S = q.shape[1]
    c = jnp.cumsum(log_f, axis=1)
    c = jnp.transpose(c, (0, 2, 1))
    scale = HEAD_DIM ** -0.5
    outs = []
    for i in range(S // Q_BLOCK):
        q0, q1 = i * Q_BLOCK, (i + 1) * Q_BLOCK
        qb = q[:, q0:q1].astype(jnp.float32)
        kb = k[:, :q1].astype(jnp.float32)
        vb = v[:, :q1]
        logits = jnp.einsum('bqhd,bkhd->bhqk', qb, kb) * scale
        decay = c[:, :, q0:q1, None] - c[:, :, None, :q1]
        qpos = jnp.arange(q0, q1)[:, None]
        kpos = jnp.arange(q1)[None, :]
        logits = jnp.where(kpos <= qpos, logits + decay, NEG_INF)
        probs = jax.nn.softmax(logits, axis=-1).astype(v.dtype)
        outs.append(jnp.einsum('bhqk,bkhd->bqhd', probs, vb))
    return jnp.concatenate(outs, axis=1)


def causal_depthwise_conv(u, w, b):
    C = u.shape[-1]
    y = lax.conv_general_dilated(
        u, w.reshape(CONV_KERNEL, 1, C).astype(u.dtype),
        window_strides=(1,), padding=[(CONV_KERNEL - 1, 0)],
        dimension_numbers=('NWC', 'WIO', 'NWC'), feature_group_count=C)
    return y + b.astype(u.dtype)


def _fwd_setup_inputs(seed: int = 0) -> dict:
    key = jax.random.key(seed)
    ks = jax.random.split(key, 12)
    f32 = jnp.float32
    x = jax.random.normal(ks[0], (BATCH, SEQ, D_MODEL), f32)
    norm_g = 1.0 + 0.02 * jax.random.normal(ks[1], (DEPTH, D_MODEL), f32)
    w_in = jax.random.normal(ks[2], (DEPTH, D_MODEL, IN_COLS), f32) * D_MODEL ** -0.5
    b_forget = (jnp.linspace(1.0, 5.0, FOX_HEADS, dtype=f32)[None, :]
                + 0.1 * jax.random.normal(ks[3], (DEPTH, FOX_HEADS), f32))
    q_norm_g = 1.0 + 0.02 * jax.random.normal(ks[4], (DEPTH, FOX_HEADS, HEAD_DIM), f32)
    k_norm_g = 1.0 + 0.02 * jax.random.normal(ks[5], (DEPTH, FOX_HEADS, HEAD_DIM), f32)
    conv_w = jax.random.normal(ks[6], (DEPTH, CONV_KERNEL, CONV_WIDTH), f32) * CONV_KERNEL ** -0.5
    conv_b = 0.02 * jax.random.normal(ks[7], (DEPTH, CONV_WIDTH), f32)
    conv_ln_g = 1.0 + 0.02 * jax.random.normal(ks[8], (DEPTH, CONV_WIDTH), f32)
    conv_ln_b = 0.02 * jax.random.normal(ks[9], (DEPTH, CONV_WIDTH), f32)
    w_out = jax.random.normal(ks[10], (DEPTH, D_MIX, D_MODEL), f32) * D_MIX ** -0.5
    return {"x": x, "norm_g": norm_g, "w_in": w_in, "b_forget": b_forget,
            "q_norm_g": q_norm_g, "k_norm_g": k_norm_g, "conv_w": conv_w,
            "conv_b": conv_b, "conv_ln_g": conv_ln_g, "conv_ln_b": conv_ln_b,
            "w_out": w_out}


def _fwd_reference(x, norm_g, w_in, b_forget, q_norm_g, k_norm_g, conv_w, conv_b,
              conv_ln_g, conv_ln_b, w_out):
    B, S, _ = x.shape
    o_q = 0
    o_k = o_q + FOX_WIDTH
    o_v = o_k + FOX_WIDTH
    o_f = o_v + FOX_WIDTH
    o_gf = o_f + FOX_HEADS
    o_glu = o_gf + FOX_WIDTH
    o_gc = o_glu + 2 * CONV_WIDTH
    for l in range(DEPTH):
        h = rmsnorm(x, norm_g[l])
        z = jnp.einsum('bsd,de->bse', h, w_in[l])

        q = z[..., o_q:o_k].reshape(B, S, FOX_HEADS, HEAD_DIM)
        k = z[..., o_k:o_v].reshape(B, S, FOX_HEADS, HEAD_DIM)
        v = z[..., o_v:o_f].reshape(B, S, FOX_HEADS, HEAD_DIM)
        q = rmsnorm(q, q_norm_g[l])
        k = rmsnorm(k, k_norm_g[l])
        log_f = jax.nn.log_sigmoid(z[..., o_f:o_gf].astype(jnp.float32)
                                   + b_forget[l].astype(jnp.float32))
        a = fox_attention(q, k, v, log_f).reshape(B, S, FOX_WIDTH)
        a = a * jax.nn.silu(z[..., o_gf:o_glu])

        u = z[..., o_glu:o_gc]
        u = u[..., :CONV_WIDTH] * jax.nn.sigmoid(u[..., CONV_WIDTH:])
        u = causal_depthwise_conv(u, conv_w[l], conv_b[l])
        u = jax.nn.silu(layernorm(u, conv_ln_g[l], conv_ln_b[l]))
        u = u * jax.nn.silu(z[..., o_gc:])

        y = jnp.concatenate([a, u], axis=-1)
        x = x + jnp.einsum('bse,ed->bsd', y, w_out[l])
    return x


import jax as _jax
import jax.numpy as _jnp

TWIN_FORMAT = 'train_step'
FWD_PARAMS = ['x', 'norm_g', 'w_in', 'b_forget', 'q_norm_g', 'k_norm_g', 'conv_w', 'conv_b', 'conv_ln_g', 'conv_ln_b', 'w_out']
TWIN_WEIGHTS = ['norm_g', 'w_in', 'b_forget', 'q_norm_g', 'k_norm_g', 'conv_w', 'conv_b', 'conv_ln_g', 'conv_ln_b', 'w_out']
TWIN_DIFF_INPUT = 'x'
TWIN_INPUTS = ['x', 'norm_g', 'w_in', 'b_forget', 'q_norm_g', 'k_norm_g', 'conv_w', 'conv_b', 'conv_ln_g', 'conv_ln_b', 'w_out', 'loss_target', 'm_norm_g', 'm_w_in', 'm_b_forget', 'm_q_norm_g', 'm_k_norm_g', 'm_conv_w', 'm_conv_b', 'm_conv_ln_g', 'm_conv_ln_b', 'm_w_out', 'v_norm_g', 'v_w_in', 'v_b_forget', 'v_q_norm_g', 'v_k_norm_g', 'v_conv_w', 'v_conv_b', 'v_conv_ln_g', 'v_conv_ln_b', 'v_w_out']
TWIN_OUTPUTS = ['loss', 'grad_x', 'grad_norm_g', 'grad_w_in', 'grad_b_forget', 'grad_q_norm_g', 'grad_k_norm_g', 'grad_conv_w', 'grad_conv_b', 'grad_conv_ln_g', 'grad_conv_ln_b', 'grad_w_out', 'delta_norm_g', 'delta_w_in', 'delta_b_forget', 'delta_q_norm_g', 'delta_k_norm_g', 'delta_conv_w', 'delta_conv_b', 'delta_conv_ln_g', 'delta_conv_ln_b', 'delta_w_out', 'new_m_norm_g', 'new_m_w_in', 'new_m_b_forget', 'new_m_q_norm_g', 'new_m_k_norm_g', 'new_m_conv_w', 'new_m_conv_b', 'new_m_conv_ln_g', 'new_m_conv_ln_b', 'new_m_w_out', 'new_v_norm_g', 'new_v_w_in', 'new_v_b_forget', 'new_v_q_norm_g', 'new_v_k_norm_g', 'new_v_conv_w', 'new_v_conv_b', 'new_v_conv_ln_g', 'new_v_conv_ln_b', 'new_v_w_out']
TWIN_LEAF_KINDS = {'loss': 'loss', 'grad_x': 'grad_x', 'grad_norm_g': 'grad_w', 'grad_w_in': 'grad_w', 'grad_b_forget': 'grad_w', 'grad_q_norm_g': 'grad_w', 'grad_k_norm_g': 'grad_w', 'grad_conv_w': 'grad_w', 'grad_conv_b': 'grad_w', 'grad_conv_ln_g': 'grad_w', 'grad_conv_ln_b': 'grad_w', 'grad_w_out': 'grad_w', 'delta_norm_g': 'delta_w', 'delta_w_in': 'delta_w', 'delta_b_forget': 'delta_w', 'delta_q_norm_g': 'delta_w', 'delta_k_norm_g': 'delta_w', 'delta_conv_w': 'delta_w', 'delta_conv_b': 'delta_w', 'delta_conv_ln_g': 'delta_w', 'delta_conv_ln_b': 'delta_w', 'delta_w_out': 'delta_w', 'new_m_norm_g': 'new_m', 'new_m_w_in': 'new_m', 'new_m_b_forget': 'new_m', 'new_m_q_norm_g': 'new_m', 'new_m_k_norm_g': 'new_m', 'new_m_conv_w': 'new_m', 'new_m_conv_b': 'new_m', 'new_m_conv_ln_g': 'new_m', 'new_m_conv_ln_b': 'new_m', 'new_m_w_out': 'new_m', 'new_v_norm_g': 'new_v', 'new_v_w_in': 'new_v', 'new_v_b_forget': 'new_v', 'new_v_q_norm_g': 'new_v', 'new_v_k_norm_g': 'new_v', 'new_v_conv_w': 'new_v', 'new_v_conv_b': 'new_v', 'new_v_conv_ln_g': 'new_v', 'new_v_conv_ln_b': 'new_v', 'new_v_w_out': 'new_v'}


def _forward(args):
    return _fwd_reference(*[args[k] for k in FWD_PARAMS])


def _output_shape():
    out = _jax.eval_shape(lambda: _forward(_fwd_setup_inputs(0)))
    return out.shape, out.dtype

N_MICROBATCH = 1
ADAM_LR = 0.001
ADAM_B1 = 0.9
ADAM_B2 = 0.999
ADAM_EPS = 1e-08
ADAM_WD = 0.01
ADAM_STEP = 10
PER_EXAMPLE_BATCH_AXIS = {'x': 0, 'loss_target': 0}
SHARED_INPUTS = []
_WEIGHT_DTYPES = {'norm_g': _jnp.float32, 'w_in': _jnp.float32, 'b_forget': _jnp.float32, 'q_norm_g': _jnp.float32, 'k_norm_g': _jnp.float32, 'conv_w': _jnp.float32, 'conv_b': _jnp.float32, 'conv_ln_g': _jnp.float32, 'conv_ln_b': _jnp.float32, 'w_out': _jnp.float32}
MOMENT_SCALE = {'norm_g': 3.656014e+00, 'w_in': 5.896644e-02, 'b_forget': 1.335110e+01, 'q_norm_g': 2.288449e-01, 'k_norm_g': 2.291147e-01, 'conv_w': 8.756001e-02, 'conv_b': 5.079216e-01, 'conv_ln_g': 2.395992e+00, 'conv_ln_b': 1.529270e+00, 'w_out': 9.421647e-02}


def _to_microbatches(a, axis):
    t = _jnp.moveaxis(a, axis, 0)
    t = t.reshape((N_MICROBATCH, t.shape[0] // N_MICROBATCH) + t.shape[1:])
    return _jnp.moveaxis(t, 1, axis + 1)


def setup_inputs(seed: int = 0) -> dict:
    inp = _fwd_setup_inputs(seed)
    key = _jax.random.fold_in(_jax.random.key(seed), 7919)
    shape, _ = _output_shape()
    out = dict(inp)
    out["loss_target"] = _jax.random.normal(_jax.random.fold_in(key, 0), shape, _jnp.float32)
    for i, name in enumerate(TWIN_WEIGHTS):
        w = inp[name].astype(_jnp.float32)
        if MOMENT_SCALE is None:
            s = _jnp.sqrt(_jnp.mean(_jnp.square(w)) + 1e-30)
        else:
            s = MOMENT_SCALE[name]
        km, kv = _jax.random.split(_jax.random.fold_in(key, i + 1))
        out[name] = w
        out["m_" + name] = s * _jax.random.normal(km, w.shape, _jnp.float32)
        out["v_" + name] = (s * s) * _jax.random.uniform(kv, w.shape, _jnp.float32, 0.5, 1.5)
    if N_MICROBATCH > 1:
        for name, axis in PER_EXAMPLE_BATCH_AXIS.items():
            out[name] = _to_microbatches(out[name], axis)
    return {'x': out['x'], 'norm_g': out['norm_g'], 'w_in': out['w_in'], 'b_forget': out['b_forget'], 'q_norm_g': out['q_norm_g'], 'k_norm_g': out['k_norm_g'], 'conv_w': out['conv_w'], 'conv_b': out['conv_b'], 'conv_ln_g': out['conv_ln_g'], 'conv_ln_b': out['conv_ln_b'], 'w_out': out['w_out'], 'loss_target': out['loss_target'], 'm_norm_g': out['m_norm_g'], 'm_w_in': out['m_w_in'], 'm_b_forget': out['m_b_forget'], 'm_q_norm_g': out['m_q_norm_g'], 'm_k_norm_g': out['m_k_norm_g'], 'm_conv_w': out['m_conv_w'], 'm_conv_b': out['m_conv_b'], 'm_conv_ln_g': out['m_conv_ln_g'], 'm_conv_ln_b': out['m_conv_ln_b'], 'm_w_out': out['m_w_out'], 'v_norm_g': out['v_norm_g'], 'v_w_in': out['v_w_in'], 'v_b_forget': out['v_b_forget'], 'v_q_norm_g': out['v_q_norm_g'], 'v_k_norm_g': out['v_k_norm_g'], 'v_conv_w': out['v_conv_w'], 'v_conv_b': out['v_conv_b'], 'v_conv_ln_g': out['v_conv_ln_g'], 'v_conv_ln_b': out['v_conv_ln_b'], 'v_w_out': out['v_w_out']}


def _loss(weights, diff, rest, loss_target):
    with _jax.named_scope("forward"):
        args = {**rest, TWIN_DIFF_INPUT: diff, **{k: w.astype(_WEIGHT_DTYPES[k]) for k, w in weights.items()}}
        y = _forward(args)
    with _jax.named_scope("loss_head"):
        err = _jnp.square(y.astype(_jnp.float32) - loss_target)
        return 0.5 * _jnp.sum(_jnp.mean(err, axis=-1)) if err.ndim else 0.5 * err


def _adamw(w, g, m, v):
    m = ADAM_B1 * m + (1.0 - ADAM_B1) * g
    v = ADAM_B2 * v + (1.0 - ADAM_B2) * _jnp.square(g)
    m_hat = m / (1.0 - ADAM_B1 ** ADAM_STEP)
    v_hat = v / (1.0 - ADAM_B2 ** ADAM_STEP)
    delta = -ADAM_LR * (m_hat / (_jnp.sqrt(v_hat) + ADAM_EPS) + ADAM_WD * w)
    return delta, m, v


def reference(x, norm_g, w_in, b_forget, q_norm_g, k_norm_g, conv_w, conv_b, conv_ln_g, conv_ln_b, w_out, loss_target, m_norm_g, m_w_in, m_b_forget, m_q_norm_g, m_k_norm_g, m_conv_w, m_conv_b, m_conv_ln_g, m_conv_ln_b, m_w_out, v_norm_g, v_w_in, v_b_forget, v_q_norm_g, v_k_norm_g, v_conv_w, v_conv_b, v_conv_ln_g, v_conv_ln_b, v_w_out):
    given = dict(x=x, norm_g=norm_g, w_in=w_in, b_forget=b_forget, q_norm_g=q_norm_g, k_norm_g=k_norm_g, conv_w=conv_w, conv_b=conv_b, conv_ln_g=conv_ln_g, conv_ln_b=conv_ln_b, w_out=w_out, loss_target=loss_target, m_norm_g=m_norm_g, m_w_in=m_w_in, m_b_forget=m_b_forget, m_q_norm_g=m_q_norm_g, m_k_norm_g=m_k_norm_g, m_conv_w=m_conv_w, m_conv_b=m_conv_b, m_conv_ln_g=m_conv_ln_g, m_conv_ln_b=m_conv_ln_b, m_w_out=m_w_out, v_norm_g=v_norm_g, v_w_in=v_w_in, v_b_forget=v_b_forget, v_q_norm_g=v_q_norm_g, v_k_norm_g=v_k_norm_g, v_conv_w=v_conv_w, v_conv_b=v_conv_b, v_conv_ln_g=v_conv_ln_g, v_conv_ln_b=v_conv_ln_b, v_w_out=v_w_out)
    weights = {n: given[n] for n in TWIN_WEIGHTS}
    shared = {n: given[n] for n in SHARED_INPUTS}
    per_example = {n: given[n] for n in ['x']}
    grad_fn = _jax.value_and_grad(_loss, argnums=(0, 1))

    def one_microbatch(ex, loss_target):
        ex = dict(ex)
        diff = ex.pop(TWIN_DIFF_INPUT)
        return grad_fn(weights, diff, {**shared, **ex}, loss_target)

    if N_MICROBATCH == 1:
        loss, (grad_w, grad_x) = one_microbatch(per_example, given["loss_target"])
    else:
        def body(carry, xs):
            loss_sum, grad_sum = carry
            l_k, (gw_k, gx_k) = one_microbatch(xs[0], xs[1])
            with _jax.named_scope("update"):
                return (loss_sum + l_k, _jax.tree.map(_jnp.add, grad_sum, gw_k)), gx_k

        init = (_jnp.zeros((), _jnp.float32), _jax.tree.map(_jnp.zeros_like, weights))
        (loss, grad_w), grad_x = _jax.lax.scan(body, init, (per_example, given["loss_target"]))
    with _jax.named_scope("update"):
        delta_w, new_m, new_v = {}, {}, {}
        for n in TWIN_WEIGHTS:
            delta_w[n], new_m[n], new_v[n] = _adamw(weights[n], grad_w[n], given["m_" + n], given["v_" + n])
    return (loss, grad_x, *[grad_w[n] for n in TWIN_WEIGHTS], *[delta_w[n] for n in TWIN_WEIGHTS],
            *[new_m[n] for n in TWIN_WEIGHTS], *[new_v[n] for n in TWIN_WEIGHTS])
```

```python
import jax
import jax.numpy as jnp
from jax import lax
from jax.experimental import pallas as pl
from jax.experimental.pallas import tpu as pltpu

F32, BF16 = jnp.float32, jnp.bfloat16
EPS = 1e-6
NEG_INF = -1e30
CONV_K = 31
HEAD_DIM = 64
ADAM_LR, ADAM_B1, ADAM_B2, ADAM_EPS, ADAM_WD, ADAM_STEP = 0.001, 0.9, 0.999, 1e-08, 0.01, 10

LANES = 128
HALO = 32
N_DEV = 8
VMEM_LIMIT = 56 * 1024 * 1024

C0, K0, L0 = HEAD_DIM, HEAD_DIM + 3, HEAD_DIM + 6


def _pcall(body, *, name, vmem=VMEM_LIMIT, **kw):
    return pl.pallas_call(body, name=name, compiler_params=pltpu.CompilerParams(vmem_limit_bytes=vmem), **kw)


def _dot(a, b):
    return jnp.dot(a, b, preferred_element_type=F32)


def _dot_nt(a, b):
    return lax.dot_general(a, b, (((1,), (1,)), ((), ())), preferred_element_type=F32)


def _dot_tn(a, b):
    return lax.dot_general(a, b, (((0,), (0,)), ((), ())), preferred_element_type=F32)


def _split3(x):
    a = x.astype(BF16)
    r = x - a.astype(F32)
    b = r.astype(BF16)
    c = (r - b.astype(F32)).astype(BF16)
    return a, b, c


def _sig(x):
    return jax.nn.sigmoid(x)


def _dsilu(x, s):
    return s * (1.0 + x * (1.0 - s))


def _lane_iota(rows):
    return lax.broadcasted_iota(jnp.int32, (rows, LANES), 1)


def _half_sums(v, lane):
    lo = jnp.sum(jnp.where(lane < HEAD_DIM, v, 0.0), axis=-1, keepdims=True)
    hi = jnp.sum(jnp.where(lane >= HEAD_DIM, v, 0.0), axis=-1, keepdims=True)
    return jnp.where(lane < HEAD_DIM, lo, hi)


def _head_to_low(v, e, lane):
    if e:
        v = pltpu.roll(v, HEAD_DIM, 1)
    return jnp.where(lane < HEAD_DIM, v, 0.0)


def _put3(base, lane, first, pieces):
    for n, p in enumerate(pieces):
        base = jnp.where(lane == first + n, p.astype(F32), base)
    return base


def _inproj_fwd(x, g, wm, wf, *, tm, tn):
    T, D = x.shape
    nj = wm.shape[1] // tn

    def body(x_ref, g_ref, wm_ref, wf_ref, z_ref, zf_ref, h_ref, h_scr):
        @pl.when(pl.program_id(1) == 0)
        def _():
            xv = x_ref[...]
            r = lax.rsqrt(jnp.mean(xv * xv, axis=-1, keepdims=True) + EPS)
            h = ((xv * r) * g_ref[...]).astype(BF16)
            h_scr[...] = h
            h_ref[...] = h
            zf_ref[...] = _dot(h, wf_ref[...])

        z_ref[...] = _dot(h_scr[...], wm_ref[...])

    return _pcall(
        body, name="inproj_fwd", grid=(T // tm, nj),
        in_specs=[pl.BlockSpec((tm, D), lambda i, j: (i, 0)), pl.BlockSpec((1, D), lambda i, j: (0, 0)),
                  pl.BlockSpec((D, tn), lambda i, j: (0, j)), pl.BlockSpec((D, LANES), lambda i, j: (0, 0))],
        out_specs=[pl.BlockSpec((tm, tn), lambda i, j: (i, j)), pl.BlockSpec((tm, LANES), lambda i, j: (i, 0)),
                   pl.BlockSpec((tm, D), lambda i, j: (i, 0))],
        out_shape=[jax.ShapeDtypeStruct((T, wm.shape[1]), F32), jax.ShapeDtypeStruct((T, LANES), F32),
                   jax.ShapeDtypeStruct((T, D), BF16)],
        scratch_shapes=[pltpu.VMEM((tm, D), BF16)],
    )(x, g, wm, wf)


def _attn_prep(z, zf, bfp, gq, gk, *, B, S, H, ts):
    FW = H * HEAD_DIM
    nts = S // ts
    scale = HEAD_DIM ** -0.5

    def body(zq_ref, zk_ref, zv_ref, zf_ref, bf_ref, gq_ref, gk_ref, qa_ref, ka_ref, va_ref, carry):
        @pl.when(pl.program_id(1) == 0)
        def _():
            carry[...] = jnp.zeros_like(carry)

        lane = _lane_iota(ts)
        xf = zf_ref[...] + bf_ref[...]
        logf = jnp.minimum(xf, 0.0) - jnp.log1p(jnp.exp(-jnp.abs(xf)))
        logf = jnp.where(lane < H, logf, 0.0)
        ri = lax.broadcasted_iota(jnp.int32, (ts, ts), 0)
        ci = lax.broadcasted_iota(jnp.int32, (ts, ts), 1)
        tri = (ri >= ci).astype(BF16)
        c = carry[...]
        for piece in _split3(logf):
            c = c + _dot(tri, piece)
        carry[...] = c[ts - 1:ts, :]
        c1, c2, c3 = _split3(c)
        cpk = c1.astype(F32) + pltpu.roll(c2.astype(F32), H, 1) + pltpu.roll(c3.astype(F32), 2 * H, 1)
        cq, ck = cpk.astype(BF16), (-cpk).astype(BF16)
        prow = lax.broadcasted_iota(jnp.int32, (LANES, LANES), 0)
        pcol = lax.broadcasted_iota(jnp.int32, (LANES, LANES), 1)
        ones_q = jnp.where((lane >= K0) & (lane < K0 + 3), 1.0, 0.0)
        ones_k = jnp.where(((lane >= C0) & (lane < C0 + 3)) | ((lane >= L0) & (lane < L0 + 3)), 1.0, 0.0)
        ones_v = jnp.where((lane >= C0) & (lane < C0 + 3), 1.0, 0.0)
        for h in range(H):
            p, e = divmod(h, 2)
            cols = slice(p * LANES, (p + 1) * LANES)
            piece = jnp.where(prow == h, 0, jnp.where(prow == H + h, 1, jnp.where(prow == 2 * H + h, 2, -LANES)))
            aug_q = _dot(cq, (pcol == C0 + piece).astype(BF16)) + ones_q
            aug_k = _dot(ck, (pcol == K0 + piece).astype(BF16)) + ones_k
            for z_ref, g_ref, o_ref, aug, mul in ((zq_ref, gq_ref, qa_ref, aug_q, scale), (zk_ref, gk_ref, ka_ref, aug_k, 1.0)):
                blk = z_ref[:, cols]
                r = lax.rsqrt(_half_sums(blk * blk, lane) * (1.0 / HEAD_DIM) + EPS)
                y = ((blk * r) * g_ref[:, cols]).astype(BF16).astype(F32) * mul
                o_ref[0, h] = (_head_to_low(y, e, lane) + aug).astype(BF16)
            va_ref[0, h] = (_head_to_low(zv_ref[:, cols], e, lane) + ones_v).astype(BF16)

    sec = lambda n: pl.BlockSpec((ts, FW), lambda b, i: (b * nts + i, n))
    row = lambda w: pl.BlockSpec((1, w), lambda b, i: (0, 0))
    head = pl.BlockSpec((1, H, ts, LANES), lambda b, i: (b, 0, i, 0))
    hshape = jax.ShapeDtypeStruct((B, H, S, LANES), BF16)
    return _pcall(
        body, name="attn_prep", grid=(B, nts),
        in_specs=[sec(0), sec(1), sec(2), pl.BlockSpec((ts, LANES), lambda b, i: (b * nts + i, 0)), row(LANES), row(FW), row(FW)],
        out_specs=[head, head, head], out_shape=[hshape, hshape, hshape],
        scratch_shapes=[pltpu.VMEM((1, LANES), F32)],
    )(z, z, z, zf, bfp, gq, gk)


def _attn_fwd(qa, ka, va, *, t):
    B, H, S, _ = qa.shape
    n = S // t

    def body(q_ref, k_ref, v_ref, a_ref, qb_ref):
        i = pl.program_id(2)
        lane = _lane_iota(t)
        row = lax.broadcasted_iota(jnp.int32, (t, t), 0)
        col = lax.broadcasted_iota(jnp.int32, (t, t), 1)
        outs = []
        for e in range(2):
            q = q_ref[0, e]

            def block(j, m, acc, masked):
                rows = pl.ds(pl.multiple_of(j * t, t), t)
                s = _dot_nt(q, k_ref[0, e, rows, :])
                if masked:
                    s = jnp.where(col <= row, s, NEG_INF)
                m_new = jnp.maximum(m, jnp.max(s, axis=-1, keepdims=True))
                p = jnp.exp(s - m_new)
                acc = jnp.exp(m - m_new) * acc + _dot(p.astype(BF16), v_ref[0, e, rows, :])
                return m_new, acc

            init = (jnp.full((t, 1), NEG_INF, F32), jnp.zeros((t, LANES), F32))
            m, acc = lax.fori_loop(0, i, lambda j, c: block(j, c[0], c[1], False), init)
            m, acc = block(i, m, acc, True)
            l = jnp.sum(jnp.where(lane == C0, acc, 0.0), axis=-1, keepdims=True)
            outs.append(acc / l)
            qb_ref[0, e] = _put3(q.astype(F32), lane, L0, _split3(-(m + jnp.log(l)))).astype(BF16)
        a_ref[...] = jnp.where(lane < HEAD_DIM, outs[0], pltpu.roll(outs[1], HEAD_DIM, 1))

    return _pcall(
        body, name="attn_fwd", grid=(B, H // 2, n),
        in_specs=[pl.BlockSpec((1, 2, t, LANES), lambda b, p, i: (b, p, i, 0)),
                  pl.BlockSpec((1, 2, S, LANES), lambda b, p, i: (b, p, 0, 0)),
                  pl.BlockSpec((1, 2, S, LANES), lambda b, p, i: (b, p, 0, 0))],
        out_specs=[pl.BlockSpec((t, LANES), lambda b, p, i: (b * n + i, p)),
                   pl.BlockSpec((1, 2, t, LANES), lambda b, p, i: (b, p, i, 0))],
        out_shape=[jax.ShapeDtypeStruct((B * S, H * HEAD_DIM), F32), jax.ShapeDtypeStruct((B, H, S, LANES), BF16)],
    )(qa, ka, va)


def _conv_fwd(z, cw, cb, *, B, S, sec_w, rc):
    C = cw.shape[1]
    nct = C // LANES
    off_a, off_b = 4 * sec_w // LANES, 5 * sec_w // LANES

    def body(ga_ref, gb_ref, w_ref, b_ref, uc_ref, buf):
        buf[0:HALO, :] = jnp.zeros((HALO, LANES), F32)
        buf[HALO:HALO + S, :] = ga_ref[...] * _sig(gb_ref[...])

        def chunk(cidx, carry):
            r0 = pl.multiple_of(cidx * rc, rc)
            acc = jnp.zeros((rc, LANES), F32)
            for j in range(CONV_K):
                acc = acc + w_ref[j:j + 1, :] * buf[pl.ds(r0 + (HALO - CONV_K + 1 + j), rc), :]
            uc_ref[pl.ds(r0, rc), :] = acc + b_ref[...]
            return carry

        lax.fori_loop(0, S // rc, chunk, 0)

    return _pcall(
        body, name="conv_fwd", grid=(B, nct),
        in_specs=[pl.BlockSpec((S, LANES), lambda b, c: (b, off_a + c)), pl.BlockSpec((S, LANES), lambda b, c: (b, off_b + c)),
                  pl.BlockSpec((CONV_K, LANES), lambda b, c: (0, c)), pl.BlockSpec((1, LANES), lambda b, c: (0, c))],
        out_specs=pl.BlockSpec((S, LANES), lambda b, c: (b, c)),
        out_shape=jax.ShapeDtypeStruct((B * S, C), F32),
        scratch_shapes=[pltpu.VMEM((HALO + S, LANES), F32)],
    )(z, z, cw, cb)


def _ln_parts(uc, lg, lb):
    mu = jnp.mean(uc, axis=-1, keepdims=True)
    d = uc - mu
    rs = lax.rsqrt(jnp.mean(d * d, axis=-1, keepdims=True) + EPS)
    un = d * rs
    return rs, un, un * lg + lb


def _conv_post(uc, z, lg, lb, *, tm, sec_w):
    T, C = uc.shape
    off_c = 6 * sec_w // C

    def body(uc_ref, gc_ref, lg_ref, lb_ref, uo_ref):
        _, _, ul = _ln_parts(uc_ref[...], lg_ref[...], lb_ref[...])
        gc = gc_ref[...]
        uo_ref[...] = ((ul * _sig(ul)) * (gc * _sig(gc))).astype(BF16)

    return _pcall(
        body, name="conv_post", grid=(T // tm,),
        in_specs=[pl.BlockSpec((tm, C), lambda i: (i, 0)), pl.BlockSpec((tm, C), lambda i: (i, off_c)),
                  pl.BlockSpec((1, C), lambda i: (0, 0)), pl.BlockSpec((1, C), lambda i: (0, 0))],
        out_specs=pl.BlockSpec((tm, C), lambda i: (i, 0)), out_shape=jax.ShapeDtypeStruct((T, C), BF16),
    )(uc, z, lg, lb)


def _outproj(a, z, uo, x, tgt, wo, *, tm, sec_w):
    T, D = x.shape
    FW = a.shape[1]
    C = uo.shape[1]

    def body(a_ref, gf_ref, uo_ref, x_ref, t_ref, wo_ref, dout_ref, dy_ref, dwo_ref, loss_ref):
        @pl.when(pl.program_id(0) == 0)
        def _():
            dwo_ref[...] = jnp.zeros_like(dwo_ref)
            loss_ref[...] = jnp.zeros_like(loss_ref)

        gf = gf_ref[...]
        ao = (a_ref[...] * (gf * _sig(gf))).astype(BF16)
        uo = uo_ref[...]
        out = x_ref[...] + (_dot(ao, wo_ref[0:FW, :]) + _dot(uo, wo_ref[FW:FW + C, :]))
        err = out - t_ref[...]
        loss_ref[...] += jnp.sum(err * err) * (0.5 / D)
        dout = err * (1.0 / D)
        dout_ref[...] = dout
        db = dout.astype(BF16)
        dy_ref[:, 0:FW] = _dot_nt(db, wo_ref[0:FW, :])
        dy_ref[:, FW:FW + C] = _dot_nt(db, wo_ref[FW:FW + C, :])
        dwo_ref[0:FW, :] += _dot_tn(ao, db)
        dwo_ref[FW:FW + C, :] += _dot_tn(uo, db)

    tok = lambda w, n=0: pl.BlockSpec((tm, w), lambda i: (i, n))
    return _pcall(
        body, name="outproj", grid=(T // tm,),
        in_specs=[tok(FW), tok(sec_w, 3), tok(C), tok(D), tok(D), pl.BlockSpec((FW + C, D), lambda i: (0, 0))],
        out_specs=[tok(D), tok(FW + C), pl.BlockSpec((FW + C, D), lambda i: (0, 0)), pl.BlockSpec((8, LANES), lambda i: (0, 0))],
        out_shape=[jax.ShapeDtypeStruct((T, D), F32), jax.ShapeDtypeStruct((T, FW + C), F32),
                   jax.ShapeDtypeStruct((FW + C, D), F32), jax.ShapeDtypeStruct((8, LANES), F32)],
    )(a, z, uo, x, tgt, wo)


def _conv_post_bwd(dy, uc, z, lg, lb, *, tm, sec_w):
    T, C = uc.shape
    FW = dy.shape[1] - C
    off_c = 6 * sec_w // C

    def body(dy_ref, uc_ref, gc_ref, lg_ref, lb_ref, dgc_ref, duc_ref, small_ref):
        @pl.when(pl.program_id(0) == 0)
        def _():
            small_ref[...] = jnp.zeros_like(small_ref)

        lg = lg_ref[...]
        rs, un, ul = _ln_parts(uc_ref[...], lg, lb_ref[...])
        s_ul = _sig(ul)
        gc = gc_ref[...]
        s_gc = _sig(gc)
        duo = dy_ref[...]
        dgc_ref[...] = (duo * (ul * s_ul) * _dsilu(gc, s_gc)).astype(BF16)
        dul = duo * (gc * s_gc) * _dsilu(ul, s_ul)
        dun = dul * lg
        duc = rs * (dun - jnp.mean(dun, axis=-1, keepdims=True) - un * jnp.mean(dun * un, axis=-1, keepdims=True))
        duc_ref[...] = duc
        small_ref[0:1, :] += jnp.sum(dul * un, axis=0, keepdims=True)
        small_ref[1:2, :] += jnp.sum(dul, axis=0, keepdims=True)
        small_ref[2:3, :] += jnp.sum(duc, axis=0, keepdims=True)

    tok = lambda n: pl.BlockSpec((tm, C), lambda i: (i, n))
    row = pl.BlockSpec((1, C), lambda i: (0, 0))
    return _pcall(
        body, name="conv_post_bwd", grid=(T // tm,),
        in_specs=[tok(FW // C), tok(0), tok(off_c), row, row],
        out_specs=[tok(0), tok(0), pl.BlockSpec((8, C), lambda i: (0, 0))],
        out_shape=[jax.ShapeDtypeStruct((T, C), BF16), jax.ShapeDtypeStruct((T, C), F32), jax.ShapeDtypeStruct((8, C), F32)],
    )(dy, uc, z, lg, lb)


def _conv_bwd(duc, z, cw, *, B, S, sec_w, rc):
    C = cw.shape[1]
    nct = C // LANES
    off_a, off_b = 4 * sec_w // LANES, 5 * sec_w // LANES

    def body(duc_ref, ga_ref, gb_ref, w_ref, dga_ref, dgb_ref, dw_ref, ubuf, dbuf):
        @pl.when(pl.program_id(1) == 0)
        def _():
            dw_ref[...] = jnp.zeros_like(dw_ref)

        ubuf[0:HALO, :] = jnp.zeros((HALO, LANES), F32)
        ubuf[HALO:HALO + S, :] = ga_ref[...] * _sig(gb_ref[...])
        dbuf[0:S, :] = duc_ref[...]
        dbuf[S:S + HALO, :] = jnp.zeros((HALO, LANES), F32)

        def chunk(cidx, carry):
            r0 = pl.multiple_of(cidx * rc, rc)
            acc = jnp.zeros((rc, LANES), F32)
            for j in range(CONV_K):
                acc = acc + w_ref[j:j + 1, :] * dbuf[pl.ds(r0 + (CONV_K - 1 - j), rc), :]
            rows = pl.ds(r0, rc)
            ga, sg = ga_ref[rows, :], _sig(gb_ref[rows, :])
            dga_ref[rows, :] = (acc * sg).astype(BF16)
            dgb_ref[rows, :] = (acc * ga * sg * (1.0 - sg)).astype(BF16)
            return carry

        lax.fori_loop(0, S // rc, chunk, 0)
        d = duc_ref[...]
        for j in range(CONV_K):
            dw_ref[j:j + 1, :] += jnp.sum(d * ubuf[pl.ds(HALO - CONV_K + 1 + j, S), :], axis=0, keepdims=True)

    return _pcall(
        body, name="conv_bwd", grid=(nct, B),
        in_specs=[pl.BlockSpec((S, LANES), lambda c, b: (b, c)), pl.BlockSpec((S, LANES), lambda c, b: (b, off_a + c)),
                  pl.BlockSpec((S, LANES), lambda c, b: (b, off_b + c)), pl.BlockSpec((CONV_K, LANES), lambda c, b: (0, c))],
        out_specs=[pl.BlockSpec((S, LANES), lambda c, b: (b, c)), pl.BlockSpec((S, LANES), lambda c, b: (b, c)),
                   pl.BlockSpec((CONV_K, LANES), lambda c, b: (0, c))],
        out_shape=[jax.ShapeDtypeStruct((B * S, C), BF16), jax.ShapeDtypeStruct((B * S, C), BF16),
                   jax.ShapeDtypeStruct((CONV_K, C), F32)],
        scratch_shapes=[pltpu.VMEM((HALO + S, LANES), F32), pltpu.VMEM((S + HALO, LANES), F32)],
    )(duc, z, z, cw)


def _attn_bwd_prep(dy, a, z, *, B, S, H, ts, sec_w):
    FW = H * HEAD_DIM
    nts = S // ts

    def body(dy_ref, a_ref, gf_ref, dgf_ref, do_ref):
        lane = _lane_iota(ts)
        for p in range(H // 2):
            cols = slice(p * LANES, (p + 1) * LANES)
            gf, av, dya = gf_ref[:, cols], a_ref[:, cols], dy_ref[:, cols]
            sg = _sig(gf)
            dgf_ref[:, cols] = (dya * av * _dsilu(gf, sg)).astype(BF16)
            da = dya * (gf * sg)
            nd = -_half_sums(da * av, lane)
            for e in range(2):
                nd_e = nd[:, e * HEAD_DIM:e * HEAD_DIM + 1]
                do_ref[0, 2 * p + e] = _put3(_head_to_low(da, e, lane), lane, C0, _split3(nd_e)).astype(BF16)

    tok = lambda n: pl.BlockSpec((ts, FW), lambda b, i: (b * nts + i, n))
    return _pcall(
        body, name="attn_bwd_prep", grid=(B, nts),
        in_specs=[tok(0), tok(0), pl.BlockSpec((ts, sec_w), lambda b, i: (b * nts + i, 3))],
        out_specs=[tok(0), pl.BlockSpec((1, H, ts, LANES), lambda b, i: (b, 0, i, 0))],
        out_shape=[jax.ShapeDtypeStruct((B * S, FW), BF16), jax.ShapeDtypeStruct((B, H, S, LANES), BF16)],
    )(dy, a, z)


def _attn_bwd(qb, ka, va, doa, *, t):
    B, H, S, _ = qb.shape
    n = S // t

    def body(q_ref, k_ref, v_ref, do_ref, dq_ref, dk_ref, dv_ref):
        j = pl.program_id(2)

        @pl.when(j == 0)
        def _():
            dq_ref[...] = jnp.zeros_like(dq_ref)

        row = lax.broadcasted_iota(jnp.int32, (t, t), 0)
        col = lax.broadcasted_iota(jnp.int32, (t, t), 1)
        k, v = k_ref[0, 0], v_ref[0, 0]

        def block(i, dk, dv, masked):
            rows = pl.ds(pl.multiple_of(i * t, t), t)
            q, do = q_ref[0, 0, rows, :], do_ref[0, 0, rows, :]
            st = _dot_nt(k, q)
            dpt = _dot_nt(v, do)
            if masked:
                st = jnp.where(row <= col, st, NEG_INF)
            pt = jnp.exp(st)
            dst = pt * dpt
            dsb = dst.astype(BF16)
            dq_ref[0, 0, rows, :] += _dot_tn(dsb, k)
            return dk + _dot(dsb, q), dv + _dot(pt.astype(BF16), do)

        zero = jnp.zeros((t, LANES), F32)
        dk, dv = block(j, zero, zero, True)
        dk, dv = lax.fori_loop(j + 1, n, lambda i, c: block(i, c[0], c[1], False), (dk, dv))
        dk_ref[0, 0] = dk
        dv_ref[0, 0] = dv

    full = pl.BlockSpec((1, 1, S, LANES), lambda b, h, j: (b, h, 0, 0))
    tile = pl.BlockSpec((1, 1, t, LANES), lambda b, h, j: (b, h, j, 0))
    shape = jax.ShapeDtypeStruct((B, H, S, LANES), F32)
    return _pcall(
        body, name="attn_bwd", grid=(B, H, n),
        in_specs=[full, tile, tile, full], out_specs=[full, tile, tile], out_shape=[shape, shape, shape],
    )(qb, ka, va, doa)


def _attn_post_bwd(dq, dk, dv, z, zf, bfp, gq, gk, *, B, S, H, ts):
    FW = H * HEAD_DIM
    nts = S // ts
    scale = HEAD_DIM ** -0.5

    def body(dq_ref, dk_ref, dv_ref, zq_ref, zk_ref, zf_ref, bf_ref, gq_ref, gk_ref,
             dzq_ref, dzk_ref, dzv_ref, dzf_ref, small_ref, carry):
        first = (pl.program_id(0) == 0) & (pl.program_id(1) == 0)

        @pl.when(first)
        def _():
            small_ref[...] = jnp.zeros_like(small_ref)

        @pl.when(pl.program_id(1) == 0)
        def _():
            carry[...] = jnp.zeros_like(carry)

        lane = _lane_iota(ts)
        dcs = jnp.zeros((ts, LANES), F32)
        for p in range(H // 2):
            cols = slice(p * LANES, (p + 1) * LANES)
            for d_ref, z_ref, g_ref, o_ref, srow, mul in ((dq_ref, zq_ref, gq_ref, dzq_ref, 0, scale),
                                                          (dk_ref, zk_ref, gk_ref, dzk_ref, 1, 1.0)):
                dy = jnp.where(lane < HEAD_DIM, d_ref[0, 2 * p], pltpu.roll(d_ref[0, 2 * p + 1], HEAD_DIM, 1)) * mul
                xv, g = z_ref[:, cols], g_ref[:, cols]
                r = lax.rsqrt(_half_sums(xv * xv, lane) * (1.0 / HEAD_DIM) + EPS)
                dxh = dy * g
                mm = _half_sums(dxh * xv, lane) * (1.0 / HEAD_DIM)
                o_ref[:, cols] = (r * (dxh - xv * (r * r) * mm)).astype(BF16)
                small_ref[srow:srow + 1, cols] += jnp.sum(dy * xv * r, axis=0, keepdims=True)
            dzv_ref[:, cols] = jnp.where(lane < HEAD_DIM, dv_ref[0, 2 * p], pltpu.roll(dv_ref[0, 2 * p + 1], HEAD_DIM, 1)).astype(BF16)
            for e in range(2):
                h = 2 * p + e
                dc = jnp.sum(jnp.where(lane == C0, dq_ref[0, h], 0.0) - jnp.where(lane == K0, dk_ref[0, h], 0.0),
                             axis=-1, keepdims=True)
                dcs = jnp.where(lane == h, dc, dcs)
        ri = lax.broadcasted_iota(jnp.int32, (ts, ts), 0)
        ci = lax.broadcasted_iota(jnp.int32, (ts, ts), 1)
        tri = (ci >= ri).astype(BF16)
        dl = carry[...]
        for piece in _split3(dcs):
            dl = dl + _dot(tri, piece)
        carry[...] = dl[0:1, :]
        xf = zf_ref[...] + bf_ref[...]
        df = jnp.where(lane < H, dl * _sig(-xf), 0.0)
        dzf_ref[...] = df.astype(BF16)
        small_ref[2:3, 0:LANES] += jnp.sum(df, axis=0, keepdims=True)

    rev = lambda b, i: (b, 0, nts - 1 - i, 0)
    head = pl.BlockSpec((1, H, ts, LANES), rev)
    tokrow = lambda b, i: b * nts + nts - 1 - i
    sec = lambda n: pl.BlockSpec((ts, FW), lambda b, i: (tokrow(b, i), n))
    narrow = pl.BlockSpec((ts, LANES), lambda b, i: (tokrow(b, i), 0))
    row = lambda w: pl.BlockSpec((1, w), lambda b, i: (0, 0))
    return _pcall(
        body, name="attn_post_bwd", grid=(B, nts),
        in_specs=[head, head, head, sec(0), sec(1), narrow, row(LANES), row(FW), row(FW)],
        out_specs=[sec(0), sec(0), sec(0), narrow, pl.BlockSpec((8, FW), lambda b, i: (0, 0))],
        out_shape=[jax.ShapeDtypeStruct((B * S, FW), BF16)] * 3 + [jax.ShapeDtypeStruct((B * S, LANES), BF16),
                                                                   jax.ShapeDtypeStruct((8, FW), F32)],
        scratch_shapes=[pltpu.VMEM((1, LANES), F32)],
    )(dq, dk, dv, z, z, zf, bfp, gq, gk)


def _inproj_bwd_x(dz, dzf, wm, wf, x, g, dout, *, tm, tn):
    T, D = x.shape
    nj = wm.shape[1] // tn

    def body(dz_ref, dzf_ref, wm_ref, wf_ref, x_ref, g_ref, dout_ref, gx_ref, dg_ref, acc):
        i, j = pl.program_id(0), pl.program_id(1)

        @pl.when((i == 0) & (j == 0))
        def _():
            dg_ref[...] = jnp.zeros_like(dg_ref)

        @pl.when(j == 0)
        def _():
            acc[...] = _dot_nt(dzf_ref[...], wf_ref[...])

        acc[...] += _dot_nt(dz_ref[...], wm_ref[...])

        @pl.when(j == nj - 1)
        def _():
            dh, xv, g = acc[...], x_ref[...], g_ref[...]
            r = lax.rsqrt(jnp.mean(xv * xv, axis=-1, keepdims=True) + EPS)
            dxh = dh * g
            mm = jnp.mean(dxh * xv, axis=-1, keepdims=True)
            gx_ref[...] = dout_ref[...] + r * (dxh - xv * (r * r) * mm)
            dg_ref[0:1, :] += jnp.sum(dh * xv * r, axis=0, keepdims=True)

    tok = pl.BlockSpec((tm, D), lambda i, j: (i, 0))
    return _pcall(
        body, name="inproj_bwd_x", grid=(T // tm, nj),
        in_specs=[pl.BlockSpec((tm, tn), lambda i, j: (i, j)), pl.BlockSpec((tm, LANES), lambda i, j: (i, 0)),
                  pl.BlockSpec((D, tn), lambda i, j: (0, j)), pl.BlockSpec((D, LANES), lambda i, j: (0, 0)),
                  tok, pl.BlockSpec((1, D), lambda i, j: (0, 0)), tok],
        out_specs=[tok, pl.BlockSpec((8, D), lambda i, j: (0, 0))],
        out_shape=[jax.ShapeDtypeStruct((T, D), F32), jax.ShapeDtypeStruct((8, D), F32)],
        scratch_shapes=[pltpu.VMEM((tm, D), F32)],
    )(dz, dzf, wm, wf, x, g, dout)


def _inproj_bwd_w(h, dz, dzf, *, tm, tn):
    T, D = h.shape
    nj = dz.shape[1] // tn
    ni = T // tm

    def body(h_ref, dz_ref, dzf_ref, dwm_ref, dwf_ref):
        i, j = pl.program_id(1), pl.program_id(0)

        @pl.when(i == 0)
        def _():
            dwm_ref[...] = jnp.zeros_like(dwm_ref)

        @pl.when((i == 0) & (j == 0))
        def _():
            dwf_ref[...] = jnp.zeros_like(dwf_ref)

        hv = h_ref[...]
        dwm_ref[...] += _dot_tn(hv, dz_ref[...])

        @pl.when(j == 0)
        def _():
            dwf_ref[...] += _dot_tn(hv, dzf_ref[...])

    return _pcall(
        body, name="inproj_bwd_w", grid=(nj, ni),
        in_specs=[pl.BlockSpec((tm, D), lambda j, i: (i, 0)), pl.BlockSpec((tm, tn), lambda j, i: (i, j)),
                  pl.BlockSpec((tm, LANES), lambda j, i: (i, 0))],
        out_specs=[pl.BlockSpec((D, tn), lambda j, i: (0, j)), pl.BlockSpec((D, LANES), lambda j, i: (0, 0))],
        out_shape=[jax.ShapeDtypeStruct((D, dz.shape[1]), F32), jax.ShapeDtypeStruct((D, LANES), F32)],
    )(h, dz, dzf)


def _adamw(w, g, m, v):
    m = ADAM_B1 * m + (1.0 - ADAM_B1) * g
    v = ADAM_B2 * v + (1.0 - ADAM_B2) * (g * g)
    m_hat = m / (1.0 - ADAM_B1 ** ADAM_STEP)
    v_hat = v / (1.0 - ADAM_B2 ** ADAM_STEP)
    return -ADAM_LR * (m_hat / (jnp.sqrt(v_hat) + ADAM_EPS) + ADAM_WD * w), m, v


def _adam_update(w, m, v, parts, *, name, tr):
    R, Cc = w.shape

    def body(w_ref, m_ref, v_ref, p_ref, g_ref, d_ref, nm_ref, nv_ref):
        g = p_ref[0].astype(F32)
        for k in range(1, N_DEV):
            g = g + p_ref[k].astype(F32)
        g_ref[...] = g
        d_ref[...], nm_ref[...], nv_ref[...] = _adamw(w_ref[...], g, m_ref[...], v_ref[...])

    blk = pl.BlockSpec((tr, Cc), lambda i: (i, 0))
    shape = jax.ShapeDtypeStruct((R, Cc), F32)
    return _pcall(
        body, name=name, grid=(R // tr,),
        in_specs=[blk, blk, blk, pl.BlockSpec((N_DEV, tr, Cc), lambda i: (0, i, 0))],
        out_specs=[blk, blk, blk, blk], out_shape=[shape, shape, shape, shape],
    )(w, m, v, parts)


MESH = pl.DeviceIdType.MESH
ANY = pl.BlockSpec(memory_space=pl.ANY)


def _flip(v, bit):
    return 1 - v if bit else v


def _allgather(shards):
    n = len(shards)

    def body(*refs):
        srcs, outs = refs[:n], refs[n:2 * n]
        send_sems, recv_sems, local_sems = refs[2 * n:]
        x, y, c = lax.axis_index("x"), lax.axis_index("y"), lax.axis_index("c")
        sibling = (x, y, 1 - c)
        chips = [(1 - x, y), (x, 1 - y), (1 - x, 1 - y)]
        slot = lambda px, py, pc: 4 * px + 2 * py + pc

        def copy(a, k, block, to, src=None):
            dst = outs[a].at[slot(*block)]
            return pltpu.make_async_remote_copy(src_ref=dst if src is None else src, dst_ref=dst, send_sem=send_sems.at[7 * a + k],
                                                recv_sem=recv_sems.at[7 * a + k], device_id=to, device_id_type=MESH)

        mine = [pltpu.make_async_copy(srcs[a], outs[a].at[slot(x, y, c)], local_sems.at[a]) for a in range(n)]
        sends = []
        for a in range(n):
            mine[a].start()
            sends.append(copy(a, 0, (x, y, c), sibling, src=srcs[a]))
            sends += [copy(a, 1 + j, (x, y, c), (*chip, c), src=srcs[a]) for j, chip in enumerate(chips)]
        for cp in sends:
            cp.start()
        for a in range(n):
            for j, chip in enumerate(chips):
                copy(a, 1 + j, (*chip, c), (x, y, c)).wait_recv()
                fwd = copy(a, 4 + j, (*chip, c), sibling)
                fwd.start()
                sends.append(fwd)
        for a in range(n):
            copy(a, 0, (x, y, 1 - c), (x, y, c)).wait_recv()
            for j, chip in enumerate(chips):
                copy(a, 4 + j, (*chip, 1 - c), (x, y, c)).wait_recv()
        for cp in sends:
            cp.wait_send()
        for cp in mine:
            cp.wait()

    return pl.pallas_call(
        body, name="allgather_weights", in_specs=[ANY] * n, out_specs=[ANY] * n,
        out_shape=[jax.ShapeDtypeStruct((N_DEV,) + s.shape, s.dtype) for s in shards],
        scratch_shapes=[pltpu.SemaphoreType.DMA((7 * n,)), pltpu.SemaphoreType.DMA((7 * n,)), pltpu.SemaphoreType.DMA((n,))],
    )(*shards)


def _exchange(scatter, gather):
    arrs = list(scatter) + list(gather)
    n, ns = len(arrs), len(scatter)

    def body(*refs):
        srcs, outs = refs[:n], refs[n:2 * n]
        send_sems, recv_sems, local_sems = refs[2 * n:]
        x, y, c = lax.axis_index("x"), lax.axis_index("y"), lax.axis_index("c")
        me = 4 * x + 2 * y + c
        pending = []
        for a in range(n):
            piece = lambda k: srcs[a].at[k] if a < ns else srcs[a]
            loc = pltpu.make_async_copy(piece(me), outs[a].at[me], local_sems.at[a])
            loc.start()
            pending.append(loc.wait)
            for d in range(1, N_DEV):
                peer = (_flip(x, d & 4), _flip(y, d & 2), _flip(c, d & 1))
                pid = 4 * peer[0] + 2 * peer[1] + peer[2]
                sem = 7 * a + d - 1
                out = pltpu.make_async_remote_copy(src_ref=piece(pid), dst_ref=outs[a].at[me], send_sem=send_sems.at[sem],
                                                   recv_sem=recv_sems.at[sem], device_id=peer, device_id_type=MESH)
                out.start()
                back = pltpu.make_async_remote_copy(src_ref=piece(me), dst_ref=outs[a].at[pid], send_sem=send_sems.at[sem],
                                                    recv_sem=recv_sems.at[sem], device_id=peer, device_id_type=MESH)
                pending += [out.wait_send, back.wait_recv]
        for wait in pending:
            wait()

    return pl.pallas_call(
        body, name="exchange_grads", in_specs=[ANY] * n, out_specs=[ANY] * n,
        out_shape=[jax.ShapeDtypeStruct(s.shape, s.dtype) for s in scatter]
        + [jax.ShapeDtypeStruct((N_DEV,) + s.shape, s.dtype) for s in gather],
        scratch_shapes=[pltpu.SemaphoreType.DMA((7 * n,)), pltpu.SemaphoreType.DMA((7 * n,)), pltpu.SemaphoreType.DMA((n,))],
    )(*arrs)


def _local_step(x, tgt, norm_g, wm, wf, bfp, gq, gk, cw, cb, lg, lb, wo, *, B, S, H):
    T, D = x.shape
    sec_w = H * HEAD_DIM
    tm = min(512, T)
    ts = min(256, S)
    rc = min(64, S)
    z, zf, h = _inproj_fwd(x, norm_g, wm, wf, tm=tm, tn=sec_w)
    qa, ka, va = _attn_prep(z, zf, bfp, gq, gk, B=B, S=S, H=H, ts=ts)
    a, qb = _attn_fwd(qa, ka, va, t=ts)
    uc = _conv_fwd(z, cw, cb, B=B, S=S, sec_w=sec_w, rc=rc)
    uo = _conv_post(uc, z, lg, lb, tm=ts, sec_w=sec_w)
    dout, dy, dwo, loss = _outproj(a, z, uo, x, tgt, wo, tm=ts, sec_w=sec_w)
    dgc, duc, small_c = _conv_post_bwd(dy, uc, z, lg, lb, tm=ts, sec_w=sec_w)
    dga, dgb, dcw = _conv_bwd(duc, z, cw, B=B, S=S, sec_w=sec_w, rc=rc)
    dgf, doa = _attn_bwd_prep(dy, a, z, B=B, S=S, H=H, ts=ts, sec_w=sec_w)
    dq, dk, dv = _attn_bwd(qb, ka, va, doa, t=ts)
    dzq, dzk, dzv, dzf, small_a = _attn_post_bwd(dq, dk, dv, z, zf, bfp, gq, gk, B=B, S=S, H=H, ts=ts)
    dz = jnp.concatenate([dzq, dzk, dzv, dgf, dga, dgb, dgc], axis=1)
    gx, dg = _inproj_bwd_x(dz, dzf, wm, wf, x, norm_g, dout, tm=tm, tn=sec_w)
    dwm, dwf = _inproj_bwd_w(h, dz, dzf, tm=tm, tn=sec_w)
    small = jnp.concatenate([dg[0:1], small_a[2:3], small_a[0:2], small_c[2:3], small_c[0:2], jnp.zeros((1, D), F32)], axis=0)
    return loss, gx, dwm, dwf, dwo, dcw, small


def kernel(x, norm_g, w_in, b_forget, q_norm_g, k_norm_g, conv_w, conv_b, conv_ln_g, conv_ln_b, w_out, loss_target, m_norm_g, m_w_in, m_b_forget, m_q_norm_g, m_k_norm_g, m_conv_w, m_conv_b, m_conv_ln_g, m_conv_ln_b, m_w_out, v_norm_g, v_w_in, v_b_forget, v_q_norm_g, v_k_norm_g, v_conv_w, v_conv_b, v_conv_ln_g, v_conv_ln_b, v_w_out):
    B, S, D = x.shape
    H = q_norm_g.shape[1]
    FW = H * HEAD_DIM
    assert q_norm_g.shape[2] == HEAD_DIM and FW == D and conv_b.shape[1] == D and H <= LANES // 3
    ncol = w_in.shape[2]
    T = B * S

    wg, wog, cwg = _allgather([w_in[0].astype(BF16), w_out[0].astype(BF16), conv_w[0]])
    w_full = jnp.transpose(wg, (1, 0, 2)).reshape(D, N_DEV * ncol)
    wm = jnp.concatenate([w_full[:, :3 * FW], w_full[:, 3 * FW + H:]], axis=1)
    wf = jnp.pad(w_full[:, 3 * FW:3 * FW + H], ((0, 0), (0, LANES - H)))
    wo = wog.reshape(N_DEV * wog.shape[1], D)
    cw = jnp.transpose(cwg, (1, 0, 2)).reshape(CONV_K, D)
    bfp = jnp.pad(b_forget, ((0, 0), (0, LANES - H)))
    gq, gk = q_norm_g.reshape(1, FW), k_norm_g.reshape(1, FW)

    loss, gx, dwm, dwf, dwo, dcw, small = _local_step(
        x.reshape(T, D), loss_target.reshape(T, D), norm_g, wm, wf, bfp, gq, gk, cw, conv_b, conv_ln_g, conv_ln_b, wo,
        B=B, S=S, H=H)

    dw_full = jnp.concatenate([dwm[:, :3 * FW], dwf[:, :H], dwm[:, 3 * FW:]], axis=1)
    dw_parts = jnp.transpose(dw_full.reshape(D, N_DEV, ncol), (1, 0, 2)).astype(BF16)
    dwo_parts = dwo.reshape(N_DEV, dwo.shape[0] // N_DEV, D).astype(BF16)
    dcw_parts = jnp.transpose(dcw.reshape(CONV_K, N_DEV, D // N_DEV), (1, 0, 2))
    small = jnp.concatenate([small, jnp.broadcast_to(loss[0:1, 0:1], (8, D))], axis=0)
    r_w, r_wo, r_cw, r_small = _exchange([dw_parts, dwo_parts, dcw_parts], [small])

    g_w, d_w, nm_w, nv_w = _adam_update(w_in[0], m_w_in[0], v_w_in[0], r_w, name="adam_w_in", tr=min(128, D))
    g_wo, d_wo, nm_wo, nv_wo = _adam_update(w_out[0], m_w_out[0], v_w_out[0], r_wo, name="adam_w_out", tr=min(128, w_out.shape[1]))
    g_cw, d_cw, nm_cw, nv_cw = _adam_update(conv_w[0], m_conv_w[0], v_conv_w[0], r_cw, name="adam_conv_w", tr=CONV_K)
    pad_b = lambda t: jnp.pad(t, ((0, 0), (0, D - H)))
    pack = lambda nb, bf, q, k, b, g, bb: jnp.concatenate(
        [nb, pad_b(bf), q.reshape(1, FW), k.reshape(1, FW), b, g, bb, jnp.zeros((9, D), F32)], axis=0)
    small_w = pack(norm_g, b_forget, q_norm_g, k_norm_g, conv_b, conv_ln_g, conv_ln_b)
    small_m = pack(m_norm_g, m_b_forget, m_q_norm_g, m_k_norm_g, m_conv_b, m_conv_ln_g, m_conv_ln_b)
    small_v = pack(v_norm_g, v_b_forget, v_q_norm_g, v_k_norm_g, v_conv_b, v_conv_ln_g, v_conv_ln_b)
    g_s, d_s, nm_s, nv_s = _adam_update(small_w, small_m, small_v, r_small, name="adam_small", tr=16)

    def unpack(t):
        return [t[0:1], t[1:2, :H], t[2:3].reshape(1, H, HEAD_DIM), t[3:4].reshape(1, H, HEAD_DIM), t[4:5], t[5:6], t[6:7]]

    def leaves(small_t, w_t, cw_t, wo_t):
        ng, bf, q, k, cb, lg, lb = unpack(small_t)
        return [ng, w_t[None], bf, q, k, cw_t[None], cb, lg, lb, wo_t[None]]

    return (g_s[8, 0], gx.reshape(B, S, D), *leaves(g_s, g_w, g_cw, g_wo), *leaves(d_s, d_w, d_cw, d_wo),
            *leaves(nm_s, nm_w, nm_cw, nm_wo), *leaves(nv_s, nv_w, nv_cw, nv_wo))
```

```python
import jax
import jax.numpy as jnp
from jax import lax
from jax.experimental import pallas as pl
from jax.experimental.pallas import tpu as pltpu

F32, BF16 = jnp.float32, jnp.bfloat16
EPS = 1e-6
NEG_INF = -1e30
CONV_K = 31
HEAD_DIM = 64
ADAM_LR, ADAM_B1, ADAM_B2, ADAM_EPS, ADAM_WD, ADAM_STEP = 0.001, 0.9, 0.999, 1e-08, 0.01, 10

LANES = 128
HALO = 32
N_DEV = 8
VMEM_LIMIT = 56 * 1024 * 1024

C0, K0, L0 = HEAD_DIM, HEAD_DIM + 3, HEAD_DIM + 6


def _pcall(body, *, name, vmem=VMEM_LIMIT, **kw):
    return pl.pallas_call(body, name=name, compiler_params=pltpu.CompilerParams(vmem_limit_bytes=vmem), **kw)


def _dot(a, b):
    return jnp.dot(a, b, preferred_element_type=F32)


def _dot_nt(a, b):
    return lax.dot_general(a, b, (((1,), (1,)), ((), ())), preferred_element_type=F32)


def _dot_tn(a, b):
    return lax.dot_general(a, b, (((0,), (0,)), ((), ())), preferred_element_type=F32)


def _split3(x):
    a = x.astype(BF16)
    r = x - a.astype(F32)
    b = r.astype(BF16)
    c = (r - b.astype(F32)).astype(BF16)
    return a, b, c


def _sig(x):
    return jax.nn.sigmoid(x)


def _dsilu(x, s):
    return s * (1.0 + x * (1.0 - s))


def _lane_iota(rows):
    return lax.broadcasted_iota(jnp.int32, (rows, LANES), 1)


def _half_sums(v, lane):
    lo = jnp.sum(jnp.where(lane < HEAD_DIM, v, 0.0), axis=-1, keepdims=True)
    hi = jnp.sum(jnp.where(lane >= HEAD_DIM, v, 0.0), axis=-1, keepdims=True)
    return jnp.where(lane < HEAD_DIM, lo, hi)


def _head_to_low(v, e, lane):
    if e:
        v = pltpu.roll(v, HEAD_DIM, 1)
    return jnp.where(lane < HEAD_DIM, v, 0.0)


def _put3(base, lane, first, pieces):
    for n, p in enumerate(pieces):
        base = jnp.where(lane == first + n, p.astype(F32), base)
    return base


def _inproj_fwd(x, g, wm, wf, *, tm, tn):
    T, D = x.shape
    nj = wm.shape[1] // tn

    def body(x_ref, g_ref, wm_ref, wf_ref, z_ref, zf_ref, h_ref, h_scr):
        @pl.when(pl.program_id(1) == 0)
        def _():
            xv = x_ref[...]
            r = lax.rsqrt(jnp.mean(xv * xv, axis=-1, keepdims=True) + EPS)
            h = ((xv * r) * g_ref[...]).astype(BF16)
            h_scr[...] = h
            h_ref[...] = h
            zf_ref[...] = _dot(h, wf_ref[...])

        z_ref[...] = _dot(h_scr[...], wm_ref[...])

    return _pcall(
        body, name="inproj_fwd", grid=(T // tm, nj),
        in_specs=[pl.BlockSpec((tm, D), lambda i, j: (i, 0)), pl.BlockSpec((1, D), lambda i, j: (0, 0)),
                  pl.BlockSpec((D, tn), lambda i, j: (0, j)), pl.BlockSpec((D, LANES), lambda i, j: (0, 0))],
        out_specs=[pl.BlockSpec((tm, tn), lambda i, j: (i, j)), pl.BlockSpec((tm, LANES), lambda i, j: (i, 0)),
                   pl.BlockSpec((tm, D), lambda i, j: (i, 0))],
        out_shape=[jax.ShapeDtypeStruct((T, wm.shape[1]), F32), jax.ShapeDtypeStruct((T, LANES), F32),
                   jax.ShapeDtypeStruct((T, D), BF16)],
        scratch_shapes=[pltpu.VMEM((tm, D), BF16)],
    )(x, g, wm, wf)


def _attn_prep(z, zf, bfp, gq, gk, *, B, S, H, ts):
    FW = H * HEAD_DIM
    nts = S // ts
    scale = HEAD_DIM ** -0.5

    def body(zq_ref, zk_ref, zv_ref, zf_ref, bf_ref, gq_ref, gk_ref, qa_ref, ka_ref, va_ref, carry):
        @pl.when(pl.program_id(1) == 0)
        def _():
            carry[...] = jnp.zeros_like(carry)

        lane = _lane_iota(ts)
        xf = zf_ref[...] + bf_ref[...]
        logf = jnp.minimum(xf, 0.0) - jnp.log1p(jnp.exp(-jnp.abs(xf)))
        logf = jnp.where(lane < H, logf, 0.0)
        ri = lax.broadcasted_iota(jnp.int32, (ts, ts), 0)
        ci = lax.broadcasted_iota(jnp.int32, (ts, ts), 1)
        tri = (ri >= ci).astype(BF16)
        c = carry[...]
        for piece in _split3(logf):
            c = c + _dot(tri, piece)
        carry[...] = c[ts - 1:ts, :]
        c1, c2, c3 = _split3(c)
        cpk = c1.astype(F32) + pltpu.roll(c2.astype(F32), H, 1) + pltpu.roll(c3.astype(F32), 2 * H, 1)
        cq, ck = cpk.astype(BF16), (-cpk).astype(BF16)
        prow = lax.broadcasted_iota(jnp.int32, (LANES, LANES), 0)
        pcol = lax.broadcasted_iota(jnp.int32, (LANES, LANES), 1)
        ones_q = jnp.where((lane >= K0) & (lane < K0 + 3), 1.0, 0.0)
        ones_k = jnp.where(((lane >= C0) & (lane < C0 + 3)) | ((lane >= L0) & (lane < L0 + 3)), 1.0, 0.0)
        ones_v = jnp.where((lane >= C0) & (lane < C0 + 3), 1.0, 0.0)
        for h in range(H):
            p, e = divmod(h, 2)
            cols = slice(p * LANES, (p + 1) * LANES)
            piece = jnp.where(prow == h, 0, jnp.where(prow == H + h, 1, jnp.where(prow == 2 * H + h, 2, -LANES)))
            aug_q = _dot(cq, (pcol == C0 + piece).astype(BF16)) + ones_q
            aug_k = _dot(ck, (pcol == K0 + piece).astype(BF16)) + ones_k
            for z_ref, g_ref, o_ref, aug, mul in ((zq_ref, gq_ref, qa_ref, aug_q, scale), (zk_ref, gk_ref, ka_ref, aug_k, 1.0)):
                blk = z_ref[:, cols]
                r = lax.rsqrt(_half_sums(blk * blk, lane) * (1.0 / HEAD_DIM) + EPS)
                y = ((blk * r) * g_ref[:, cols]).astype(BF16).astype(F32) * mul
                o_ref[0, h] = (_head_to_low(y, e, lane) + aug).astype(BF16)
            va_ref[0, h] = (_head_to_low(zv_ref[:, cols], e, lane) + ones_v).astype(BF16)

    sec = lambda n: pl.BlockSpec((ts, FW), lambda b, i: (b * nts + i, n))
    row = lambda w: pl.BlockSpec((1, w), lambda b, i: (0, 0))
    head = pl.BlockSpec((1, H, ts, LANES), lambda b, i: (b, 0, i, 0))
    hshape = jax.ShapeDtypeStruct((B, H, S, LANES), BF16)
    return _pcall(
        body, name="attn_prep", grid=(B, nts),
        in_specs=[sec(0), sec(1), sec(2), pl.BlockSpec((ts, LANES), lambda b, i: (b * nts + i, 0)), row(LANES), row(FW), row(FW)],
        out_specs=[head, head, head], out_shape=[hshape, hshape, hshape],
        scratch_shapes=[pltpu.VMEM((1, LANES), F32)],
    )(z, z, z, zf, bfp, gq, gk)


def _attn_fwd(qa, ka, va, *, t, G):
    B, H, S, _ = qa.shape
    n = S // t

    def body(q_ref, k_ref, v_ref, a_ref, qb_ref):
        i = pl.program_id(2)
        lane = _lane_iota(t)
        row = lax.broadcasted_iota(jnp.int32, (t, t), 0)
        col = lax.broadcasted_iota(jnp.int32, (t, t), 1)
        qs = [q_ref[0, e] for e in range(G)]

        def step(start, width, carry, masked):
            rows = pl.ds(pl.multiple_of(start, t), width)
            new = []
            for e in range(G):
                m, acc = carry[e]
                s = _dot_nt(qs[e], k_ref[0, e, rows, :])
                if masked:
                    s = jnp.where(col <= row, s, NEG_INF)
                m_new = jnp.maximum(m, jnp.max(s, axis=-1, keepdims=True))
                p = jnp.exp(s - m_new)
                new.append((m_new, jnp.exp(m - m_new) * acc + _dot(p.astype(BF16), v_ref[0, e, rows, :])))
            return tuple(new)

        carry = tuple((jnp.full((t, 1), NEG_INF, F32), jnp.zeros((t, LANES), F32)) for _ in range(G))
        carry = lax.fori_loop(0, i // 2, lambda j, c: step(j * (2 * t), 2 * t, c, False), carry)
        carry = lax.fori_loop(2 * (i // 2), i, lambda j, c: step(j * t, t, c, False), carry)
        carry = step(i * t, t, carry, True)
        outs = []
        for e in range(G):
            m, acc = carry[e]
            l = jnp.sum(jnp.where(lane == C0, acc, 0.0), axis=-1, keepdims=True)
            outs.append(acc / l)
            qb_ref[0, e] = _put3(qs[e].astype(F32), lane, L0, _split3(-(m + jnp.log(l)))).astype(BF16)
        for pp in range(G // 2):
            a_ref[:, pp * LANES:(pp + 1) * LANES] = jnp.where(lane < HEAD_DIM, outs[2 * pp], pltpu.roll(outs[2 * pp + 1], HEAD_DIM, 1))

    return _pcall(
        body, name="attn_fwd", grid=(B, H // G, n),
        in_specs=[pl.BlockSpec((1, G, t, LANES), lambda b, p, i: (b, p, i, 0)),
                  pl.BlockSpec((1, G, S, LANES), lambda b, p, i: (b, p, 0, 0)),
                  pl.BlockSpec((1, G, S, LANES), lambda b, p, i: (b, p, 0, 0))],
        out_specs=[pl.BlockSpec((t, G * HEAD_DIM), lambda b, p, i: (b * n + i, p)),
                   pl.BlockSpec((1, G, t, LANES), lambda b, p, i: (b, p, i, 0))],
        out_shape=[jax.ShapeDtypeStruct((B * S, H * HEAD_DIM), F32), jax.ShapeDtypeStruct((B, H, S, LANES), BF16)],
    )(qa, ka, va)


def _conv_fwd(z, cw, cb, *, B, S, sec_w, rc):
    C = cw.shape[1]
    nct = C // LANES
    off_a, off_b = 4 * sec_w // LANES, 5 * sec_w // LANES

    def body(ga_ref, gb_ref, w_ref, b_ref, uc_ref, buf):
        buf[0:HALO, :] = jnp.zeros((HALO, LANES), F32)
        buf[HALO:HALO + S, :] = ga_ref[...] * _sig(gb_ref[...])

        def chunk(cidx, carry):
            r0 = pl.multiple_of(cidx * rc, rc)
            acc = jnp.zeros((rc, LANES), F32)
            for j in range(CONV_K):
                acc = acc + w_ref[j:j + 1, :] * buf[pl.ds(r0 + (HALO - CONV_K + 1 + j), rc), :]
            uc_ref[pl.ds(r0, rc), :] = acc + b_ref[...]
            return carry

        lax.fori_loop(0, S // rc, chunk, 0)

    return _pcall(
        body, name="conv_fwd", grid=(B, nct),
        in_specs=[pl.BlockSpec((S, LANES), lambda b, c: (b, off_a + c)), pl.BlockSpec((S, LANES), lambda b, c: (b, off_b + c)),
                  pl.BlockSpec((CONV_K, LANES), lambda b, c: (0, c)), pl.BlockSpec((1, LANES), lambda b, c: (0, c))],
        out_specs=pl.BlockSpec((S, LANES), lambda b, c: (b, c)),
        out_shape=jax.ShapeDtypeStruct((B * S, C), F32),
        scratch_shapes=[pltpu.VMEM((HALO + S, LANES), F32)],
    )(z, z, cw, cb)


def _ln_parts(uc, lg, lb):
    mu = jnp.mean(uc, axis=-1, keepdims=True)
    d = uc - mu
    rs = lax.rsqrt(jnp.mean(d * d, axis=-1, keepdims=True) + EPS)
    un = d * rs
    return rs, un, un * lg + lb


def _conv_post(uc, z, lg, lb, *, tm, sec_w):
    T, C = uc.shape
    off_c = 6 * sec_w // C

    def body(uc_ref, gc_ref, lg_ref, lb_ref, uo_ref):
        _, _, ul = _ln_parts(uc_ref[...], lg_ref[...], lb_ref[...])
        gc = gc_ref[...]
        uo_ref[...] = ((ul * _sig(ul)) * (gc * _sig(gc))).astype(BF16)

    return _pcall(
        body, name="conv_post", grid=(T // tm,),
        in_specs=[pl.BlockSpec((tm, C), lambda i: (i, 0)), pl.BlockSpec((tm, C), lambda i: (i, off_c)),
                  pl.BlockSpec((1, C), lambda i: (0, 0)), pl.BlockSpec((1, C), lambda i: (0, 0))],
        out_specs=pl.BlockSpec((tm, C), lambda i: (i, 0)), out_shape=jax.ShapeDtypeStruct((T, C), BF16),
    )(uc, z, lg, lb)


def _outproj(a, z, uo, x, tgt, wo, *, tm, sec_w):
    T, D = x.shape
    FW = a.shape[1]
    C = uo.shape[1]

    def body(a_ref, gf_ref, uo_ref, x_ref, t_ref, wo_ref, dout_ref, dy_ref, dwo_ref, loss_ref):
        @pl.when(pl.program_id(0) == 0)
        def _():
            dwo_ref[...] = jnp.zeros_like(dwo_ref)
            loss_ref[...] = jnp.zeros_like(loss_ref)

        gf = gf_ref[...]
        ao = (a_ref[...] * (gf * _sig(gf))).astype(BF16)
        uo = uo_ref[...]
        out = x_ref[...] + (_dot(ao, wo_ref[0:FW, :]) + _dot(uo, wo_ref[FW:FW + C, :]))
        err = out - t_ref[...]
        loss_ref[...] += jnp.sum(err * err) * (0.5 / D)
        dout = err * (1.0 / D)
        dout_ref[...] = dout
        db = dout.astype(BF16)
        dy_ref[:, 0:FW] = _dot_nt(db, wo_ref[0:FW, :])
        dy_ref[:, FW:FW + C] = _dot_nt(db, wo_ref[FW:FW + C, :])
        dwo_ref[0:FW, :] += _dot_tn(ao, db)
        dwo_ref[FW:FW + C, :] += _dot_tn(uo, db)

    tok = lambda w, n=0: pl.BlockSpec((tm, w), lambda i: (i, n))
    return _pcall(
        body, name="outproj", grid=(T // tm,),
        in_specs=[tok(FW), tok(sec_w, 3), tok(C), tok(D), tok(D), pl.BlockSpec((FW + C, D), lambda i: (0, 0))],
        out_specs=[tok(D), tok(FW + C), pl.BlockSpec((FW + C, D), lambda i: (0, 0)), pl.BlockSpec((8, LANES), lambda i: (0, 0))],
        out_shape=[jax.ShapeDtypeStruct((T, D), F32), jax.ShapeDtypeStruct((T, FW + C), F32),
                   jax.ShapeDtypeStruct((FW + C, D), F32), jax.ShapeDtypeStruct((8, LANES), F32)],
    )(a, z, uo, x, tgt, wo)


def _conv_post_bwd(dy, uc, z, lg, lb, *, tm, sec_w):
    T, C = uc.shape
    FW = dy.shape[1] - C
    off_c = 6 * sec_w // C

    def body(dy_ref, uc_ref, gc_ref, lg_ref, lb_ref, dgc_ref, duc_ref, small_ref):
        @pl.when(pl.program_id(0) == 0)
        def _():
            small_ref[...] = jnp.zeros_like(small_ref)

        lg = lg_ref[...]
        rs, un, ul = _ln_parts(uc_ref[...], lg, lb_ref[...])
        s_ul = _sig(ul)
        gc = gc_ref[...]
        s_gc = _sig(gc)
        duo = dy_ref[...]
        dgc_ref[...] = (duo * (ul * s_ul) * _dsilu(gc, s_gc)).astype(BF16)
        dul = duo * (gc * s_gc) * _dsilu(ul, s_ul)
        dun = dul * lg
        duc = rs * (dun - jnp.mean(dun, axis=-1, keepdims=True) - un * jnp.mean(dun * un, axis=-1, keepdims=True))
        duc_ref[...] = duc
        small_ref[0:1, :] += jnp.sum(dul * un, axis=0, keepdims=True)
        small_ref[1:2, :] += jnp.sum(dul, axis=0, keepdims=True)
        small_ref[2:3, :] += jnp.sum(duc, axis=0, keepdims=True)

    tok = lambda n: pl.BlockSpec((tm, C), lambda i: (i, n))
    row = pl.BlockSpec((1, C), lambda i: (0, 0))
    return _pcall(
        body, name="conv_post_bwd", grid=(T // tm,),
        in_specs=[tok(FW // C), tok(0), tok(off_c), row, row],
        out_specs=[tok(0), tok(0), pl.BlockSpec((8, C), lambda i: (0, 0))],
        out_shape=[jax.ShapeDtypeStruct((T, C), BF16), jax.ShapeDtypeStruct((T, C), F32), jax.ShapeDtypeStruct((8, C), F32)],
    )(dy, uc, z, lg, lb)


def _conv_bwd(duc, z, cw, *, B, S, sec_w, rc):
    C = cw.shape[1]
    nct = C // LANES
    off_a, off_b = 4 * sec_w // LANES, 5 * sec_w // LANES

    def body(duc_ref, ga_ref, gb_ref, w_ref, dga_ref, dgb_ref, dw_ref, ubuf, dbuf):
        @pl.when(pl.program_id(1) == 0)
        def _():
            dw_ref[...] = jnp.zeros_like(dw_ref)

        ubuf[0:HALO, :] = jnp.zeros((HALO, LANES), F32)
        ubuf[HALO:HALO + S, :] = ga_ref[...] * _sig(gb_ref[...])
        dbuf[0:S, :] = duc_ref[...]
        dbuf[S:S + HALO, :] = jnp.zeros((HALO, LANES), F32)

        def chunk(cidx, carry):
            r0 = pl.multiple_of(cidx * rc, rc)
            acc = jnp.zeros((rc, LANES), F32)
            for j in range(CONV_K):
                acc = acc + w_ref[j:j + 1, :] * dbuf[pl.ds(r0 + (CONV_K - 1 - j), rc), :]
            rows = pl.ds(r0, rc)
            ga, sg = ga_ref[rows, :], _sig(gb_ref[rows, :])
            dga_ref[rows, :] = (acc * sg).astype(BF16)
            dgb_ref[rows, :] = (acc * ga * sg * (1.0 - sg)).astype(BF16)
            return carry

        lax.fori_loop(0, S // rc, chunk, 0)
        d = duc_ref[...]
        for j in range(CONV_K):
            dw_ref[j:j + 1, :] += jnp.sum(d * ubuf[pl.ds(HALO - CONV_K + 1 + j, S), :], axis=0, keepdims=True)

    return _pcall(
        body, name="conv_bwd", grid=(nct, B),
        in_specs=[pl.BlockSpec((S, LANES), lambda c, b: (b, c)), pl.BlockSpec((S, LANES), lambda c, b: (b, off_a + c)),
                  pl.BlockSpec((S, LANES), lambda c, b: (b, off_b + c)), pl.BlockSpec((CONV_K, LANES), lambda c, b: (0, c))],
        out_specs=[pl.BlockSpec((S, LANES), lambda c, b: (b, c)), pl.BlockSpec((S, LANES), lambda c, b: (b, c)),
                   pl.BlockSpec((CONV_K, LANES), lambda c, b: (0, c))],
        out_shape=[jax.ShapeDtypeStruct((B * S, C), BF16), jax.ShapeDtypeStruct((B * S, C), BF16),
                   jax.ShapeDtypeStruct((CONV_K, C), F32)],
        scratch_shapes=[pltpu.VMEM((HALO + S, LANES), F32), pltpu.VMEM((S + HALO, LANES), F32)],
    )(duc, z, z, cw)


def _attn_bwd_prep(dy, a, z, *, B, S, H, ts, sec_w):
    FW = H * HEAD_DIM
    nts = S // ts

    def body(dy_ref, a_ref, gf_ref, dgf_ref, do_ref):
        lane = _lane_iota(ts)
        for p in range(H // 2):
            cols = slice(p * LANES, (p + 1) * LANES)
            gf, av, dya = gf_ref[:, cols], a_ref[:, cols], dy_ref[:, cols]
            sg = _sig(gf)
            dgf_ref[:, cols] = (dya * av * _dsilu(gf, sg)).astype(BF16)
            da = dya * (gf * sg)
            nd = -_half_sums(da * av, lane)
            for e in range(2):
                nd_e = nd[:, e * HEAD_DIM:e * HEAD_DIM + 1]
                do_ref[0, 2 * p + e] = _put3(_head_to_low(da, e, lane), lane, C0, _split3(nd_e)).astype(BF16)

    tok = lambda n: pl.BlockSpec((ts, FW), lambda b, i: (b * nts + i, n))
    return _pcall(
        body, name="attn_bwd_prep", grid=(B, nts),
        in_specs=[tok(0), tok(0), pl.BlockSpec((ts, sec_w), lambda b, i: (b * nts + i, 3))],
        out_specs=[tok(0), pl.BlockSpec((1, H, ts, LANES), lambda b, i: (b, 0, i, 0))],
        out_shape=[jax.ShapeDtypeStruct((B * S, FW), BF16), jax.ShapeDtypeStruct((B, H, S, LANES), BF16)],
    )(dy, a, z)


def _attn_bwd(qb, ka, va, doa, *, t, G):
    B, H, S, _ = qb.shape
    n = S // t
    assert n % 2 == 0

    def body(q_ref, k_ref, v_ref, do_ref, dq_ref, dk_ref, dv_ref):
        j = pl.program_id(2)

        @pl.when(j == 0)
        def _():
            dq_ref[...] = jnp.zeros_like(dq_ref)

        row = lax.broadcasted_iota(jnp.int32, (t, t), 0)
        col = lax.broadcasted_iota(jnp.int32, (t, t), 1)
        ks = [k_ref[0, e] for e in range(G)]
        vs = [v_ref[0, e] for e in range(G)]

        def step(start, width, carry, masked):
            rows = pl.ds(pl.multiple_of(start, t), width)
            new = []
            for e in range(G):
                dk, dv = carry[e]
                q, do = q_ref[0, e, rows, :], do_ref[0, e, rows, :]
                st = _dot_nt(ks[e], q)
                dpt = _dot_nt(vs[e], do)
                if masked:
                    st = jnp.where(row <= col, st, NEG_INF)
                pt = jnp.exp(st)
                dsb = (pt * dpt).astype(BF16)
                dq_ref[0, e, rows, :] += _dot_tn(dsb, ks[e])
                new.append((dk + _dot(dsb, q), dv + _dot(pt.astype(BF16), do)))
            return tuple(new)

        zero = jnp.zeros((t, LANES), F32)
        carry = step(j * t, t, tuple((zero, zero) for _ in range(G)), True)
        first_pair = (j + 2) // 2
        carry = lax.fori_loop(j + 1, 2 * first_pair, lambda i, c: step(i * t, t, c, False), carry)
        carry = lax.fori_loop(first_pair, n // 2, lambda i, c: step(i * (2 * t), 2 * t, c, False), carry)
        for e in range(G):
            dk_ref[0, e], dv_ref[0, e] = carry[e]

    full = pl.BlockSpec((1, G, S, LANES), lambda b, h, j: (b, h, 0, 0))
    tile = pl.BlockSpec((1, G, t, LANES), lambda b, h, j: (b, h, j, 0))
    shape = jax.ShapeDtypeStruct((B, H, S, LANES), F32)
    return _pcall(
        body, name="attn_bwd", grid=(B, H // G, n),
        in_specs=[full, tile, tile, full], out_specs=[full, tile, tile], out_shape=[shape, shape, shape],
    )(qb, ka, va, doa)


def _attn_post_bwd(dq, dk, dv, z, zf, bfp, gq, gk, *, B, S, H, ts):
    FW = H * HEAD_DIM
    nts = S // ts
    scale = HEAD_DIM ** -0.5

    def body(dq_ref, dk_ref, dv_ref, zq_ref, zk_ref, zf_ref, bf_ref, gq_ref, gk_ref,
             dzq_ref, dzk_ref, dzv_ref, dzf_ref, small_ref, carry):
        first = (pl.program_id(0) == 0) & (pl.program_id(1) == 0)

        @pl.when(first)
        def _():
            small_ref[...] = jnp.zeros_like(small_ref)

        @pl.when(pl.program_id(1) == 0)
        def _():
            carry[...] = jnp.zeros_like(carry)

        lane = _lane_iota(ts)
        dcs = jnp.zeros((ts, LANES), F32)
        for p in range(H // 2):
            cols = slice(p * LANES, (p + 1) * LANES)
            for d_ref, z_ref, g_ref, o_ref, srow, mul in ((dq_ref, zq_ref, gq_ref, dzq_ref, 0, scale),
                                                          (dk_ref, zk_ref, gk_ref, dzk_ref, 1, 1.0)):
                dy = jnp.where(lane < HEAD_DIM, d_ref[0, 2 * p], pltpu.roll(d_ref[0, 2 * p + 1], HEAD_DIM, 1)) * mul
                xv, g = z_ref[:, cols], g_ref[:, cols]
                r = lax.rsqrt(_half_sums(xv * xv, lane) * (1.0 / HEAD_DIM) + EPS)
                dxh = dy * g
                mm = _half_sums(dxh * xv, lane) * (1.0 / HEAD_DIM)
                o_ref[:, cols] = (r * (dxh - xv * (r * r) * mm)).astype(BF16)
                small_ref[srow:srow + 1, cols] += jnp.sum(dy * xv * r, axis=0, keepdims=True)
            dzv_ref[:, cols] = jnp.where(lane < HEAD_DIM, dv_ref[0, 2 * p], pltpu.roll(dv_ref[0, 2 * p + 1], HEAD_DIM, 1)).astype(BF16)
            for e in range(2):
                h = 2 * p + e
                dc = jnp.sum(jnp.where(lane == C0, dq_ref[0, h], 0.0) - jnp.where(lane == K0, dk_ref[0, h], 0.0),
                             axis=-1, keepdims=True)
                dcs = jnp.where(lane == h, dc, dcs)
        ri = lax.broadcasted_iota(jnp.int32, (ts, ts), 0)
        ci = lax.broadcasted_iota(jnp.int32, (ts, ts), 1)
        tri = (ci >= ri).astype(BF16)
        dl = carry[...]
        for piece in _split3(dcs):
            dl = dl + _dot(tri, piece)
        carry[...] = dl[0:1, :]
        xf = zf_ref[...] + bf_ref[...]
        df = jnp.where(lane < H, dl * _sig(-xf), 0.0)
        dzf_ref[...] = df.astype(BF16)
        small_ref[2:3, 0:LANES] += jnp.sum(df, axis=0, keepdims=True)

    rev = lambda b, i: (b, 0, nts - 1 - i, 0)
    head = pl.BlockSpec((1, H, ts, LANES), rev)
    tokrow = lambda b, i: b * nts + nts - 1 - i
    sec = lambda n: pl.BlockSpec((ts, FW), lambda b, i: (tokrow(b, i), n))
    narrow = pl.BlockSpec((ts, LANES), lambda b, i: (tokrow(b, i), 0))
    row = lambda w: pl.BlockSpec((1, w), lambda b, i: (0, 0))
    return _pcall(
        body, name="attn_post_bwd", grid=(B, nts),
        in_specs=[head, head, head, sec(0), sec(1), narrow, row(LANES), row(FW), row(FW)],
        out_specs=[sec(0), sec(0), sec(0), narrow, pl.BlockSpec((8, FW), lambda b, i: (0, 0))],
        out_shape=[jax.ShapeDtypeStruct((B * S, FW), BF16)] * 3 + [jax.ShapeDtypeStruct((B * S, LANES), BF16),
                                                                   jax.ShapeDtypeStruct((8, FW), F32)],
        scratch_shapes=[pltpu.VMEM((1, LANES), F32)],
    )(dq, dk, dv, z, z, zf, bfp, gq, gk)


def _inproj_bwd_x(dz, dzf, wm, wf, x, g, dout, *, tm, tn):
    T, D = x.shape
    nj = wm.shape[1] // tn

    def body(dz_ref, dzf_ref, wm_ref, wf_ref, x_ref, g_ref, dout_ref, gx_ref, dg_ref, acc):
        i, j = pl.program_id(0), pl.program_id(1)

        @pl.when((i == 0) & (j == 0))
        def _():
            dg_ref[...] = jnp.zeros_like(dg_ref)

        @pl.when(j == 0)
        def _():
            acc[...] = _dot_nt(dzf_ref[...], wf_ref[...])

        acc[...] += _dot_nt(dz_ref[...], wm_ref[...])

        @pl.when(j == nj - 1)
        def _():
            dh, xv, g = acc[...], x_ref[...], g_ref[...]
            r = lax.rsqrt(jnp.mean(xv * xv, axis=-1, keepdims=True) + EPS)
            dxh = dh * g
            mm = jnp.mean(dxh * xv, axis=-1, keepdims=True)
            gx_ref[...] = dout_ref[...] + r * (dxh - xv * (r * r) * mm)
            dg_ref[0:1, :] += jnp.sum(dh * xv * r, axis=0, keepdims=True)

    tok = pl.BlockSpec((tm, D), lambda i, j: (i, 0))
    return _pcall(
        body, name="inproj_bwd_x", grid=(T // tm, nj),
        in_specs=[pl.BlockSpec((tm, tn), lambda i, j: (i, j)), pl.BlockSpec((tm, LANES), lambda i, j: (i, 0)),
                  pl.BlockSpec((D, tn), lambda i, j: (0, j)), pl.BlockSpec((D, LANES), lambda i, j: (0, 0)),
                  tok, pl.BlockSpec((1, D), lambda i, j: (0, 0)), tok],
        out_specs=[tok, pl.BlockSpec((8, D), lambda i, j: (0, 0))],
        out_shape=[jax.ShapeDtypeStruct((T, D), F32), jax.ShapeDtypeStruct((8, D), F32)],
        scratch_shapes=[pltpu.VMEM((tm, D), F32)],
    )(dz, dzf, wm, wf, x, g, dout)


def _inproj_bwd_w(h, dz, dzf, *, tm, tn):
    T, D = h.shape
    nj = dz.shape[1] // tn
    ni = T // tm

    def body(h_ref, dz_ref, dzf_ref, dwm_ref, dwf_ref):
        i, j = pl.program_id(1), pl.program_id(0)

        @pl.when(i == 0)
        def _():
            dwm_ref[...] = jnp.zeros_like(dwm_ref)

        @pl.when((i == 0) & (j == 0))
        def _():
            dwf_ref[...] = jnp.zeros_like(dwf_ref)

        hv = h_ref[...]
        dwm_ref[...] += _dot_tn(hv, dz_ref[...])

        @pl.when(j == 0)
        def _():
            dwf_ref[...] += _dot_tn(hv, dzf_ref[...])

    return _pcall(
        body, name="inproj_bwd_w", grid=(nj, ni),
        in_specs=[pl.BlockSpec((tm, D), lambda j, i: (i, 0)), pl.BlockSpec((tm, tn), lambda j, i: (i, j)),
                  pl.BlockSpec((tm, LANES), lambda j, i: (i, 0))],
        out_specs=[pl.BlockSpec((D, tn), lambda j, i: (0, j)), pl.BlockSpec((D, LANES), lambda j, i: (0, 0))],
        out_shape=[jax.ShapeDtypeStruct((D, dz.shape[1]), F32), jax.ShapeDtypeStruct((D, LANES), F32)],
    )(h, dz, dzf)


def _adamw(w, g, m, v):
    m = ADAM_B1 * m + (1.0 - ADAM_B1) * g
    v = ADAM_B2 * v + (1.0 - ADAM_B2) * (g * g)
    m_hat = m / (1.0 - ADAM_B1 ** ADAM_STEP)
    v_hat = v / (1.0 - ADAM_B2 ** ADAM_STEP)
    return -ADAM_LR * (m_hat / (jnp.sqrt(v_hat) + ADAM_EPS) + ADAM_WD * w), m, v


def _adam_update(w, m, v, parts, *, name, tr):
    R, Cc = w.shape

    def body(w_ref, m_ref, v_ref, p_ref, g_ref, d_ref, nm_ref, nv_ref):
        g = p_ref[0].astype(F32)
        for k in range(1, N_DEV):
            g = g + p_ref[k].astype(F32)
        g_ref[...] = g
        d_ref[...], nm_ref[...], nv_ref[...] = _adamw(w_ref[...], g, m_ref[...], v_ref[...])

    blk = pl.BlockSpec((tr, Cc), lambda i: (i, 0))
    shape = jax.ShapeDtypeStruct((R, Cc), F32)
    return _pcall(
        body, name=name, grid=(R // tr,),
        in_specs=[blk, blk, blk, pl.BlockSpec((N_DEV, tr, Cc), lambda i: (0, i, 0))],
        out_specs=[blk, blk, blk, blk], out_shape=[shape, shape, shape, shape],
    )(w, m, v, parts)


MESH = pl.DeviceIdType.MESH
ANY = pl.BlockSpec(memory_space=pl.ANY)


def _flip(v, bit):
    return 1 - v if bit else v


def _allgather(shards):
    n = len(shards)

    def body(*refs):
        srcs, outs = refs[:n], refs[n:2 * n]
        send_sems, recv_sems, local_sems = refs[2 * n:]
        x, y, c = lax.axis_index("x"), lax.axis_index("y"), lax.axis_index("c")
        sibling = (x, y, 1 - c)
        chips = [(1 - x, y), (x, 1 - y), (1 - x, 1 - y)]
        slot = lambda px, py, pc: 4 * px + 2 * py + pc

        def copy(a, k, block, to, src=None):
            dst = outs[a].at[slot(*block)]
            return pltpu.make_async_remote_copy(src_ref=dst if src is None else src, dst_ref=dst, send_sem=send_sems.at[7 * a + k],
                                                recv_sem=recv_sems.at[7 * a + k], device_id=to, device_id_type=MESH)

        mine = [pltpu.make_async_copy(srcs[a], outs[a].at[slot(x, y, c)], local_sems.at[a]) for a in range(n)]
        sends = []
        for a in range(n):
            mine[a].start()
            sends.append(copy(a, 0, (x, y, c), sibling, src=srcs[a]))
            sends += [copy(a, 1 + j, (x, y, c), (*chip, c), src=srcs[a]) for j, chip in enumerate(chips)]
        for cp in sends:
            cp.start()
        for a in range(n):
            for j, chip in enumerate(chips):
                copy(a, 1 + j, (*chip, c), (x, y, c)).wait_recv()
                fwd = copy(a, 4 + j, (*chip, c), sibling)
                fwd.start()
                sends.append(fwd)
        for a in range(n):
            copy(a, 0, (x, y, 1 - c), (x, y, c)).wait_recv()
            for j, chip in enumerate(chips):
                copy(a, 4 + j, (*chip, 1 - c), (x, y, c)).wait_recv()
        for cp in sends:
            cp.wait_send()
        for cp in mine:
            cp.wait()

    return pl.pallas_call(
        body, name="allgather_weights", in_specs=[ANY] * n, out_specs=[ANY] * n,
        out_shape=[jax.ShapeDtypeStruct((N_DEV,) + s.shape, s.dtype) for s in shards],
        scratch_shapes=[pltpu.SemaphoreType.DMA((7 * n,)), pltpu.SemaphoreType.DMA((7 * n,)), pltpu.SemaphoreType.DMA((n,))],
    )(*shards)


def _exchange(scatter, gather):
    arrs = list(scatter) + list(gather)
    n, ns = len(arrs), len(scatter)

    def body(*refs):
        srcs, outs = refs[:n], refs[n:2 * n]
        send_sems, recv_sems, local_sems = refs[2 * n:]
        x, y, c = lax.axis_index("x"), lax.axis_index("y"), lax.axis_index("c")
        me = 4 * x + 2 * y + c
        pending = []
        for a in range(n):
            piece = lambda k: srcs[a].at[k] if a < ns else srcs[a]
            loc = pltpu.make_async_copy(piece(me), outs[a].at[me], local_sems.at[a])
            loc.start()
            pending.append(loc.wait)
            for d in range(1, N_DEV):
                peer = (_flip(x, d & 4), _flip(y, d & 2), _flip(c, d & 1))
                pid = 4 * peer[0] + 2 * peer[1] + peer[2]
                sem = 7 * a + d - 1
                out = pltpu.make_async_remote_copy(src_ref=piece(pid), dst_ref=outs[a].at[me], send_sem=send_sems.at[sem],
                                                   recv_sem=recv_sems.at[sem], device_id=peer, device_id_type=MESH)
                out.start()
                back = pltpu.make_async_remote_copy(src_ref=piece(me), dst_ref=outs[a].at[pid], send_sem=send_sems.at[sem],
                                                    recv_sem=recv_sems.at[sem], device_id=peer, device_id_type=MESH)
                pending += [out.wait_send, back.wait_recv]
        for wait in pending:
            wait()

    return pl.pallas_call(
        body, name="exchange_grads", in_specs=[ANY] * n, out_specs=[ANY] * n,
        out_shape=[jax.ShapeDtypeStruct(s.shape, s.dtype) for s in scatter]
        + [jax.ShapeDtypeStruct((N_DEV,) + s.shape, s.dtype) for s in gather],
        scratch_shapes=[pltpu.SemaphoreType.DMA((7 * n,)), pltpu.SemaphoreType.DMA((7 * n,)), pltpu.SemaphoreType.DMA((n,))],
    )(*arrs)


def _local_step(x, tgt, norm_g, wm, wf, bfp, gq, gk, cw, cb, lg, lb, wo, *, B, S, H):
    T, D = x.shape
    sec_w = H * HEAD_DIM
    tm = min(512, T)
    ts = min(256, S)
    rc = min(64, S)
    z, zf, h = _inproj_fwd(x, norm_g, wm, wf, tm=tm, tn=sec_w)
    qa, ka, va = _attn_prep(z, zf, bfp, gq, gk, B=B, S=S, H=H, ts=ts)
    a, qb = _attn_fwd(qa, ka, va, t=ts, G=4)
    uc = _conv_fwd(z, cw, cb, B=B, S=S, sec_w=sec_w, rc=rc)
    uo = _conv_post(uc, z, lg, lb, tm=ts, sec_w=sec_w)
    dout, dy, dwo, loss = _outproj(a, z, uo, x, tgt, wo, tm=ts, sec_w=sec_w)
    dgc, duc, small_c = _conv_post_bwd(dy, uc, z, lg, lb, tm=ts, sec_w=sec_w)
    dga, dgb, dcw = _conv_bwd(duc, z, cw, B=B, S=S, sec_w=sec_w, rc=rc)
    dgf, doa = _attn_bwd_prep(dy, a, z, B=B, S=S, H=H, ts=ts, sec_w=sec_w)
    dq, dk, dv = _attn_bwd(qb, ka, va, doa, t=ts, G=2)
    dzq, dzk, dzv, dzf, small_a = _attn_post_bwd(dq, dk, dv, z, zf, bfp, gq, gk, B=B, S=S, H=H, ts=ts)
    dz = jnp.concatenate([dzq, dzk, dzv, dgf, dga, dgb, dgc], axis=1)
    gx, dg = _inproj_bwd_x(dz, dzf, wm, wf, x, norm_g, dout, tm=tm, tn=sec_w)
    dwm, dwf = _inproj_bwd_w(h, dz, dzf, tm=tm, tn=sec_w)
    small = jnp.concatenate([dg[0:1], small_a[2:3], small_a[0:2], small_c[2:3], small_c[0:2], jnp.zeros((1, D), F32)], axis=0)
    return loss, gx, dwm, dwf, dwo, dcw, small


def kernel(x, norm_g, w_in, b_forget, q_norm_g, k_norm_g, conv_w, conv_b, conv_ln_g, conv_ln_b, w_out, loss_target, m_norm_g, m_w_in, m_b_forget, m_q_norm_g, m_k_norm_g, m_conv_w, m_conv_b, m_conv_ln_g, m_conv_ln_b, m_w_out, v_norm_g, v_w_in, v_b_forget, v_q_norm_g, v_k_norm_g, v_conv_w, v_conv_b, v_conv_ln_g, v_conv_ln_b, v_w_out):
    B, S, D = x.shape
    H = q_norm_g.shape[1]
    FW = H * HEAD_DIM
    assert q_norm_g.shape[2] == HEAD_DIM and FW == D and conv_b.shape[1] == D and H <= LANES // 3
    ncol = w_in.shape[2]
    T = B * S

    wg, wog, cwg = _allgather([w_in[0].astype(BF16), w_out[0].astype(BF16), conv_w[0]])
    w_full = jnp.transpose(wg, (1, 0, 2)).reshape(D, N_DEV * ncol)
    wm = jnp.concatenate([w_full[:, :3 * FW], w_full[:, 3 * FW + H:]], axis=1)
    wf = jnp.pad(w_full[:, 3 * FW:3 * FW + H], ((0, 0), (0, LANES - H)))
    wo = wog.reshape(N_DEV * wog.shape[1], D)
    cw = jnp.transpose(cwg, (1, 0, 2)).reshape(CONV_K, D)
    bfp = jnp.pad(b_forget, ((0, 0), (0, LANES - H)))
    gq, gk = q_norm_g.reshape(1, FW), k_norm_g.reshape(1, FW)

    loss, gx, dwm, dwf, dwo, dcw, small = _local_step(
        x.reshape(T, D), loss_target.reshape(T, D), norm_g, wm, wf, bfp, gq, gk, cw, conv_b, conv_ln_g, conv_ln_b, wo,
        B=B, S=S, H=H)

    dw_full = jnp.concatenate([dwm[:, :3 * FW], dwf[:, :H], dwm[:, 3 * FW:]], axis=1)
    dw_parts = jnp.transpose(dw_full.reshape(D, N_DEV, ncol), (1, 0, 2)).astype(BF16)
    dwo_parts = dwo.reshape(N_DEV, dwo.shape[0] // N_DEV, D).astype(BF16)
    dcw_parts = jnp.transpose(dcw.reshape(CONV_K, N_DEV, D // N_DEV), (1, 0, 2))
    small = jnp.concatenate([small, jnp.broadcast_to(loss[0:1, 0:1], (8, D))], axis=0)
    r_w, r_wo, r_cw, r_small = _exchange([dw_parts, dwo_parts, dcw_parts], [small])

    g_w, d_w, nm_w, nv_w = _adam_update(w_in[0], m_w_in[0], v_w_in[0], r_w, name="adam_w_in", tr=min(128, D))
    g_wo, d_wo, nm_wo, nv_wo = _adam_update(w_out[0], m_w_out[0], v_w_out[0], r_wo, name="adam_w_out", tr=min(128, w_out.shape[1]))
    g_cw, d_cw, nm_cw, nv_cw = _adam_update(conv_w[0], m_conv_w[0], v_conv_w[0], r_cw, name="adam_conv_w", tr=CONV_K)
    pad_b = lambda t: jnp.pad(t, ((0, 0), (0, D - H)))
    pack = lambda nb, bf, q, k, b, g, bb: jnp.concatenate(
        [nb, pad_b(bf), q.reshape(1, FW), k.reshape(1, FW), b, g, bb, jnp.zeros((9, D), F32)], axis=0)
    small_w = pack(norm_g, b_forget, q_norm_g, k_norm_g, conv_b, conv_ln_g, conv_ln_b)
    small_m = pack(m_norm_g, m_b_forget, m_q_norm_g, m_k_norm_g, m_conv_b, m_conv_ln_g, m_conv_ln_b)
    small_v = pack(v_norm_g, v_b_forget, v_q_norm_g, v_k_norm_g, v_conv_b, v_conv_ln_g, v_conv_ln_b)
    g_s, d_s, nm_s, nv_s = _adam_update(small_w, small_m, small_v, r_small, name="adam_small", tr=16)

    def unpack(t):
        return [t[0:1], t[1:2, :H], t[2:3].reshape(1, H, HEAD_DIM), t[3:4].reshape(1, H, HEAD_DIM), t[4:5], t[5:6], t[6:7]]

    def leaves(small_t, w_t, cw_t, wo_t):
        ng, bf, q, k, cb, lg, lb = unpack(small_t)
        return [ng, w_t[None], bf, q, k, cw_t[None], cb, lg, lb, wo_t[None]]

    return (g_s[8, 0], gx.reshape(B, S, D), *leaves(g_s, g_w, g_cw, g_wo), *leaves(d_s, d_w, d_cw, d_wo),
            *leaves(nm_s, nm_w, nm_cw, nm_wo), *leaves(nv_s, nv_w, nv_cw, nv_wo))
```

```python
import jax
import jax.numpy as jnp
from jax import lax
from jax.experimental import pallas as pl
from jax.experimental.pallas import tpu as pltpu

F32, BF16 = jnp.float32, jnp.bfloat16
EPS = 1e-6
NEG_INF = -1e30
CONV_K = 31
HEAD_DIM = 64
ADAM_LR, ADAM_B1, ADAM_B2, ADAM_EPS, ADAM_WD, ADAM_STEP = 0.001, 0.9, 0.999, 1e-08, 0.01, 10

LANES = 128
HALO = 32
N_DEV = 8
VMEM_LIMIT = 56 * 1024 * 1024

C0, K0, L0 = HEAD_DIM, HEAD_DIM + 3, HEAD_DIM + 6


def _pcall(body, *, name, vmem=VMEM_LIMIT, **kw):
    return pl.pallas_call(body, name=name, compiler_params=pltpu.CompilerParams(vmem_limit_bytes=vmem), **kw)


def _dot(a, b):
    return jnp.dot(a, b, preferred_element_type=F32)


def _dot_nt(a, b):
    return lax.dot_general(a, b, (((1,), (1,)), ((), ())), preferred_element_type=F32)


def _dot_tn(a, b):
    return lax.dot_general(a, b, (((0,), (0,)), ((), ())), preferred_element_type=F32)


def _split3(x):
    a = x.astype(BF16)
    r = x - a.astype(F32)
    b = r.astype(BF16)
    c = (r - b.astype(F32)).astype(BF16)
    return a, b, c


def _sig(x):
    return jax.nn.sigmoid(x)


def _dsilu(x, s):
    return s * (1.0 + x * (1.0 - s))


def _lane_iota(rows):
    return lax.broadcasted_iota(jnp.int32, (rows, LANES), 1)


def _half_sums(v, lane):
    lo = jnp.sum(jnp.where(lane < HEAD_DIM, v, 0.0), axis=-1, keepdims=True)
    hi = jnp.sum(jnp.where(lane >= HEAD_DIM, v, 0.0), axis=-1, keepdims=True)
    return jnp.where(lane < HEAD_DIM, lo, hi)


def _head_to_low(v, e, lane):
    if e:
        v = pltpu.roll(v, HEAD_DIM, 1)
    return jnp.where(lane < HEAD_DIM, v, 0.0)


def _put3(base, lane, first, pieces):
    for n, p in enumerate(pieces):
        base = jnp.where(lane == first + n, p.astype(F32), base)
    return base


def _section_rows(D, H):
    FW = H * HEAD_DIM
    return 3 * FW, 3 * FW + H, 7 * FW + H


def _row_pieces(a, b, R, step):
    out = []
    while a < b:
        k = a // R
        lo = a - k * R
        hi = min(R, lo + min(step, b - a))
        out.append((k, lo, hi))
        a += hi - lo
    return out


def _w_relayout(wg, *, D, H):
    R = wg.shape[1]
    FW = H * HEAD_DIM
    f0, f1, end = _section_rows(D, H)

    def body(wg_ref, wm_ref, wf_ref):
        def put(dst_ref, d0, a, b):
            for k, lo, hi in _row_pieces(a, b, R, 256):
                dst_ref[d0:d0 + hi - lo, :] = wg_ref[k, lo:hi, :]
                d0 += hi - lo

        put(wm_ref, 0, 0, f0)
        put(wm_ref, f0, f1, end)
        wf_ref[...] = jnp.zeros_like(wf_ref)
        put(wf_ref, 0, f0, f1)

    return _pcall(body, name="w_relayout",
                  out_shape=[jax.ShapeDtypeStruct((7 * FW, D), BF16), jax.ShapeDtypeStruct((LANES, D), BF16)])(wg)


def _dw_relayout(secs, dwf, *, R, H):
    FW, D = secs[0].shape
    f0, f1, end = _section_rows(D, H)

    def body(*refs):
        sec_refs, dwf_ref, out_ref = refs[:7], refs[7], refs[8]

        def src(g0, g1):
            if f0 <= g0 < f1:
                return dwf_ref[g0 - f0:g1 - f0, :]
            s0 = g0 if g0 < f0 else g0 - H
            return sec_refs[s0 // FW][s0 % FW:s0 % FW + (g1 - g0), :]

        cuts = sorted({0, end, f0, f1} | {s * FW for s in range(4)} | {f1 + s * FW for s in range(5)})
        for k in range(N_DEV):
            g = k * R
            while g < (k + 1) * R:
                nxt = min(min(c for c in cuts if c > g), (k + 1) * R, g + 256)
                out_ref[k, g - k * R:nxt - k * R, :] = src(g, nxt)
                g = nxt

    return _pcall(body, name="dw_relayout", out_shape=jax.ShapeDtypeStruct((N_DEV, R, D), BF16))(*secs, dwf)


def _inproj_fwd(x, g, wmt, wft, *, tm, tn):
    T, D = x.shape
    nj = wmt.shape[0] // tn

    def body(x_ref, g_ref, wm_ref, wf_ref, z_ref, zf_ref, h_ref, h_scr):
        @pl.when(pl.program_id(1) == 0)
        def _():
            xv = x_ref[...]
            r = lax.rsqrt(jnp.mean(xv * xv, axis=-1, keepdims=True) + EPS)
            h = ((xv * r) * g_ref[...]).astype(BF16)
            h_scr[...] = h
            h_ref[...] = h
            zf_ref[...] = _dot_nt(h, wf_ref[...])

        z_ref[...] = _dot_nt(h_scr[...], wm_ref[...])

    return _pcall(
        body, name="inproj_fwd", grid=(T // tm, nj),
        in_specs=[pl.BlockSpec((tm, D), lambda i, j: (i, 0)), pl.BlockSpec((1, D), lambda i, j: (0, 0)),
                  pl.BlockSpec((tn, D), lambda i, j: (j, 0)), pl.BlockSpec((LANES, D), lambda i, j: (0, 0))],
        out_specs=[pl.BlockSpec((tm, tn), lambda i, j: (i, j)), pl.BlockSpec((tm, LANES), lambda i, j: (i, 0)),
                   pl.BlockSpec((tm, D), lambda i, j: (i, 0))],
        out_shape=[jax.ShapeDtypeStruct((T, wmt.shape[0]), F32), jax.ShapeDtypeStruct((T, LANES), F32),
                   jax.ShapeDtypeStruct((T, D), BF16)],
        scratch_shapes=[pltpu.VMEM((tm, D), BF16)],
    )(x, g, wmt, wft)


def _attn_prep(z, zf, bfp, gq, gk, *, B, S, H, ts):
    FW = H * HEAD_DIM
    nts = S // ts
    scale = HEAD_DIM ** -0.5

    def body(zq_ref, zk_ref, zv_ref, zf_ref, bf_ref, gq_ref, gk_ref, qa_ref, ka_ref, va_ref, carry):
        @pl.when(pl.program_id(1) == 0)
        def _():
            carry[...] = jnp.zeros_like(carry)

        lane = _lane_iota(ts)
        xf = zf_ref[...] + bf_ref[...]
        logf = jnp.minimum(xf, 0.0) - jnp.log1p(jnp.exp(-jnp.abs(xf)))
        logf = jnp.where(lane < H, logf, 0.0)
        ri = lax.broadcasted_iota(jnp.int32, (ts, ts), 0)
        ci = lax.broadcasted_iota(jnp.int32, (ts, ts), 1)
        tri = (ri >= ci).astype(BF16)
        c = carry[...]
        for piece in _split3(logf):
            c = c + _dot(tri, piece)
        carry[...] = c[ts - 1:ts, :]
        c1, c2, c3 = _split3(c)
        cpk = c1.astype(F32) + pltpu.roll(c2.astype(F32), H, 1) + pltpu.roll(c3.astype(F32), 2 * H, 1)
        cq, ck = cpk.astype(BF16), (-cpk).astype(BF16)
        prow = lax.broadcasted_iota(jnp.int32, (LANES, LANES), 0)
        pcol = lax.broadcasted_iota(jnp.int32, (LANES, LANES), 1)
        ones_q = jnp.where((lane >= K0) & (lane < K0 + 3), 1.0, 0.0)
        ones_k = jnp.where(((lane >= C0) & (lane < C0 + 3)) | ((lane >= L0) & (lane < L0 + 3)), 1.0, 0.0)
        ones_v = jnp.where((lane >= C0) & (lane < C0 + 3), 1.0, 0.0)
        for h in range(H):
            p, e = divmod(h, 2)
            cols = slice(p * LANES, (p + 1) * LANES)
            piece = jnp.where(prow == h, 0, jnp.where(prow == H + h, 1, jnp.where(prow == 2 * H + h, 2, -LANES)))
            aug_q = _dot(cq, (pcol == C0 + piece).astype(BF16)) + ones_q
            aug_k = _dot(ck, (pcol == K0 + piece).astype(BF16)) + ones_k
            for z_ref, g_ref, o_ref, aug, mul in ((zq_ref, gq_ref, qa_ref, aug_q, scale), (zk_ref, gk_ref, ka_ref, aug_k, 1.0)):
                blk = z_ref[:, cols]
                r = lax.rsqrt(_half_sums(blk * blk, lane) * (1.0 / HEAD_DIM) + EPS)
                y = ((blk * r) * g_ref[:, cols]).astype(BF16).astype(F32) * mul
                o_ref[0, h] = (_head_to_low(y, e, lane) + aug).astype(BF16)
            va_ref[0, h] = (_head_to_low(zv_ref[:, cols], e, lane) + ones_v).astype(BF16)

    sec = lambda n: pl.BlockSpec((ts, FW), lambda b, i: (b * nts + i, n))
    row = lambda w: pl.BlockSpec((1, w), lambda b, i: (0, 0))
    head = pl.BlockSpec((1, H, ts, LANES), lambda b, i: (b, 0, i, 0))
    hshape = jax.ShapeDtypeStruct((B, H, S, LANES), BF16)
    return _pcall(
        body, name="attn_prep", grid=(B, nts),
        in_specs=[sec(0), sec(1), sec(2), pl.BlockSpec((ts, LANES), lambda b, i: (b * nts + i, 0)), row(LANES), row(FW), row(FW)],
        out_specs=[head, head, head], out_shape=[hshape, hshape, hshape],
        scratch_shapes=[pltpu.VMEM((1, LANES), F32)],
    )(z, z, z, zf, bfp, gq, gk)


def _attn_fwd(qa, ka, va, *, t, G):
    B, H, S, _ = qa.shape
    n = S // t

    def body(q_ref, k_ref, v_ref, a_ref, qb_ref):
        i = pl.program_id(2)
        lane = _lane_iota(t)
        row = lax.broadcasted_iota(jnp.int32, (t, t), 0)
        col = lax.broadcasted_iota(jnp.int32, (t, t), 1)
        qs = [q_ref[0, e] for e in range(G)]

        def step(start, width, carry, masked):
            rows = pl.ds(pl.multiple_of(start, t), width)
            new = []
            for e in range(G):
                m, acc = carry[e]
                s = _dot_nt(qs[e], k_ref[0, e, rows, :])
                if masked:
                    s = jnp.where(col <= row, s, NEG_INF)
                m_new = jnp.maximum(m, jnp.max(s, axis=-1, keepdims=True))
                p = jnp.exp(s - m_new)
                new.append((m_new, jnp.exp(m - m_new) * acc + _dot(p.astype(BF16), v_ref[0, e, rows, :])))
            return tuple(new)

        carry = tuple((jnp.full((t, 1), NEG_INF, F32), jnp.zeros((t, LANES), F32)) for _ in range(G))
        carry = lax.fori_loop(0, i // 2, lambda j, c: step(j * (2 * t), 2 * t, c, False), carry)
        carry = lax.fori_loop(2 * (i // 2), i, lambda j, c: step(j * t, t, c, False), carry)
        carry = step(i * t, t, carry, True)
        outs = []
        for e in range(G):
            m, acc = carry[e]
            l = jnp.sum(jnp.where(lane == C0, acc, 0.0), axis=-1, keepdims=True)
            outs.append(acc / l)
            qb_ref[0, e] = _put3(qs[e].astype(F32), lane, L0, _split3(-(m + jnp.log(l)))).astype(BF16)
        for pp in range(G // 2):
            a_ref[:, pp * LANES:(pp + 1) * LANES] = jnp.where(lane < HEAD_DIM, outs[2 * pp], pltpu.roll(outs[2 * pp + 1], HEAD_DIM, 1))

    return _pcall(
        body, name="attn_fwd", grid=(B, H // G, n),
        in_specs=[pl.BlockSpec((1, G, t, LANES), lambda b, p, i: (b, p, i, 0)),
                  pl.BlockSpec((1, G, S, LANES), lambda b, p, i: (b, p, 0, 0)),
                  pl.BlockSpec((1, G, S, LANES), lambda b, p, i: (b, p, 0, 0))],
        out_specs=[pl.BlockSpec((t, G * HEAD_DIM), lambda b, p, i: (b * n + i, p)),
                   pl.BlockSpec((1, G, t, LANES), lambda b, p, i: (b, p, i, 0))],
        out_shape=[jax.ShapeDtypeStruct((B * S, H * HEAD_DIM), F32), jax.ShapeDtypeStruct((B, H, S, LANES), BF16)],
    )(qa, ka, va)


def _conv_fwd(z, cw, cb, *, B, S, sec_w, rc):
    C = cw.shape[1]
    nct = C // LANES
    off_a, off_b = 4 * sec_w // LANES, 5 * sec_w // LANES

    def body(ga_ref, gb_ref, w_ref, b_ref, uc_ref, buf):
        buf[0:HALO, :] = jnp.zeros((HALO, LANES), F32)
        buf[HALO:HALO + S, :] = ga_ref[...] * _sig(gb_ref[...])

        def chunk(cidx, carry):
            r0 = pl.multiple_of(cidx * rc, rc)
            acc = jnp.zeros((rc, LANES), F32)
            for j in range(CONV_K):
                acc = acc + w_ref[j:j + 1, :] * buf[pl.ds(r0 + (HALO - CONV_K + 1 + j), rc), :]
            uc_ref[pl.ds(r0, rc), :] = acc + b_ref[...]
            return carry

        lax.fori_loop(0, S // rc, chunk, 0)

    return _pcall(
        body, name="conv_fwd", grid=(B, nct),
        in_specs=[pl.BlockSpec((S, LANES), lambda b, c: (b, off_a + c)), pl.BlockSpec((S, LANES), lambda b, c: (b, off_b + c)),
                  pl.BlockSpec((CONV_K, LANES), lambda b, c: (0, c)), pl.BlockSpec((1, LANES), lambda b, c: (0, c))],
        out_specs=pl.BlockSpec((S, LANES), lambda b, c: (b, c)),
        out_shape=jax.ShapeDtypeStruct((B * S, C), F32),
        scratch_shapes=[pltpu.VMEM((HALO + S, LANES), F32)],
    )(z, z, cw, cb)


def _ln_parts(uc, lg, lb):
    mu = jnp.mean(uc, axis=-1, keepdims=True)
    d = uc - mu
    rs = lax.rsqrt(jnp.mean(d * d, axis=-1, keepdims=True) + EPS)
    un = d * rs
    return rs, un, un * lg + lb


def _conv_post(uc, z, lg, lb, *, tm, sec_w):
    T, C = uc.shape
    off_c = 6 * sec_w // C

    def body(uc_ref, gc_ref, lg_ref, lb_ref, uo_ref):
        _, _, ul = _ln_parts(uc_ref[...], lg_ref[...], lb_ref[...])
        gc = gc_ref[...]
        uo_ref[...] = ((ul * _sig(ul)) * (gc * _sig(gc))).astype(BF16)

    return _pcall(
        body, name="conv_post", grid=(T // tm,),
        in_specs=[pl.BlockSpec((tm, C), lambda i: (i, 0)), pl.BlockSpec((tm, C), lambda i: (i, off_c)),
                  pl.BlockSpec((1, C), lambda i: (0, 0)), pl.BlockSpec((1, C), lambda i: (0, 0))],
        out_specs=pl.BlockSpec((tm, C), lambda i: (i, 0)), out_shape=jax.ShapeDtypeStruct((T, C), BF16),
    )(uc, z, lg, lb)


def _outproj(a, z, uo, x, tgt, wo, *, tm, sec_w):
    T, D = x.shape
    FW = a.shape[1]
    C = uo.shape[1]

    def body(a_ref, gf_ref, uo_ref, x_ref, t_ref, wo_ref, dout_ref, dy_ref, dwo_ref, loss_ref):
        @pl.when(pl.program_id(0) == 0)
        def _():
            dwo_ref[...] = jnp.zeros_like(dwo_ref)
            loss_ref[...] = jnp.zeros_like(loss_ref)

        gf = gf_ref[...]
        ao = (a_ref[...] * (gf * _sig(gf))).astype(BF16)
        uo = uo_ref[...]
        out = x_ref[...] + (_dot(ao, wo_ref[0:FW, :]) + _dot(uo, wo_ref[FW:FW + C, :]))
        err = out - t_ref[...]
        loss_ref[...] += jnp.sum(err * err) * (0.5 / D)
        dout = err * (1.0 / D)
        dout_ref[...] = dout
        db = dout.astype(BF16)
        dy_ref[:, 0:FW] = _dot_nt(db, wo_ref[0:FW, :])
        dy_ref[:, FW:FW + C] = _dot_nt(db, wo_ref[FW:FW + C, :])
        dwo_ref[0:FW, :] += _dot_tn(ao, db)
        dwo_ref[FW:FW + C, :] += _dot_tn(uo, db)

    tok = lambda w, n=0: pl.BlockSpec((tm, w), lambda i: (i, n))
    return _pcall(
        body, name="outproj", grid=(T // tm,),
        in_specs=[tok(FW), tok(sec_w, 3), tok(C), tok(D), tok(D), pl.BlockSpec((FW + C, D), lambda i: (0, 0))],
        out_specs=[tok(D), tok(FW + C), pl.BlockSpec((FW + C, D), lambda i: (0, 0)), pl.BlockSpec((8, LANES), lambda i: (0, 0))],
        out_shape=[jax.ShapeDtypeStruct((T, D), F32), jax.ShapeDtypeStruct((T, FW + C), F32),
                   jax.ShapeDtypeStruct((FW + C, D), F32), jax.ShapeDtypeStruct((8, LANES), F32)],
    )(a, z, uo, x, tgt, wo)


def _conv_post_bwd(dy, uc, z, lg, lb, *, tm, sec_w):
    T, C = uc.shape
    FW = dy.shape[1] - C
    off_c = 6 * sec_w // C

    def body(dy_ref, uc_ref, gc_ref, lg_ref, lb_ref, dgc_ref, duc_ref, small_ref):
        @pl.when(pl.program_id(0) == 0)
        def _():
            small_ref[...] = jnp.zeros_like(small_ref)

        lg = lg_ref[...]
        rs, un, ul = _ln_parts(uc_ref[...], lg, lb_ref[...])
        s_ul = _sig(ul)
        gc = gc_ref[...]
        s_gc = _sig(gc)
        duo = dy_ref[...]
        dgc_ref[...] = (duo * (ul * s_ul) * _dsilu(gc, s_gc)).astype(BF16)
        dul = duo * (gc * s_gc) * _dsilu(ul, s_ul)
        dun = dul * lg
        duc = rs * (dun - jnp.mean(dun, axis=-1, keepdims=True) - un * jnp.mean(dun * un, axis=-1, keepdims=True))
        duc_ref[...] = duc
        small_ref[0:1, :] += jnp.sum(dul * un, axis=0, keepdims=True)
        small_ref[8:9, :] += jnp.sum(dul, axis=0, keepdims=True)
        small_ref[16:17, :] += jnp.sum(duc, axis=0, keepdims=True)

    tok = lambda n: pl.BlockSpec((tm, C), lambda i: (i, n))
    row = pl.BlockSpec((1, C), lambda i: (0, 0))
    return _pcall(
        body, name="conv_post_bwd", grid=(T // tm,),
        in_specs=[tok(FW // C), tok(0), tok(off_c), row, row],
        out_specs=[tok(0), tok(0), pl.BlockSpec((24, C), lambda i: (0, 0))],
        out_shape=[jax.ShapeDtypeStruct((T, C), BF16), jax.ShapeDtypeStruct((T, C), F32), jax.ShapeDtypeStruct((24, C), F32)],
    )(dy, uc, z, lg, lb)


def _conv_bwd(duc, z, cw, *, B, S, sec_w, rc):
    C = cw.shape[1]
    nct = C // LANES
    off_a, off_b = 4 * sec_w // LANES, 5 * sec_w // LANES

    def body(duc_ref, ga_ref, gb_ref, w_ref, dga_ref, dgb_ref, dw_ref, ubuf, dbuf):
        @pl.when(pl.program_id(1) == 0)
        def _():
            dw_ref[...] = jnp.zeros_like(dw_ref)

        ubuf[0:HALO, :] = jnp.zeros((HALO, LANES), F32)
        ubuf[HALO:HALO + S, :] = ga_ref[...] * _sig(gb_ref[...])
        dbuf[0:S, :] = duc_ref[...]
        dbuf[S:S + HALO, :] = jnp.zeros((HALO, LANES), F32)

        def chunk(cidx, carry):
            r0 = pl.multiple_of(cidx * rc, rc)
            acc = jnp.zeros((rc, LANES), F32)
            for j in range(CONV_K):
                acc = acc + w_ref[j:j + 1, :] * dbuf[pl.ds(r0 + (CONV_K - 1 - j), rc), :]
            rows = pl.ds(r0, rc)
            ga, sg = ga_ref[rows, :], _sig(gb_ref[rows, :])
            dga_ref[rows, :] = (acc * sg).astype(BF16)
            dgb_ref[rows, :] = (acc * ga * sg * (1.0 - sg)).astype(BF16)
            return carry

        lax.fori_loop(0, S // rc, chunk, 0)
        d = duc_ref[...]
        for j in range(CONV_K):
            dw_ref[j:j + 1, :] += jnp.sum(d * ubuf[pl.ds(HALO - CONV_K + 1 + j, S), :], axis=0, keepdims=True)

    return _pcall(
        body, name="conv_bwd", grid=(nct, B),
        in_specs=[pl.BlockSpec((S, LANES), lambda c, b: (b, c)), pl.BlockSpec((S, LANES), lambda c, b: (b, off_a + c)),
                  pl.BlockSpec((S, LANES), lambda c, b: (b, off_b + c)), pl.BlockSpec((CONV_K, LANES), lambda c, b: (0, c))],
        out_specs=[pl.BlockSpec((S, LANES), lambda c, b: (b, c)), pl.BlockSpec((S, LANES), lambda c, b: (b, c)),
                   pl.BlockSpec((CONV_K, LANES), lambda c, b: (0, c))],
        out_shape=[jax.ShapeDtypeStruct((B * S, C), BF16), jax.ShapeDtypeStruct((B * S, C), BF16),
                   jax.ShapeDtypeStruct((CONV_K, C), F32)],
        scratch_shapes=[pltpu.VMEM((HALO + S, LANES), F32), pltpu.VMEM((S + HALO, LANES), F32)],
    )(duc, z, z, cw)


def _attn_bwd_prep(dy, a, z, *, B, S, H, ts, sec_w):
    FW = H * HEAD_DIM
    nts = S // ts

    def body(dy_ref, a_ref, gf_ref, dgf_ref, do_ref):
        lane = _lane_iota(ts)
        for p in range(H // 2):
            cols = slice(p * LANES, (p + 1) * LANES)
            gf, av, dya = gf_ref[:, cols], a_ref[:, cols], dy_ref[:, cols]
            sg = _sig(gf)
            dgf_ref[:, cols] = (dya * av * _dsilu(gf, sg)).astype(BF16)
            da = dya * (gf * sg)
            nd = -_half_sums(da * av, lane)
            for e in range(2):
                nd_e = nd[:, e * HEAD_DIM:e * HEAD_DIM + 1]
                do_ref[0, 2 * p + e] = _put3(_head_to_low(da, e, lane), lane, C0, _split3(nd_e)).astype(BF16)

    tok = lambda n: pl.BlockSpec((ts, FW), lambda b, i: (b * nts + i, n))
    return _pcall(
        body, name="attn_bwd_prep", grid=(B, nts),
        in_specs=[tok(0), tok(0), pl.BlockSpec((ts, sec_w), lambda b, i: (b * nts + i, 3))],
        out_specs=[tok(0), pl.BlockSpec((1, H, ts, LANES), lambda b, i: (b, 0, i, 0))],
        out_shape=[jax.ShapeDtypeStruct((B * S, FW), BF16), jax.ShapeDtypeStruct((B, H, S, LANES), BF16)],
    )(dy, a, z)


def _attn_bwd(qb, ka, va, doa, *, t, G):
    B, H, S, _ = qb.shape
    n = S // t
    assert n % 2 == 0

    def body(q_ref, k_ref, v_ref, do_ref, dq_ref, dk_ref, dv_ref):
        j = pl.program_id(2)

        @pl.when(j == 0)
        def _():
            dq_ref[...] = jnp.zeros_like(dq_ref)

        row = lax.broadcasted_iota(jnp.int32, (t, t), 0)
        col = lax.broadcasted_iota(jnp.int32, (t, t), 1)
        ks = [k_ref[0, e] for e in range(G)]
        vs = [v_ref[0, e] for e in range(G)]

        def step(start, width, carry, masked):
            rows = pl.ds(pl.multiple_of(start, t), width)
            new = []
            for e in range(G):
                dk, dv = carry[e]
                q, do = q_ref[0, e, rows, :], do_ref[0, e, rows, :]
                st = _dot_nt(ks[e], q)
                dpt = _dot_nt(vs[e], do)
                if masked:
                    st = jnp.where(row <= col, st, NEG_INF)
                pt = jnp.exp(st)
                dsb = (pt * dpt).astype(BF16)
                dq_ref[0, e, rows, :] += _dot_tn(dsb, ks[e])
                new.append((dk + _dot(dsb, q), dv + _dot(pt.astype(BF16), do)))
            return tuple(new)

        zero = jnp.zeros((t, LANES), F32)
        carry = step(j * t, t, tuple((zero, zero) for _ in range(G)), True)
        first_pair = (j + 2) // 2
        carry = lax.fori_loop(j + 1, 2 * first_pair, lambda i, c: step(i * t, t, c, False), carry)
        carry = lax.fori_loop(first_pair, n // 2, lambda i, c: step(i * (2 * t), 2 * t, c, False), carry)
        for e in range(G):
            dk_ref[0, e], dv_ref[0, e] = carry[e]

    full = pl.BlockSpec((1, G, S, LANES), lambda b, h, j: (b, h, 0, 0))
    tile = pl.BlockSpec((1, G, t, LANES), lambda b, h, j: (b, h, j, 0))
    shape = jax.ShapeDtypeStruct((B, H, S, LANES), F32)
    return _pcall(
        body, name="attn_bwd", grid=(B, H // G, n),
        in_specs=[full, tile, tile, full], out_specs=[full, tile, tile], out_shape=[shape, shape, shape],
    )(qb, ka, va, doa)


def _attn_post_bwd(dq, dk, dv, z, zf, bfp, gq, gk, *, B, S, H, ts):
    FW = H * HEAD_DIM
    nts = S // ts
    scale = HEAD_DIM ** -0.5

    def body(dq_ref, dk_ref, dv_ref, zq_ref, zk_ref, zf_ref, bf_ref, gq_ref, gk_ref,
             dzq_ref, dzk_ref, dzv_ref, dzf_ref, small_ref, carry):
        first = (pl.program_id(0) == 0) & (pl.program_id(1) == 0)

        @pl.when(first)
        def _():
            small_ref[...] = jnp.zeros_like(small_ref)

        @pl.when(pl.program_id(1) == 0)
        def _():
            carry[...] = jnp.zeros_like(carry)

        lane = _lane_iota(ts)
        dcs = jnp.zeros((ts, LANES), F32)
        for p in range(H // 2):
            cols = slice(p * LANES, (p + 1) * LANES)
            for d_ref, z_ref, g_ref, o_ref, srow, mul in ((dq_ref, zq_ref, gq_ref, dzq_ref, 0, scale),
                                                          (dk_ref, zk_ref, gk_ref, dzk_ref, 1, 1.0)):
                dy = jnp.where(lane < HEAD_DIM, d_ref[0, 2 * p], pltpu.roll(d_ref[0, 2 * p + 1], HEAD_DIM, 1)) * mul
                xv, g = z_ref[:, cols], g_ref[:, cols]
                r = lax.rsqrt(_half_sums(xv * xv, lane) * (1.0 / HEAD_DIM) + EPS)
                dxh = dy * g
                mm = _half_sums(dxh * xv, lane) * (1.0 / HEAD_DIM)
                o_ref[:, cols] = (r * (dxh - xv * (r * r) * mm)).astype(BF16)
                small_ref[8 * srow:8 * srow + 1, cols] += jnp.sum(dy * xv * r, axis=0, keepdims=True)
            dzv_ref[:, cols] = jnp.where(lane < HEAD_DIM, dv_ref[0, 2 * p], pltpu.roll(dv_ref[0, 2 * p + 1], HEAD_DIM, 1)).astype(BF16)
            for e in range(2):
                h = 2 * p + e
                dc = jnp.sum(jnp.where(lane == C0, dq_ref[0, h], 0.0) - jnp.where(lane == K0, dk_ref[0, h], 0.0),
                             axis=-1, keepdims=True)
                dcs = jnp.where(lane == h, dc, dcs)
        ri = lax.broadcasted_iota(jnp.int32, (ts, ts), 0)
        ci = lax.broadcasted_iota(jnp.int32, (ts, ts), 1)
        tri = (ci >= ri).astype(BF16)
        dl = carry[...]
        for piece in _split3(dcs):
            dl = dl + _dot(tri, piece)
        carry[...] = dl[0:1, :]
        xf = zf_ref[...] + bf_ref[...]
        df = jnp.where(lane < H, dl * _sig(-xf), 0.0)
        dzf_ref[...] = df.astype(BF16)
        small_ref[16:17, 0:LANES] += jnp.sum(df, axis=0, keepdims=True)

    rev = lambda b, i: (b, 0, nts - 1 - i, 0)
    head = pl.BlockSpec((1, H, ts, LANES), rev)
    tokrow = lambda b, i: b * nts + nts - 1 - i
    sec = lambda n: pl.BlockSpec((ts, FW), lambda b, i: (tokrow(b, i), n))
    narrow = pl.BlockSpec((ts, LANES), lambda b, i: (tokrow(b, i), 0))
    row = lambda w: pl.BlockSpec((1, w), lambda b, i: (0, 0))
    return _pcall(
        body, name="attn_post_bwd", grid=(B, nts),
        in_specs=[head, head, head, sec(0), sec(1), narrow, row(LANES), row(FW), row(FW)],
        out_specs=[sec(0), sec(0), sec(0), narrow, pl.BlockSpec((24, FW), lambda b, i: (0, 0))],
        out_shape=[jax.ShapeDtypeStruct((B * S, FW), BF16)] * 3 + [jax.ShapeDtypeStruct((B * S, LANES), BF16),
                                                                   jax.ShapeDtypeStruct((24, FW), F32)],
        scratch_shapes=[pltpu.VMEM((1, LANES), F32)],
    )(dq, dk, dv, z, z, zf, bfp, gq, gk)


def _inproj_bwd_x(dzs, dzf, wmt, wft, x, g, dout, *, tm):
    T, D = x.shape
    nj = len(dzs)
    tn = dzs[0].shape[1]

    def body(*refs):
        dz_refs = refs[:nj]
        dzf_ref, wm_ref, wf_ref, x_ref, g_ref, dout_ref, gx_ref, dg_ref, acc = refs[nj:]
        i, j = pl.program_id(0), pl.program_id(1)

        @pl.when((i == 0) & (j == 0))
        def _():
            dg_ref[...] = jnp.zeros_like(dg_ref)

        @pl.when(j == 0)
        def _():
            acc[...] = _dot(dzf_ref[...], wf_ref[...])

        for s in range(nj):
            @pl.when(j == s)
            def _():
                acc[...] += _dot(dz_refs[s][...], wm_ref[...])

        @pl.when(j == nj - 1)
        def _():
            dh, xv, g = acc[...], x_ref[...], g_ref[...]
            r = lax.rsqrt(jnp.mean(xv * xv, axis=-1, keepdims=True) + EPS)
            dxh = dh * g
            mm = jnp.mean(dxh * xv, axis=-1, keepdims=True)
            gx_ref[...] = dout_ref[...] + r * (dxh - xv * (r * r) * mm)
            dg_ref[0:1, :] += jnp.sum(dh * xv * r, axis=0, keepdims=True)

    tok = pl.BlockSpec((tm, D), lambda i, j: (i, 0))
    sec = pl.BlockSpec((tm, tn), lambda i, j: (i, 0))
    return _pcall(
        body, name="inproj_bwd_x", grid=(T // tm, nj),
        in_specs=[sec] * nj + [pl.BlockSpec((tm, LANES), lambda i, j: (i, 0)),
                               pl.BlockSpec((tn, D), lambda i, j: (j, 0)), pl.BlockSpec((LANES, D), lambda i, j: (0, 0)),
                               tok, pl.BlockSpec((1, D), lambda i, j: (0, 0)), tok],
        out_specs=[tok, pl.BlockSpec((8, D), lambda i, j: (0, 0))],
        out_shape=[jax.ShapeDtypeStruct((T, D), F32), jax.ShapeDtypeStruct((8, D), F32)],
        scratch_shapes=[pltpu.VMEM((tm, D), F32)],
    )(*dzs, dzf, wmt, wft, x, g, dout)


def _inproj_bwd_w(h, dz, *, tm, name):
    T, D = h.shape
    W = dz.shape[1]
    ni = T // tm

    def body(h_ref, dz_ref, dw_ref, acc):
        i = pl.program_id(0)

        @pl.when(i == 0)
        def _():
            acc[...] = jnp.zeros_like(acc)

        acc[...] += _dot_tn(dz_ref[...], h_ref[...])

        @pl.when(i == ni - 1)
        def _():
            dw_ref[...] = acc[...].astype(BF16)

    return _pcall(
        body, name=name, grid=(ni,),
        in_specs=[pl.BlockSpec((tm, D), lambda i: (i, 0)), pl.BlockSpec((tm, W), lambda i: (i, 0))],
        out_specs=pl.BlockSpec((W, D), lambda i: (0, 0)), out_shape=jax.ShapeDtypeStruct((W, D), BF16),
        scratch_shapes=[pltpu.VMEM((W, D), F32)],
    )(h, dz)


def _adamw(w, g, m, v):
    m = ADAM_B1 * m + (1.0 - ADAM_B1) * g
    v = ADAM_B2 * v + (1.0 - ADAM_B2) * (g * g)
    m_hat = m / (1.0 - ADAM_B1 ** ADAM_STEP)
    v_hat = v / (1.0 - ADAM_B2 ** ADAM_STEP)
    return -ADAM_LR * (m_hat / (jnp.sqrt(v_hat) + ADAM_EPS) + ADAM_WD * w), m, v


def _adam_update(w, m, v, parts, *, name, tr):
    R, Cc = w.shape[0], w.shape[-1]

    def body(w_ref, m_ref, v_ref, p_ref, g_ref, d_ref, nm_ref, nv_ref):
        g = p_ref[0].astype(F32)
        for k in range(1, N_DEV):
            g = g + p_ref[k].astype(F32)
        g_ref[...] = g
        d_ref[...], nm_ref[...], nv_ref[...] = _adamw(w_ref[...], g, m_ref[...], v_ref[...])

    blk = pl.BlockSpec((tr, Cc), lambda i: (i, 0)) if w.ndim == 2 else pl.BlockSpec((tr, None, Cc), lambda i: (i, 0, 0))
    shape = jax.ShapeDtypeStruct(w.shape, F32)
    return _pcall(
        body, name=name, grid=(pl.cdiv(R, tr),),
        in_specs=[blk, blk, blk, pl.BlockSpec((N_DEV, tr, Cc), lambda i: (0, i, 0))],
        out_specs=[blk, blk, blk, blk], out_shape=[shape, shape, shape, shape],
    )(w, m, v, parts)


MESH = pl.DeviceIdType.MESH
ANY = pl.BlockSpec(memory_space=pl.ANY)


def _flip(v, bit):
    return 1 - v if bit else v


def _allgather(shards):
    n = len(shards)

    def body(*refs):
        srcs, outs = refs[:n], refs[n:2 * n]
        send_sems, recv_sems, local_sems = refs[2 * n:]
        x, y, c = lax.axis_index("x"), lax.axis_index("y"), lax.axis_index("c")
        sibling = (x, y, 1 - c)
        chips = [(1 - x, y), (x, 1 - y), (1 - x, 1 - y)]
        slot = lambda px, py, pc: 4 * px + 2 * py + pc

        def copy(a, k, block, to, src=None):
            dst = outs[a].at[slot(*block)]
            return pltpu.make_async_remote_copy(src_ref=dst if src is None else src, dst_ref=dst, send_sem=send_sems.at[7 * a + k],
                                                recv_sem=recv_sems.at[7 * a + k], device_id=to, device_id_type=MESH)

        mine = [pltpu.make_async_copy(srcs[a], outs[a].at[slot(x, y, c)], local_sems.at[a]) for a in range(n)]
        sends = []
        for a in range(n):
            mine[a].start()
            sends.append(copy(a, 0, (x, y, c), sibling, src=srcs[a]))
            sends += [copy(a, 1 + j, (x, y, c), (*chip, c), src=srcs[a]) for j, chip in enumerate(chips)]
        for cp in sends:
            cp.start()
        for a in range(n):
            for j, chip in enumerate(chips):
                copy(a, 1 + j, (*chip, c), (x, y, c)).wait_recv()
                fwd = copy(a, 4 + j, (*chip, c), sibling)
                fwd.start()
                sends.append(fwd)
        for a in range(n):
            copy(a, 0, (x, y, 1 - c), (x, y, c)).wait_recv()
            for j, chip in enumerate(chips):
                copy(a, 4 + j, (*chip, 1 - c), (x, y, c)).wait_recv()
        for cp in sends:
            cp.wait_send()
        for cp in mine:
            cp.wait()

    return pl.pallas_call(
        body, name="allgather_weights", in_specs=[ANY] * n, out_specs=[ANY] * n,
        out_shape=[jax.ShapeDtypeStruct((N_DEV,) + s.shape, s.dtype) for s in shards],
        scratch_shapes=[pltpu.SemaphoreType.DMA((7 * n,)), pltpu.SemaphoreType.DMA((7 * n,)), pltpu.SemaphoreType.DMA((n,))],
    )(*shards)


def _exchange(scatter, gather):
    arrs = list(scatter) + list(gather)
    n, ns = len(arrs), len(scatter)

    def body(*refs):
        srcs, outs = refs[:n], refs[n:2 * n]
        send_sems, recv_sems, local_sems = refs[2 * n:]
        x, y, c = lax.axis_index("x"), lax.axis_index("y"), lax.axis_index("c")
        me = 4 * x + 2 * y + c
        pending = []
        for a in range(n):
            piece = lambda k: srcs[a].at[k] if a < ns else srcs[a]
            loc = pltpu.make_async_copy(piece(me), outs[a].at[me], local_sems.at[a])
            loc.start()
            pending.append(loc.wait)
            for d in range(1, N_DEV):
                peer = (_flip(x, d & 4), _flip(y, d & 2), _flip(c, d & 1))
                pid = 4 * peer[0] + 2 * peer[1] + peer[2]
                sem = 7 * a + d - 1
                out = pltpu.make_async_remote_copy(src_ref=piece(pid), dst_ref=outs[a].at[me], send_sem=send_sems.at[sem],
                                                   recv_sem=recv_sems.at[sem], device_id=peer, device_id_type=MESH)
                out.start()
                back = pltpu.make_async_remote_copy(src_ref=piece(me), dst_ref=outs[a].at[pid], send_sem=send_sems.at[sem],
                                                    recv_sem=recv_sems.at[sem], device_id=peer, device_id_type=MESH)
                pending += [out.wait_send, back.wait_recv]
        for wait in pending:
            wait()

    return pl.pallas_call(
        body, name="exchange_grads", in_specs=[ANY] * n, out_specs=[ANY] * n,
        out_shape=[jax.ShapeDtypeStruct(s.shape, s.dtype) for s in scatter]
        + [jax.ShapeDtypeStruct((N_DEV,) + s.shape, s.dtype) for s in gather],
        scratch_shapes=[pltpu.SemaphoreType.DMA((7 * n,)), pltpu.SemaphoreType.DMA((7 * n,)), pltpu.SemaphoreType.DMA((n,))],
    )(*arrs)


def _local_step(x, tgt, norm_g, wmt, wft, bfp, gq, gk, cw, cb, lg, lb, wo, *, B, S, H):
    T, D = x.shape
    sec_w = H * HEAD_DIM
    tm = min(512, T)
    ts = min(256, S)
    rc = min(64, S)
    z, zf, h = _inproj_fwd(x, norm_g, wmt, wft, tm=tm, tn=sec_w)
    qa, ka, va = _attn_prep(z, zf, bfp, gq, gk, B=B, S=S, H=H, ts=ts)
    a, qb = _attn_fwd(qa, ka, va, t=ts, G=4)
    uc = _conv_fwd(z, cw, cb, B=B, S=S, sec_w=sec_w, rc=rc)
    uo = _conv_post(uc, z, lg, lb, tm=ts, sec_w=sec_w)
    dout, dy, dwo, loss = _outproj(a, z, uo, x, tgt, wo, tm=ts, sec_w=sec_w)
    dgc, duc, small_c = _conv_post_bwd(dy, uc, z, lg, lb, tm=ts, sec_w=sec_w)
    dga, dgb, dcw = _conv_bwd(duc, z, cw, B=B, S=S, sec_w=sec_w, rc=rc)
    dgf, doa = _attn_bwd_prep(dy, a, z, B=B, S=S, H=H, ts=ts, sec_w=sec_w)
    dq, dk, dv = _attn_bwd(qb, ka, va, doa, t=ts, G=4)
    dzq, dzk, dzv, dzf, small_a = _attn_post_bwd(dq, dk, dv, z, zf, bfp, gq, gk, B=B, S=S, H=H, ts=ts)
    dzs = [dzq, dzk, dzv, dgf, dga, dgb, dgc]
    dws = [_inproj_bwd_w(h, dz, tm=tm, name=f"inproj_bwd_w{s}") for s, dz in enumerate(dzs)]
    dwf = _inproj_bwd_w(h, dzf, tm=tm, name="inproj_bwd_wf")
    gx, dg = _inproj_bwd_x(dzs, dzf, wmt, wft, x, norm_g, dout, tm=tm)
    small = jnp.concatenate([dg, small_a[16:24], small_a[0:8], small_a[8:16], small_c[16:24], small_c[0:8], small_c[8:16],
                             jnp.pad(loss, ((0, 0), (0, D - LANES)))], axis=0)
    return gx, dws, dwf, dwo, dcw, small


def kernel(x, norm_g, w_in, b_forget, q_norm_g, k_norm_g, conv_w, conv_b, conv_ln_g, conv_ln_b, w_out, loss_target, m_norm_g, m_w_in, m_b_forget, m_q_norm_g, m_k_norm_g, m_conv_w, m_conv_b, m_conv_ln_g, m_conv_ln_b, m_w_out, v_norm_g, v_w_in, v_b_forget, v_q_norm_g, v_k_norm_g, v_conv_w, v_conv_b, v_conv_ln_g, v_conv_ln_b, v_w_out):
    B, S, D = x.shape
    H = q_norm_g.shape[1]
    FW = H * HEAD_DIM
    assert q_norm_g.shape[2] == HEAD_DIM and FW == D and conv_b.shape[1] == D and H <= LANES // 3
    ncol = w_in.shape[2]
    T = B * S

    to_t = lambda t: jnp.transpose(t, (2, 0, 1))
    from_t = lambda t: jnp.transpose(t, (1, 2, 0))
    wt, mt, vt = to_t(w_in), to_t(m_w_in), to_t(v_w_in)

    wg, wog, cwg = _allgather([wt.reshape(ncol, D).astype(BF16), w_out[0].astype(BF16), conv_w[0]])
    wmt, wft = _w_relayout(wg, D=D, H=H)
    wo = wog.reshape(N_DEV * wog.shape[1], D)
    cw = jnp.transpose(cwg, (1, 0, 2)).reshape(CONV_K, D)
    bfp = jnp.pad(b_forget, ((0, 0), (0, LANES - H)))
    gq, gk = q_norm_g.reshape(1, FW), k_norm_g.reshape(1, FW)

    gx, dws, dwf, dwo, dcw, small = _local_step(
        x.reshape(T, D), loss_target.reshape(T, D), norm_g, wmt, wft, bfp, gq, gk, cw, conv_b, conv_ln_g, conv_ln_b, wo,
        B=B, S=S, H=H)

    dw_parts = _dw_relayout(dws, dwf, R=ncol, H=H)
    dwo_parts = dwo.reshape(N_DEV, dwo.shape[0] // N_DEV, D).astype(BF16)
    dcw_parts = jnp.transpose(dcw.reshape(CONV_K, N_DEV, D // N_DEV), (1, 0, 2))
    r_w, r_wo, r_cw, r_small = _exchange([dw_parts, dwo_parts, dcw_parts], [small])

    g_w, d_w, nm_w, nv_w = [from_t(t) for t in _adam_update(wt, mt, vt, r_w, name="adam_w_in", tr=min(128, D))]
    g_wo, d_wo, nm_wo, nv_wo = [t[None] for t in _adam_update(w_out[0], m_w_out[0], v_w_out[0], r_wo, name="adam_w_out",
                                                               tr=min(128, w_out.shape[1]))]
    g_cw, d_cw, nm_cw, nv_cw = [t[None] for t in _adam_update(conv_w[0], m_conv_w[0], v_conv_w[0], r_cw, name="adam_conv_w", tr=CONV_K)]
    tile = lambda t: jnp.pad(t.reshape(1, -1), ((0, 7), (0, D - t.size)))
    pack = lambda *ps: jnp.concatenate([tile(p) for p in ps] + [jnp.zeros((8, D), F32)], axis=0)
    small_w = pack(norm_g, b_forget, q_norm_g, k_norm_g, conv_b, conv_ln_g, conv_ln_b)
    small_m = pack(m_norm_g, m_b_forget, m_q_norm_g, m_k_norm_g, m_conv_b, m_conv_ln_g, m_conv_ln_b)
    small_v = pack(v_norm_g, v_b_forget, v_q_norm_g, v_k_norm_g, v_conv_b, v_conv_ln_g, v_conv_ln_b)
    g_s, d_s, nm_s, nv_s = _adam_update(small_w, small_m, small_v, r_small, name="adam_small", tr=small_w.shape[0])

    def leaves(small_t, w_t, cw_t, wo_t):
        row = lambda r, like: small_t[8 * r:8 * r + 1, :like.size].reshape(like.shape)
        return [row(0, norm_g), w_t, row(1, b_forget), row(2, q_norm_g), row(3, k_norm_g), cw_t, row(4, conv_b),
                row(5, conv_ln_g), row(6, conv_ln_b), wo_t]

    return (g_s[56, 0], gx.reshape(B, S, D), *leaves(g_s, g_w, g_cw, g_wo), *leaves(d_s, d_w, d_cw, d_wo),
            *leaves(nm_s, nm_w, nm_cw, nm_wo), *leaves(nv_s, nv_w, nv_cw, nv_wo))
```

```python
import functools

import jax
import jax.numpy as jnp
from jax import lax
from jax.experimental import pallas as pl
from jax.experimental.pallas import tpu as pltpu

F32, BF16 = jnp.float32, jnp.bfloat16
EPS = 1e-6
NEG_INF = -1e30
CONV_K = 31
HEAD_DIM = 64
ADAM_LR, ADAM_B1, ADAM_B2, ADAM_EPS, ADAM_WD, ADAM_STEP = 0.001, 0.9, 0.999, 1e-08, 0.01, 10

LANES = 128
HALO = 32
N_DEV = 8
VMEM_LIMIT = 56 * 1024 * 1024

C0, K0, L0 = HEAD_DIM, HEAD_DIM + 3, HEAD_DIM + 6


def _pcall(body, *, name, vmem=VMEM_LIMIT, **kw):
    return pl.pallas_call(body, name=name, compiler_params=pltpu.CompilerParams(vmem_limit_bytes=vmem), **kw)


def _dot(a, b):
    return jnp.dot(a, b, preferred_element_type=F32)


def _dot_nt(a, b):
    return lax.dot_general(a, b, (((1,), (1,)), ((), ())), preferred_element_type=F32)


def _dot_tn(a, b):
    return lax.dot_general(a, b, (((0,), (0,)), ((), ())), preferred_element_type=F32)


def _split3(x):
    a = x.astype(BF16)
    r = x - a.astype(F32)
    b = r.astype(BF16)
    c = (r - b.astype(F32)).astype(BF16)
    return a, b, c


def _sig(x):
    return jax.nn.sigmoid(x)


def _dsilu(x, s):
    return s * (1.0 + x * (1.0 - s))


def _lane_iota(rows):
    return lax.broadcasted_iota(jnp.int32, (rows, LANES), 1)


def _half_sums(v, lane):
    lo = jnp.sum(jnp.where(lane < HEAD_DIM, v, 0.0), axis=-1, keepdims=True)
    hi = jnp.sum(jnp.where(lane >= HEAD_DIM, v, 0.0), axis=-1, keepdims=True)
    return jnp.where(lane < HEAD_DIM, lo, hi)


def _head_to_low(v, e, lane):
    if e:
        v = pltpu.roll(v, HEAD_DIM, 1)
    return jnp.where(lane < HEAD_DIM, v, 0.0)


def _put3(base, lane, first, pieces):
    for n, p in enumerate(pieces):
        base = jnp.where(lane == first + n, p.astype(F32), base)
    return base


def _section_rows(D, H):
    FW = H * HEAD_DIM
    return 3 * FW, 3 * FW + H, 7 * FW + H


def _row_pieces(a, b, R, step):
    out = []
    while a < b:
        k = a // R
        lo = a - k * R
        hi = min(R, lo + min(step, b - a))
        out.append((k, lo, hi))
        a += hi - lo
    return out


def _w_relayout(wg, *, D, H):
    R = wg.shape[1]
    FW = H * HEAD_DIM
    f0, f1, end = _section_rows(D, H)

    def body(wg_ref, wm_ref, wf_ref):
        def put(dst_ref, d0, a, b):
            for k, lo, hi in _row_pieces(a, b, R, 256):
                dst_ref[d0:d0 + hi - lo, :] = wg_ref[k, lo:hi, :]
                d0 += hi - lo

        put(wm_ref, 0, 0, f0)
        put(wm_ref, f0, f1, end)
        wf_ref[...] = jnp.zeros_like(wf_ref)
        put(wf_ref, 0, f0, f1)

    return _pcall(body, name="w_relayout",
                  out_shape=[jax.ShapeDtypeStruct((7 * FW, D), BF16), jax.ShapeDtypeStruct((LANES, D), BF16)])(wg)


def _dw_relayout(secs, dwf, *, R, H, shards, name):
    FW, D = next(iter(secs.values())).shape
    f0, f1, end = _section_rows(D, H)
    order = sorted(secs)

    def body(*refs):
        sec_refs = dict(zip(order, refs))
        dwf_ref, out_ref = (refs[-2] if dwf is not None else None), refs[-1]

        def src(g0, g1):
            if f0 <= g0 < f1:
                return dwf_ref[g0 - f0:g1 - f0, :]
            s0 = g0 if g0 < f0 else g0 - H
            return sec_refs[s0 // FW][s0 % FW:s0 % FW + (g1 - g0), :]

        cuts = sorted({0, end, f0, f1} | {s * FW for s in range(4)} | {f1 + s * FW for s in range(5)})
        for slot, k in enumerate(shards):
            g = k * R
            while g < (k + 1) * R:
                nxt = min(min(c for c in cuts if c > g), (k + 1) * R, g + 256)
                out_ref[slot, g - k * R:nxt - k * R, :] = src(g, nxt)
                g = nxt

    args = [secs[s] for s in order] + ([dwf] if dwf is not None else [])
    return _pcall(body, name=name, out_shape=jax.ShapeDtypeStruct((len(shards), R, D), BF16))(*args)


def _inproj_fwd(x, g, wmt, wft, *, tm, tn, comm=None):
    T, D = x.shape
    nj = wmt.shape[0] // tn
    grid = (T // tm, nj)

    def body(*refs):
        (x_ref, g_ref, wm_ref, wf_ref), (z_ref, zf_ref, h_ref), (h_scr,), rider = _split_refs(refs, 4, 3, comm)
        first, last = _grid_ends(grid)
        if comm:
            comm.begin(rider, first)

        @pl.when(pl.program_id(1) == 0)
        def _():
            xv = x_ref[...]
            r = lax.rsqrt(jnp.mean(xv * xv, axis=-1, keepdims=True) + EPS)
            h = ((xv * r) * g_ref[...]).astype(BF16)
            h_scr[...] = h
            h_ref[...] = h
            zf_ref[...] = _dot_nt(h, wf_ref[...])

        z_ref[...] = _dot_nt(h_scr[...], wm_ref[...])
        if comm:
            comm.end(rider, last)

    return _pcall(
        body, name="inproj_fwd", grid=grid,
        **_with_comm(
            comm,
            [pl.BlockSpec((tm, D), lambda i, j: (i, 0)), pl.BlockSpec((1, D), lambda i, j: (0, 0)),
             pl.BlockSpec((tn, D), lambda i, j: (j, 0)), pl.BlockSpec((LANES, D), lambda i, j: (0, 0))],
            [pl.BlockSpec((tm, tn), lambda i, j: (i, j)), pl.BlockSpec((tm, LANES), lambda i, j: (i, 0)),
             pl.BlockSpec((tm, D), lambda i, j: (i, 0))],
            [jax.ShapeDtypeStruct((T, wmt.shape[0]), F32), jax.ShapeDtypeStruct((T, LANES), F32), jax.ShapeDtypeStruct((T, D), BF16)],
            [pltpu.VMEM((tm, D), BF16)]),
    )(x, g, wmt, wft, *(comm.operands() if comm else ()))


def _attn_prep(z, zf, bfp, gq, gk, *, B, S, H, ts):
    FW = H * HEAD_DIM
    nts = S // ts
    scale = HEAD_DIM ** -0.5

    def body(zq_ref, zk_ref, zv_ref, zf_ref, bf_ref, gq_ref, gk_ref, qa_ref, ka_ref, va_ref, carry):
        @pl.when(pl.program_id(1) == 0)
        def _():
            carry[...] = jnp.zeros_like(carry)

        lane = _lane_iota(ts)
        xf = zf_ref[...] + bf_ref[...]
        logf = jnp.minimum(xf, 0.0) - jnp.log1p(jnp.exp(-jnp.abs(xf)))
        logf = jnp.where(lane < H, logf, 0.0)
        ri = lax.broadcasted_iota(jnp.int32, (ts, ts), 0)
        ci = lax.broadcasted_iota(jnp.int32, (ts, ts), 1)
        tri = (ri >= ci).astype(BF16)
        c = carry[...]
        for piece in _split3(logf):
            c = c + _dot(tri, piece)
        carry[...] = c[ts - 1:ts, :]
        c1, c2, c3 = _split3(c)
        cpk = c1.astype(F32) + pltpu.roll(c2.astype(F32), H, 1) + pltpu.roll(c3.astype(F32), 2 * H, 1)
        cq, ck = cpk.astype(BF16), (-cpk).astype(BF16)
        prow = lax.broadcasted_iota(jnp.int32, (LANES, LANES), 0)
        pcol = lax.broadcasted_iota(jnp.int32, (LANES, LANES), 1)
        ones_q = jnp.where((lane >= K0) & (lane < K0 + 3), 1.0, 0.0)
        ones_k = jnp.where(((lane >= C0) & (lane < C0 + 3)) | ((lane >= L0) & (lane < L0 + 3)), 1.0, 0.0)
        ones_v = jnp.where((lane >= C0) & (lane < C0 + 3), 1.0, 0.0)
        for h in range(H):
            p, e = divmod(h, 2)
            cols = slice(p * LANES, (p + 1) * LANES)
            piece = jnp.where(prow == h, 0, jnp.where(prow == H + h, 1, jnp.where(prow == 2 * H + h, 2, -LANES)))
            aug_q = _dot(cq, (pcol == C0 + piece).astype(BF16)) + ones_q
            aug_k = _dot(ck, (pcol == K0 + piece).astype(BF16)) + ones_k
            for z_ref, g_ref, o_ref, aug, mul in ((zq_ref, gq_ref, qa_ref, aug_q, scale), (zk_ref, gk_ref, ka_ref, aug_k, 1.0)):
                blk = z_ref[:, cols]
                r = lax.rsqrt(_half_sums(blk * blk, lane) * (1.0 / HEAD_DIM) + EPS)
                y = ((blk * r) * g_ref[:, cols]).astype(BF16).astype(F32) * mul
                o_ref[0, h] = (_head_to_low(y, e, lane) + aug).astype(BF16)
            va_ref[0, h] = (_head_to_low(zv_ref[:, cols], e, lane) + ones_v).astype(BF16)

    sec = lambda n: pl.BlockSpec((ts, FW), lambda b, i: (b * nts + i, n))
    row = lambda w: pl.BlockSpec((1, w), lambda b, i: (0, 0))
    head = pl.BlockSpec((1, H, ts, LANES), lambda b, i: (b, 0, i, 0))
    hshape = jax.ShapeDtypeStruct((B, H, S, LANES), BF16)
    return _pcall(
        body, name="attn_prep", grid=(B, nts),
        in_specs=[sec(0), sec(1), sec(2), pl.BlockSpec((ts, LANES), lambda b, i: (b * nts + i, 0)), row(LANES), row(FW), row(FW)],
        out_specs=[head, head, head], out_shape=[hshape, hshape, hshape],
        scratch_shapes=[pltpu.VMEM((1, LANES), F32)],
    )(z, z, z, zf, bfp, gq, gk)


def _attn_fwd(qa, ka, va, *, t, G):
    B, H, S, _ = qa.shape
    n = S // t

    def body(q_ref, k_ref, v_ref, a_ref, qb_ref):
        i = pl.program_id(2)
        lane = _lane_iota(t)
        row = lax.broadcasted_iota(jnp.int32, (t, t), 0)
        col = lax.broadcasted_iota(jnp.int32, (t, t), 1)
        qs = [q_ref[0, e] for e in range(G)]

        def step(start, width, carry, masked):
            rows = pl.ds(pl.multiple_of(start, t), width)
            new = []
            for e in range(G):
                m, acc = carry[e]
                s = _dot_nt(qs[e], k_ref[0, e, rows, :])
                if masked:
                    s = jnp.where(col <= row, s, NEG_INF)
                m_new = jnp.maximum(m, jnp.max(s, axis=-1, keepdims=True))
                p = jnp.exp(s - m_new)
                new.append((m_new, jnp.exp(m - m_new) * acc + _dot(p.astype(BF16), v_ref[0, e, rows, :])))
            return tuple(new)

        carry = tuple((jnp.full((t, 1), NEG_INF, F32), jnp.zeros((t, LANES), F32)) for _ in range(G))
        carry = lax.fori_loop(0, i // 2, lambda j, c: step(j * (2 * t), 2 * t, c, False), carry)
        carry = lax.fori_loop(2 * (i // 2), i, lambda j, c: step(j * t, t, c, False), carry)
        carry = step(i * t, t, carry, True)
        outs = []
        for e in range(G):
            m, acc = carry[e]
            l = jnp.sum(jnp.where(lane == C0, acc, 0.0), axis=-1, keepdims=True)
            outs.append(acc / l)
            qb_ref[0, e] = _put3(qs[e].astype(F32), lane, L0, _split3(-(m + jnp.log(l)))).astype(BF16)
        for pp in range(G // 2):
            a_ref[:, pp * LANES:(pp + 1) * LANES] = jnp.where(lane < HEAD_DIM, outs[2 * pp], pltpu.roll(outs[2 * pp + 1], HEAD_DIM, 1))

    return _pcall(
        body, name="attn_fwd", grid=(B, H // G, n),
        in_specs=[pl.BlockSpec((1, G, t, LANES), lambda b, p, i: (b, p, i, 0)),
                  pl.BlockSpec((1, G, S, LANES), lambda b, p, i: (b, p, 0, 0)),
                  pl.BlockSpec((1, G, S, LANES), lambda b, p, i: (b, p, 0, 0))],
        out_specs=[pl.BlockSpec((t, G * HEAD_DIM), lambda b, p, i: (b * n + i, p)),
                   pl.BlockSpec((1, G, t, LANES), lambda b, p, i: (b, p, i, 0))],
        out_shape=[jax.ShapeDtypeStruct((B * S, H * HEAD_DIM), F32), jax.ShapeDtypeStruct((B, H, S, LANES), BF16)],
    )(qa, ka, va)


def _conv_fwd(z, cw, cb, *, B, S, sec_w, rc):
    C = cw.shape[1]
    nct = C // LANES
    off_a, off_b = 4 * sec_w // LANES, 5 * sec_w // LANES

    def body(ga_ref, gb_ref, w_ref, b_ref, uc_ref, buf):
        buf[0:HALO, :] = jnp.zeros((HALO, LANES), F32)
        buf[HALO:HALO + S, :] = ga_ref[...] * _sig(gb_ref[...])

        def chunk(cidx, carry):
            r0 = pl.multiple_of(cidx * rc, rc)
            acc = jnp.zeros((rc, LANES), F32)
            for j in range(CONV_K):
                acc = acc + w_ref[j:j + 1, :] * buf[pl.ds(r0 + (HALO - CONV_K + 1 + j), rc), :]
            uc_ref[pl.ds(r0, rc), :] = acc + b_ref[...]
            return carry

        lax.fori_loop(0, S // rc, chunk, 0)

    return _pcall(
        body, name="conv_fwd", grid=(B, nct),
        in_specs=[pl.BlockSpec((S, LANES), lambda b, c: (b, off_a + c)), pl.BlockSpec((S, LANES), lambda b, c: (b, off_b + c)),
                  pl.BlockSpec((CONV_K, LANES), lambda b, c: (0, c)), pl.BlockSpec((1, LANES), lambda b, c: (0, c))],
        out_specs=pl.BlockSpec((S, LANES), lambda b, c: (b, c)),
        out_shape=jax.ShapeDtypeStruct((B * S, C), F32),
        scratch_shapes=[pltpu.VMEM((HALO + S, LANES), F32)],
    )(z, z, cw, cb)


def _ln_parts(uc, lg, lb):
    mu = jnp.mean(uc, axis=-1, keepdims=True)
    d = uc - mu
    rs = lax.rsqrt(jnp.mean(d * d, axis=-1, keepdims=True) + EPS)
    un = d * rs
    return rs, un, un * lg + lb


def _conv_post(uc, z, lg, lb, *, tm, sec_w):
    T, C = uc.shape
    off_c = 6 * sec_w // C

    def body(uc_ref, gc_ref, lg_ref, lb_ref, uo_ref):
        _, _, ul = _ln_parts(uc_ref[...], lg_ref[...], lb_ref[...])
        gc = gc_ref[...]
        uo_ref[...] = ((ul * _sig(ul)) * (gc * _sig(gc))).astype(BF16)

    return _pcall(
        body, name="conv_post", grid=(T // tm,),
        in_specs=[pl.BlockSpec((tm, C), lambda i: (i, 0)), pl.BlockSpec((tm, C), lambda i: (i, off_c)),
                  pl.BlockSpec((1, C), lambda i: (0, 0)), pl.BlockSpec((1, C), lambda i: (0, 0))],
        out_specs=pl.BlockSpec((tm, C), lambda i: (i, 0)), out_shape=jax.ShapeDtypeStruct((T, C), BF16),
    )(uc, z, lg, lb)


def _outproj(a, z, uo, x, tgt, wo, *, tm, sec_w):
    T, D = x.shape
    FW = a.shape[1]
    C = uo.shape[1]

    def body(a_ref, gf_ref, uo_ref, x_ref, t_ref, wo_ref, dout_ref, dy_ref, dwo_ref, loss_ref):
        @pl.when(pl.program_id(0) == 0)
        def _():
            dwo_ref[...] = jnp.zeros_like(dwo_ref)
            loss_ref[...] = jnp.zeros_like(loss_ref)

        gf = gf_ref[...]
        ao = (a_ref[...] * (gf * _sig(gf))).astype(BF16)
        uo = uo_ref[...]
        out = x_ref[...] + (_dot(ao, wo_ref[0:FW, :]) + _dot(uo, wo_ref[FW:FW + C, :]))
        err = out - t_ref[...]
        loss_ref[...] += jnp.sum(err * err) * (0.5 / D)
        dout = err * (1.0 / D)
        dout_ref[...] = dout
        db = dout.astype(BF16)
        dy_ref[:, 0:FW] = _dot_nt(db, wo_ref[0:FW, :])
        dy_ref[:, FW:FW + C] = _dot_nt(db, wo_ref[FW:FW + C, :])
        dwo_ref[0:FW, :] += _dot_tn(ao, db)
        dwo_ref[FW:FW + C, :] += _dot_tn(uo, db)

    tok = lambda w, n=0: pl.BlockSpec((tm, w), lambda i: (i, n))
    return _pcall(
        body, name="outproj", grid=(T // tm,),
        in_specs=[tok(FW), tok(sec_w, 3), tok(C), tok(D), tok(D), pl.BlockSpec((FW + C, D), lambda i: (0, 0))],
        out_specs=[tok(D), tok(FW + C), pl.BlockSpec((FW + C, D), lambda i: (0, 0)), pl.BlockSpec((8, LANES), lambda i: (0, 0))],
        out_shape=[jax.ShapeDtypeStruct((T, D), F32), jax.ShapeDtypeStruct((T, FW + C), F32),
                   jax.ShapeDtypeStruct((FW + C, D), F32), jax.ShapeDtypeStruct((8, LANES), F32)],
    )(a, z, uo, x, tgt, wo)


def _conv_post_bwd(dy, uc, z, lg, lb, *, tm, sec_w):
    T, C = uc.shape
    FW = dy.shape[1] - C
    off_c = 6 * sec_w // C

    def body(dy_ref, uc_ref, gc_ref, lg_ref, lb_ref, dgc_ref, duc_ref, small_ref):
        @pl.when(pl.program_id(0) == 0)
        def _():
            small_ref[...] = jnp.zeros_like(small_ref)

        lg = lg_ref[...]
        rs, un, ul = _ln_parts(uc_ref[...], lg, lb_ref[...])
        s_ul = _sig(ul)
        gc = gc_ref[...]
        s_gc = _sig(gc)
        duo = dy_ref[...]
        dgc_ref[...] = (duo * (ul * s_ul) * _dsilu(gc, s_gc)).astype(BF16)
        dul = duo * (gc * s_gc) * _dsilu(ul, s_ul)
        dun = dul * lg
        duc = rs * (dun - jnp.mean(dun, axis=-1, keepdims=True) - un * jnp.mean(dun * un, axis=-1, keepdims=True))
        duc_ref[...] = duc
        small_ref[0:1, :] += jnp.sum(dul * un, axis=0, keepdims=True)
        small_ref[8:9, :] += jnp.sum(dul, axis=0, keepdims=True)
        small_ref[16:17, :] += jnp.sum(duc, axis=0, keepdims=True)

    tok = lambda n: pl.BlockSpec((tm, C), lambda i: (i, n))
    row = pl.BlockSpec((1, C), lambda i: (0, 0))
    return _pcall(
        body, name="conv_post_bwd", grid=(T // tm,),
        in_specs=[tok(FW // C), tok(0), tok(off_c), row, row],
        out_specs=[tok(0), tok(0), pl.BlockSpec((24, C), lambda i: (0, 0))],
        out_shape=[jax.ShapeDtypeStruct((T, C), BF16), jax.ShapeDtypeStruct((T, C), F32), jax.ShapeDtypeStruct((24, C), F32)],
    )(dy, uc, z, lg, lb)


def _conv_bwd(duc, z, cw, *, B, S, sec_w, rc):
    C = cw.shape[1]
    nct = C // LANES
    off_a, off_b = 4 * sec_w // LANES, 5 * sec_w // LANES

    def body(duc_ref, ga_ref, gb_ref, w_ref, dga_ref, dgb_ref, dw_ref, ubuf, dbuf):
        @pl.when(pl.program_id(1) == 0)
        def _():
            dw_ref[...] = jnp.zeros_like(dw_ref)

        ubuf[0:HALO, :] = jnp.zeros((HALO, LANES), F32)
        ubuf[HALO:HALO + S, :] = ga_ref[...] * _sig(gb_ref[...])
        dbuf[0:S, :] = duc_ref[...]
        dbuf[S:S + HALO, :] = jnp.zeros((HALO, LANES), F32)

        def chunk(cidx, carry):
            r0 = pl.multiple_of(cidx * rc, rc)
            acc = jnp.zeros((rc, LANES), F32)
            for j in range(CONV_K):
                acc = acc + w_ref[j:j + 1, :] * dbuf[pl.ds(r0 + (CONV_K - 1 - j), rc), :]
            rows = pl.ds(r0, rc)
            ga, sg = ga_ref[rows, :], _sig(gb_ref[rows, :])
            dga_ref[rows, :] = (acc * sg).astype(BF16)
            dgb_ref[rows, :] = (acc * ga * sg * (1.0 - sg)).astype(BF16)
            return carry

        lax.fori_loop(0, S // rc, chunk, 0)
        d = duc_ref[...]
        for j in range(CONV_K):
            dw_ref[j:j + 1, :] += jnp.sum(d * ubuf[pl.ds(HALO - CONV_K + 1 + j, S), :], axis=0, keepdims=True)

    return _pcall(
        body, name="conv_bwd", grid=(nct, B),
        in_specs=[pl.BlockSpec((S, LANES), lambda c, b: (b, c)), pl.BlockSpec((S, LANES), lambda c, b: (b, off_a + c)),
                  pl.BlockSpec((S, LANES), lambda c, b: (b, off_b + c)), pl.BlockSpec((CONV_K, LANES), lambda c, b: (0, c))],
        out_specs=[pl.BlockSpec((S, LANES), lambda c, b: (b, c)), pl.BlockSpec((S, LANES), lambda c, b: (b, c)),
                   pl.BlockSpec((CONV_K, LANES), lambda c, b: (0, c))],
        out_shape=[jax.ShapeDtypeStruct((B * S, C), BF16), jax.ShapeDtypeStruct((B * S, C), BF16),
                   jax.ShapeDtypeStruct((CONV_K, C), F32)],
        scratch_shapes=[pltpu.VMEM((HALO + S, LANES), F32), pltpu.VMEM((S + HALO, LANES), F32)],
    )(duc, z, z, cw)


def _attn_bwd_prep(dy, a, z, *, B, S, H, ts, sec_w):
    FW = H * HEAD_DIM
    nts = S // ts

    def body(dy_ref, a_ref, gf_ref, dgf_ref, do_ref):
        lane = _lane_iota(ts)
        for p in range(H // 2):
            cols = slice(p * LANES, (p + 1) * LANES)
            gf, av, dya = gf_ref[:, cols], a_ref[:, cols], dy_ref[:, cols]
            sg = _sig(gf)
            dgf_ref[:, cols] = (dya * av * _dsilu(gf, sg)).astype(BF16)
            da = dya * (gf * sg)
            nd = -_half_sums(da * av, lane)
            for e in range(2):
                nd_e = nd[:, e * HEAD_DIM:e * HEAD_DIM + 1]
                do_ref[0, 2 * p + e] = _put3(_head_to_low(da, e, lane), lane, C0, _split3(nd_e)).astype(BF16)

    tok = lambda n: pl.BlockSpec((ts, FW), lambda b, i: (b * nts + i, n))
    return _pcall(
        body, name="attn_bwd_prep", grid=(B, nts),
        in_specs=[tok(0), tok(0), pl.BlockSpec((ts, sec_w), lambda b, i: (b * nts + i, 3))],
        out_specs=[tok(0), pl.BlockSpec((1, H, ts, LANES), lambda b, i: (b, 0, i, 0))],
        out_shape=[jax.ShapeDtypeStruct((B * S, FW), BF16), jax.ShapeDtypeStruct((B, H, S, LANES), BF16)],
    )(dy, a, z)


def _attn_bwd(qb, ka, va, doa, *, t, G, comm=None):
    B, H, S, _ = qb.shape
    n = S // t
    assert n % 2 == 0
    grid = (B, H // G, n)

    def body(*refs):
        (q_ref, k_ref, v_ref, do_ref), (dq_ref, dk_ref, dv_ref), _, rider = _split_refs(refs, 4, 3, comm)
        first, last = _grid_ends(grid)
        if comm:
            comm.begin(rider, first)
        j = pl.program_id(2)

        @pl.when(j == 0)
        def _():
            dq_ref[...] = jnp.zeros_like(dq_ref)

        row = lax.broadcasted_iota(jnp.int32, (t, t), 0)
        col = lax.broadcasted_iota(jnp.int32, (t, t), 1)
        ks = [k_ref[0, e] for e in range(G)]
        vs = [v_ref[0, e] for e in range(G)]

        def step(start, width, carry, masked):
            rows = pl.ds(pl.multiple_of(start, t), width)
            new = []
            for e in range(G):
                dk, dv = carry[e]
                q, do = q_ref[0, e, rows, :], do_ref[0, e, rows, :]
                st = _dot_nt(ks[e], q)
                dpt = _dot_nt(vs[e], do)
                if masked:
                    st = jnp.where(row <= col, st, NEG_INF)
                pt = jnp.exp(st)
                dsb = (pt * dpt).astype(BF16)
                dq_ref[0, e, rows, :] += _dot_tn(dsb, ks[e])
                new.append((dk + _dot(dsb, q), dv + _dot(pt.astype(BF16), do)))
            return tuple(new)

        zero = jnp.zeros((t, LANES), F32)
        carry = step(j * t, t, tuple((zero, zero) for _ in range(G)), True)
        first_pair = (j + 2) // 2
        carry = lax.fori_loop(j + 1, 2 * first_pair, lambda i, c: step(i * t, t, c, False), carry)
        carry = lax.fori_loop(first_pair, n // 2, lambda i, c: step(i * (2 * t), 2 * t, c, False), carry)
        for e in range(G):
            dk_ref[0, e], dv_ref[0, e] = carry[e]
        if comm:
            comm.end(rider, last)

    full = pl.BlockSpec((1, G, S, LANES), lambda b, h, j: (b, h, 0, 0))
    tile = pl.BlockSpec((1, G, t, LANES), lambda b, h, j: (b, h, j, 0))
    shape = jax.ShapeDtypeStruct((B, H, S, LANES), F32)
    return _pcall(
        body, name="attn_bwd", grid=grid,
        **_with_comm(comm, [full, tile, tile, full], [full, tile, tile], [shape, shape, shape], []),
    )(qb, ka, va, doa, *(comm.operands() if comm else ()))


def _attn_post_bwd(dq, dk, dv, z, zf, bfp, gq, gk, *, B, S, H, ts):
    FW = H * HEAD_DIM
    nts = S // ts
    scale = HEAD_DIM ** -0.5

    def body(dq_ref, dk_ref, dv_ref, zq_ref, zk_ref, zf_ref, bf_ref, gq_ref, gk_ref,
             dzq_ref, dzk_ref, dzv_ref, dzf_ref, small_ref, carry):
        first = (pl.program_id(0) == 0) & (pl.program_id(1) == 0)

        @pl.when(first)
        def _():
            small_ref[...] = jnp.zeros_like(small_ref)

        @pl.when(pl.program_id(1) == 0)
        def _():
            carry[...] = jnp.zeros_like(carry)

        lane = _lane_iota(ts)
        dcs = jnp.zeros((ts, LANES), F32)
        for p in range(H // 2):
            cols = slice(p * LANES, (p + 1) * LANES)
            for d_ref, z_ref, g_ref, o_ref, srow, mul in ((dq_ref, zq_ref, gq_ref, dzq_ref, 0, scale),
                                                          (dk_ref, zk_ref, gk_ref, dzk_ref, 1, 1.0)):
                dy = jnp.where(lane < HEAD_DIM, d_ref[0, 2 * p], pltpu.roll(d_ref[0, 2 * p + 1], HEAD_DIM, 1)) * mul
                xv, g = z_ref[:, cols], g_ref[:, cols]
                r = lax.rsqrt(_half_sums(xv * xv, lane) * (1.0 / HEAD_DIM) + EPS)
                dxh = dy * g
                mm = _half_sums(dxh * xv, lane) * (1.0 / HEAD_DIM)
                o_ref[:, cols] = (r * (dxh - xv * (r * r) * mm)).astype(BF16)
                small_ref[8 * srow:8 * srow + 1, cols] += jnp.sum(dy * xv * r, axis=0, keepdims=True)
            dzv_ref[:, cols] = jnp.where(lane < HEAD_DIM, dv_ref[0, 2 * p], pltpu.roll(dv_ref[0, 2 * p + 1], HEAD_DIM, 1)).astype(BF16)
            for e in range(2):
                h = 2 * p + e
                dc = jnp.sum(jnp.where(lane == C0, dq_ref[0, h], 0.0) - jnp.where(lane == K0, dk_ref[0, h], 0.0),
                             axis=-1, keepdims=True)
                dcs = jnp.where(lane == h, dc, dcs)
        ri = lax.broadcasted_iota(jnp.int32, (ts, ts), 0)
        ci = lax.broadcasted_iota(jnp.int32, (ts, ts), 1)
        tri = (ci >= ri).astype(BF16)
        dl = carry[...]
        for piece in _split3(dcs):
            dl = dl + _dot(tri, piece)
        carry[...] = dl[0:1, :]
        xf = zf_ref[...] + bf_ref[...]
        df = jnp.where(lane < H, dl * _sig(-xf), 0.0)
        dzf_ref[...] = df.astype(BF16)
        small_ref[16:17, 0:LANES] += jnp.sum(df, axis=0, keepdims=True)

    rev = lambda b, i: (b, 0, nts - 1 - i, 0)
    head = pl.BlockSpec((1, H, ts, LANES), rev)
    tokrow = lambda b, i: b * nts + nts - 1 - i
    sec = lambda n: pl.BlockSpec((ts, FW), lambda b, i: (tokrow(b, i), n))
    narrow = pl.BlockSpec((ts, LANES), lambda b, i: (tokrow(b, i), 0))
    row = lambda w: pl.BlockSpec((1, w), lambda b, i: (0, 0))
    return _pcall(
        body, name="attn_post_bwd", grid=(B, nts),
        in_specs=[head, head, head, sec(0), sec(1), narrow, row(LANES), row(FW), row(FW)],
        out_specs=[sec(0), sec(0), sec(0), narrow, pl.BlockSpec((24, FW), lambda b, i: (0, 0))],
        out_shape=[jax.ShapeDtypeStruct((B * S, FW), BF16)] * 3 + [jax.ShapeDtypeStruct((B * S, LANES), BF16),
                                                                   jax.ShapeDtypeStruct((24, FW), F32)],
        scratch_shapes=[pltpu.VMEM((1, LANES), F32)],
    )(dq, dk, dv, z, z, zf, bfp, gq, gk)


def _inproj_bwd_x(dzs, dzf, wmt, wft, x, g, dout, *, tm, comm=None):
    T, D = x.shape
    nj = len(dzs)
    tn = dzs[0].shape[1]
    grid = (T // tm, nj)

    def body(*refs):
        ins, (gx_ref, dg_ref), (acc,), rider = _split_refs(refs, nj + 6, 2, comm)
        dz_refs = ins[:nj]
        dzf_ref, wm_ref, wf_ref, x_ref, g_ref, dout_ref = ins[nj:]
        first, last = _grid_ends(grid)
        if comm:
            comm.begin(rider, first)
        i, j = pl.program_id(0), pl.program_id(1)

        @pl.when((i == 0) & (j == 0))
        def _():
            dg_ref[...] = jnp.zeros_like(dg_ref)

        @pl.when(j == 0)
        def _():
            acc[...] = _dot(dzf_ref[...], wf_ref[...])

        for s in range(nj):
            @pl.when(j == s)
            def _():
                acc[...] += _dot(dz_refs[s][...], wm_ref[...])

        @pl.when(j == nj - 1)
        def _():
            dh, xv, g = acc[...], x_ref[...], g_ref[...]
            r = lax.rsqrt(jnp.mean(xv * xv, axis=-1, keepdims=True) + EPS)
            dxh = dh * g
            mm = jnp.mean(dxh * xv, axis=-1, keepdims=True)
            gx_ref[...] = dout_ref[...] + r * (dxh - xv * (r * r) * mm)
            dg_ref[0:1, :] += jnp.sum(dh * xv * r, axis=0, keepdims=True)

        if comm:
            comm.end(rider, last)

    tok = pl.BlockSpec((tm, D), lambda i, j: (i, 0))
    sec = pl.BlockSpec((tm, tn), lambda i, j: (i, 0))
    return _pcall(
        body, name="inproj_bwd_x", grid=grid,
        **_with_comm(
            comm,
            [sec] * nj + [pl.BlockSpec((tm, LANES), lambda i, j: (i, 0)), pl.BlockSpec((tn, D), lambda i, j: (j, 0)),
                          pl.BlockSpec((LANES, D), lambda i, j: (0, 0)), tok, pl.BlockSpec((1, D), lambda i, j: (0, 0)), tok],
            [tok, pl.BlockSpec((8, D), lambda i, j: (0, 0))],
            [jax.ShapeDtypeStruct((T, D), F32), jax.ShapeDtypeStruct((8, D), F32)],
            [pltpu.VMEM((tm, D), F32)]),
    )(*dzs, dzf, wmt, wft, x, g, dout, *(comm.operands() if comm else ()))


def _inproj_bwd_w(h, dz, *, tm, name):
    T, D = h.shape
    W = dz.shape[1]
    ni = T // tm

    def body(h_ref, dz_ref, dw_ref, acc):
        i = pl.program_id(0)

        @pl.when(i == 0)
        def _():
            acc[...] = jnp.zeros_like(acc)

        acc[...] += _dot_tn(dz_ref[...], h_ref[...])

        @pl.when(i == ni - 1)
        def _():
            dw_ref[...] = acc[...].astype(BF16)

    return _pcall(
        body, name=name, grid=(ni,),
        in_specs=[pl.BlockSpec((tm, D), lambda i: (i, 0)), pl.BlockSpec((tm, W), lambda i: (i, 0))],
        out_specs=pl.BlockSpec((W, D), lambda i: (0, 0)), out_shape=jax.ShapeDtypeStruct((W, D), BF16),
        scratch_shapes=[pltpu.VMEM((W, D), F32)],
    )(h, dz)


def _adamw(w, g, m, v):
    m = ADAM_B1 * m + (1.0 - ADAM_B1) * g
    v = ADAM_B2 * v + (1.0 - ADAM_B2) * (g * g)
    m_hat = m / (1.0 - ADAM_B1 ** ADAM_STEP)
    v_hat = v / (1.0 - ADAM_B2 ** ADAM_STEP)
    return -ADAM_LR * (m_hat / (jnp.sqrt(v_hat) + ADAM_EPS) + ADAM_WD * w), m, v


def _adam_update(w, m, v, parts, *, name, tr):
    R, Cc = w.shape[0], w.shape[-1]

    def body(w_ref, m_ref, v_ref, p_ref, g_ref, d_ref, nm_ref, nv_ref):
        g = p_ref[0].astype(F32)
        for k in range(1, N_DEV):
            g = g + p_ref[k].astype(F32)
        g_ref[...] = g
        d_ref[...], nm_ref[...], nv_ref[...] = _adamw(w_ref[...], g, m_ref[...], v_ref[...])

    blk = pl.BlockSpec((tr, Cc), lambda i: (i, 0)) if w.ndim == 2 else pl.BlockSpec((tr, None, Cc), lambda i: (i, 0, 0))
    shape = jax.ShapeDtypeStruct(w.shape, F32)
    return _pcall(
        body, name=name, grid=(pl.cdiv(R, tr),),
        in_specs=[blk, blk, blk, pl.BlockSpec((N_DEV, tr, Cc), lambda i: (0, i, 0))],
        out_specs=[blk, blk, blk, blk], out_shape=[shape, shape, shape, shape],
    )(w, m, v, parts)


MESH = pl.DeviceIdType.MESH
ANY = pl.BlockSpec(memory_space=pl.ANY)


def _flip(v, bit):
    return 1 - v if bit else v


def _allgather(shards):
    n = len(shards)

    def body(*refs):
        srcs, outs = refs[:n], refs[n:2 * n]
        send_sems, recv_sems, local_sems = refs[2 * n:]
        x, y, c = lax.axis_index("x"), lax.axis_index("y"), lax.axis_index("c")
        sibling = (x, y, 1 - c)
        chips = [(1 - x, y), (x, 1 - y), (1 - x, 1 - y)]
        slot = lambda px, py, pc: 4 * px + 2 * py + pc

        def copy(a, k, block, to, src=None):
            dst = outs[a].at[slot(*block)]
            return pltpu.make_async_remote_copy(src_ref=dst if src is None else src, dst_ref=dst, send_sem=send_sems.at[7 * a + k],
                                                recv_sem=recv_sems.at[7 * a + k], device_id=to, device_id_type=MESH)

        mine = [pltpu.make_async_copy(srcs[a], outs[a].at[slot(x, y, c)], local_sems.at[a]) for a in range(n)]
        sends = []
        for a in range(n):
            mine[a].start()
            sends.append(copy(a, 0, (x, y, c), sibling, src=srcs[a]))
            sends += [copy(a, 1 + j, (x, y, c), (*chip, c), src=srcs[a]) for j, chip in enumerate(chips)]
        for cp in sends:
            cp.start()
        for a in range(n):
            for j, chip in enumerate(chips):
                copy(a, 1 + j, (*chip, c), (x, y, c)).wait_recv()
                fwd = copy(a, 4 + j, (*chip, c), sibling)
                fwd.start()
                sends.append(fwd)
        for a in range(n):
            copy(a, 0, (x, y, 1 - c), (x, y, c)).wait_recv()
            for j, chip in enumerate(chips):
                copy(a, 4 + j, (*chip, 1 - c), (x, y, c)).wait_recv()
        for cp in sends:
            cp.wait_send()
        for cp in mine:
            cp.wait()

    return pl.pallas_call(
        body, name="allgather_weights", in_specs=[ANY] * n, out_specs=[ANY] * n,
        out_shape=[jax.ShapeDtypeStruct((N_DEV,) + s.shape, s.dtype) for s in shards],
        scratch_shapes=[pltpu.SemaphoreType.DMA((7 * n,)), pltpu.SemaphoreType.DMA((7 * n,)), pltpu.SemaphoreType.DMA((n,))],
    )(*shards)


class _Comm:
    def __init__(self, arrays, owners, into=None):
        self.arrays, self.owners, self.n = list(arrays), list(owners), len(arrays)
        self.into = list(into) if into else [None] * self.n
        self.carried = [t for t in self.into if t is not None]

    def operands(self):
        return self.arrays + self.carried

    def aliases(self, n_in, n_out):
        out, pos = {}, 0
        for a, t in enumerate(self.into):
            if t is not None:
                out[n_in + self.n + pos] = n_out + a
                pos += 1
        return out

    def out_shape(self):
        return [jax.ShapeDtypeStruct((N_DEV,) + (a.shape if o is None else a.shape[1:]), a.dtype)
                for a, o in zip(self.arrays, self.owners)]

    def scratch(self):
        return [pltpu.SemaphoreType.DMA((7 * self.n,)), pltpu.SemaphoreType.DMA((7 * self.n,)), pltpu.SemaphoreType.DMA((self.n,))]

    def ops(self, srcs, outs, send_sems, recv_sems, local_sems):
        x, y, c = lax.axis_index("x"), lax.axis_index("y"), lax.axis_index("c")
        me = 4 * x + 2 * y + c
        when = lambda p, f: (lambda: pl.when(p)(f))
        starts, waits = [], []
        for a, owners in enumerate(self.owners):
            src, out = srcs[a], outs[a]

            def remote(piece, slot, rel, to):
                sem = 7 * a + rel - 1
                return pltpu.make_async_remote_copy(src_ref=piece, dst_ref=out.at[slot], send_sem=send_sems.at[sem],
                                                    recv_sem=recv_sems.at[sem], device_id=to, device_id_type=MESH)

            peers = [(d, (_flip(x, d & 4), _flip(y, d & 2), _flip(c, d & 1))) for d in range(1, N_DEV)]
            if owners is None:
                local = pltpu.make_async_copy(src, out.at[me], local_sems.at[a])
                starts.append(local.start)
                waits.append(local.wait)
                for d, peer in peers:
                    cp = remote(src, me, d, peer)
                    starts.append(cp.start)
                    waits.append(cp.wait_send)
                receives = True
            else:
                for s, dest in enumerate(owners):
                    to = (dest >> 2 & 1, dest >> 1 & 1, dest & 1)
                    rel = 4 * _flip(x, to[0]) + 2 * _flip(y, to[1]) + _flip(c, to[2])
                    cp = remote(src.at[s], me, jnp.maximum(rel, 1), to)
                    local = pltpu.make_async_copy(src.at[s], out.at[me], local_sems.at[a])
                    starts += [when(rel != 0, cp.start), when(rel == 0, local.start)]
                    waits += [when(rel != 0, cp.wait_send), when(rel == 0, local.wait)]
                receives = me == owners[0]
                for dest in owners[1:]:
                    receives = receives | (me == dest)
            piece = src if owners is None else src.at[0]
            for d, peer in peers:
                back = remote(piece, 4 * peer[0] + 2 * peer[1] + peer[2], d, peer)
                waits.append(back.wait_recv if receives is True else when(receives, back.wait_recv))
        return starts, waits

    def _phase(self, which, rider, cond):
        srcs, outs, sems = rider

        def go():
            for op in self.ops(srcs, outs, *sems)[which]:
                op()

        go() if cond is True else pl.when(cond)(go)

    def begin(self, rider, first=True):
        self._phase(0, rider, first)

    def end(self, rider, last=True):
        self._phase(1, rider, last)


def _split_refs(refs, n_in, n_out, comm):
    k = comm.n if comm else 0
    o0 = n_in + k + (len(comm.carried) if comm else 0)
    ins, c_src = refs[:n_in], refs[n_in:n_in + k]
    outs, c_out = refs[o0:o0 + n_out], refs[o0 + n_out:o0 + n_out + k]
    rest = refs[o0 + n_out + k:]
    scratch, c_sem = (rest[:-3], rest[-3:]) if comm else (rest, ())
    return ins, outs, scratch, (c_src, c_out, c_sem)


def _grid_ends(grid):
    first = functools.reduce(jnp.logical_and, [pl.program_id(a) == 0 for a in range(len(grid))])
    last = functools.reduce(jnp.logical_and, [pl.program_id(a) == g - 1 for a, g in enumerate(grid)])
    return first, last


def _with_comm(comm, in_specs, out_specs, out_shape, scratch):
    if comm is None:
        return dict(in_specs=in_specs, out_specs=out_specs, out_shape=out_shape, scratch_shapes=scratch)
    return dict(in_specs=in_specs + [ANY] * len(comm.operands()), out_specs=out_specs + [ANY] * comm.n,
                out_shape=out_shape + comm.out_shape(), scratch_shapes=scratch + comm.scratch(),
                input_output_aliases=comm.aliases(len(in_specs), len(out_specs)))


def _exchange(comm, *, name):
    n = comm.n

    def body(*refs):
        rider = (refs[:n], refs[n:2 * n], refs[2 * n:])
        comm.begin(rider)
        comm.end(rider)

    return pl.pallas_call(body, name=name, in_specs=[ANY] * n, out_specs=[ANY] * n, out_shape=comm.out_shape(),
                          scratch_shapes=comm.scratch())(*comm.arrays)


def kernel(x, norm_g, w_in, b_forget, q_norm_g, k_norm_g, conv_w, conv_b, conv_ln_g, conv_ln_b, w_out, loss_target, m_norm_g, m_w_in, m_b_forget, m_q_norm_g, m_k_norm_g, m_conv_w, m_conv_b, m_conv_ln_g, m_conv_ln_b, m_w_out, v_norm_g, v_w_in, v_b_forget, v_q_norm_g, v_k_norm_g, v_conv_w, v_conv_b, v_conv_ln_g, v_conv_ln_b, v_w_out):
    B, S, D = x.shape
    H = q_norm_g.shape[1]
    FW = H * HEAD_DIM
    assert q_norm_g.shape[2] == HEAD_DIM and FW == D and conv_b.shape[1] == D and H <= LANES // 3
    ncol = w_in.shape[2]
    T = B * S

    to_t = lambda t: jnp.transpose(t, (2, 0, 1))
    from_t = lambda t: jnp.transpose(t, (1, 2, 0))
    wt, mt, vt = to_t(w_in), to_t(m_w_in), to_t(v_w_in)

    x2, tgt = x.reshape(T, D), loss_target.reshape(T, D)
    tm, ts, rc = min(512, T), min(256, S), min(64, S)
    everyone = list(range(N_DEV))
    bfp = jnp.pad(b_forget, ((0, 0), (0, LANES - H)))
    gq, gk = q_norm_g.reshape(1, FW), k_norm_g.reshape(1, FW)

    (wg,) = _allgather([wt.reshape(ncol, D).astype(BF16)])
    wmt, wft = _w_relayout(wg, D=D, H=H)
    z, zf, h, wog, cwg = _inproj_fwd(x2, norm_g, wmt, wft, tm=tm, tn=FW,
                                     comm=_Comm([w_out[0].astype(BF16), conv_w[0]], [None, None]))
    wo = wog.reshape(N_DEV * wog.shape[1], D)
    cw = jnp.transpose(cwg, (1, 0, 2)).reshape(CONV_K, D)
    qa, ka, va = _attn_prep(z, zf, bfp, gq, gk, B=B, S=S, H=H, ts=ts)
    a, qb = _attn_fwd(qa, ka, va, t=ts, G=4)
    uc = _conv_fwd(z, cw, conv_b, B=B, S=S, sec_w=FW, rc=rc)
    uo = _conv_post(uc, z, conv_ln_g, conv_ln_b, tm=ts, sec_w=FW)
    dout, dy, dwo, loss = _outproj(a, z, uo, x2, tgt, wo, tm=ts, sec_w=FW)

    dgc, duc, small_c = _conv_post_bwd(dy, uc, z, conv_ln_g, conv_ln_b, tm=ts, sec_w=FW)
    dga, dgb, dcw = _conv_bwd(duc, z, cw, B=B, S=S, sec_w=FW, rc=rc)
    dgf, doa = _attn_bwd_prep(dy, a, z, B=B, S=S, H=H, ts=ts, sec_w=FW)
    _, f1, _ = _section_rows(D, H)
    late_rows = [k for k in everyone if k * ncol >= f1]
    early_rows = [k for k in everyone if k not in late_rows]
    first_late_sec = (late_rows[0] * ncol - H) // FW
    dws = {s: _inproj_bwd_w(h, dz, tm=tm, name=f"inproj_bwd_w{s}") for s, dz in ((3, dgf), (4, dga), (5, dgb), (6, dgc))}
    dw_late = _dw_relayout({s: dws[s] for s in range(first_late_sec, 7)}, None, R=ncol, H=H, shards=late_rows, name="dw_relayout_late")
    dwo_parts = dwo.reshape(N_DEV, dwo.shape[0] // N_DEV, D).astype(BF16)
    dcw_parts = jnp.transpose(dcw.reshape(CONV_K, N_DEV, D // N_DEV), (1, 0, 2))
    dq, dk, dv, r_w, r_wo, r_cw = _attn_bwd(qb, ka, va, doa, t=ts, G=4,
                                            comm=_Comm([dw_late, dwo_parts, dcw_parts], [late_rows, everyone, everyone]))
    dzq, dzk, dzv, dzf, small_a = _attn_post_bwd(dq, dk, dv, z, zf, bfp, gq, gk, B=B, S=S, H=H, ts=ts)
    dws.update({s: _inproj_bwd_w(h, dz, tm=tm, name=f"inproj_bwd_w{s}") for s, dz in ((0, dzq), (1, dzk), (2, dzv))})
    dwf = _inproj_bwd_w(h, dzf, tm=tm, name="inproj_bwd_wf")
    last_early_sec = (early_rows[-1] * ncol + ncol - 1 - H) // FW
    dw_early = _dw_relayout({s: dws[s] for s in range(last_early_sec + 1)}, dwf, R=ncol, H=H, shards=early_rows, name="dw_relayout_early")
    gx, dg, r_w = _inproj_bwd_x([dzq, dzk, dzv, dgf, dga, dgb, dgc], dzf, wmt, wft, x2, norm_g, dout, tm=tm,
                                comm=_Comm([dw_early], [early_rows], into=[r_w]))
    small = jnp.concatenate([dg, small_a[16:24], small_a[0:8], small_a[8:16], small_c[16:24], small_c[0:8], small_c[8:16],
                             jnp.pad(loss, ((0, 0), (0, D - LANES)))], axis=0)
    (r_small,) = _exchange(_Comm([small], [None]), name="exchange_small")

    g_w, d_w, nm_w, nv_w = [from_t(t) for t in _adam_update(wt, mt, vt, r_w, name="adam_w_in", tr=min(128, D))]
    g_wo, d_wo, nm_wo, nv_wo = [t[None] for t in _adam_update(w_out[0], m_w_out[0], v_w_out[0], r_wo, name="adam_w_out",
                                                               tr=min(128, w_out.shape[1]))]
    g_cw, d_cw, nm_cw, nv_cw = [t[None] for t in _adam_update(conv_w[0], m_conv_w[0], v_conv_w[0], r_cw, name="adam_conv_w", tr=CONV_K)]
    tile = lambda t: jnp.pad(t.reshape(1, -1), ((0, 7), (0, D - t.size)))
    pack = lambda *ps: jnp.concatenate([tile(p) for p in ps] + [jnp.zeros((8, D), F32)], axis=0)
    small_w = pack(norm_g, b_forget, q_norm_g, k_norm_g, conv_b, conv_ln_g, conv_ln_b)
    small_m = pack(m_norm_g, m_b_forget, m_q_norm_g, m_k_norm_g, m_conv_b, m_conv_ln_g, m_conv_ln_b)
    small_v = pack(v_norm_g, v_b_forget, v_q_norm_g, v_k_norm_g, v_conv_b, v_conv_ln_g, v_conv_ln_b)
    g_s, d_s, nm_s, nv_s = _adam_update(small_w, small_m, small_v, r_small, name="adam_small", tr=small_w.shape[0])

    def leaves(small_t, w_t, cw_t, wo_t):
        row = lambda r, like: small_t[8 * r:8 * r + 1, :like.size].reshape(like.shape)
        return [row(0, norm_g), w_t, row(1, b_forget), row(2, q_norm_g), row(3, k_norm_g), cw_t, row(4, conv_b),
                row(5, conv_ln_g), row(6, conv_ln_b), wo_t]

    return (g_s[56, 0], gx.reshape(B, S, D), *leaves(g_s, g_w, g_cw, g_wo), *leaves(d_s, d_w, d_cw, d_wo),
            *leaves(nm_s, nm_w, nm_cw, nm_wo), *leaves(nv_s, nv_w, nv_cw, nv_wo))
```

```python
import functools

import jax
import jax.numpy as jnp
from jax import lax
from jax.experimental import pallas as pl
from jax.experimental.pallas import tpu as pltpu

F32, BF16 = jnp.float32, jnp.bfloat16
EPS = 1e-6
NEG_INF = -1e30
CONV_K = 31
HEAD_DIM = 64
ADAM_LR, ADAM_B1, ADAM_B2, ADAM_EPS, ADAM_WD, ADAM_STEP = 0.001, 0.9, 0.999, 1e-08, 0.01, 10

LANES = 128
HALO = 32
N_DEV = 8
VMEM_LIMIT = 56 * 1024 * 1024

C0, K0, L0 = HEAD_DIM, HEAD_DIM + 3, HEAD_DIM + 6


def _pcall(body, *, name, vmem=VMEM_LIMIT, **kw):
    return pl.pallas_call(body, name=name, compiler_params=pltpu.CompilerParams(vmem_limit_bytes=vmem), **kw)


def _dot(a, b):
    return jnp.dot(a, b, preferred_element_type=F32)


def _dot_nt(a, b):
    return lax.dot_general(a, b, (((1,), (1,)), ((), ())), preferred_element_type=F32)


def _dot_tn(a, b):
    return lax.dot_general(a, b, (((0,), (0,)), ((), ())), preferred_element_type=F32)


def _split3(x):
    a = x.astype(BF16)
    r = x - a.astype(F32)
    b = r.astype(BF16)
    c = (r - b.astype(F32)).astype(BF16)
    return a, b, c


def _sig(x):
    return jax.nn.sigmoid(x)


def _dsilu(x, s):
    return s * (1.0 + x * (1.0 - s))


def _lane_iota(rows):
    return lax.broadcasted_iota(jnp.int32, (rows, LANES), 1)


def _half_sums(v, lane):
    lo = jnp.sum(jnp.where(lane < HEAD_DIM, v, 0.0), axis=-1, keepdims=True)
    hi = jnp.sum(jnp.where(lane >= HEAD_DIM, v, 0.0), axis=-1, keepdims=True)
    return jnp.where(lane < HEAD_DIM, lo, hi)


def _head_to_low(v, e, lane):
    if e:
        v = pltpu.roll(v, HEAD_DIM, 1)
    return jnp.where(lane < HEAD_DIM, v, 0.0)


def _put3(base, lane, first, pieces):
    for n, p in enumerate(pieces):
        base = jnp.where(lane == first + n, p.astype(F32), base)
    return base


def _section_rows(D, H):
    FW = H * HEAD_DIM
    return 3 * FW, 3 * FW + H, 7 * FW + H


def _row_pieces(a, b, R, step):
    out = []
    while a < b:
        k = a // R
        lo = a - k * R
        hi = min(R, lo + min(step, b - a))
        out.append((k, lo, hi))
        a += hi - lo
    return out


def _w_relayout(wg, *, D, H):
    R = wg.shape[1]
    FW = H * HEAD_DIM
    f0, f1, end = _section_rows(D, H)

    def body(wg_ref, wm_ref, wf_ref):
        def put(dst_ref, d0, a, b):
            for k, lo, hi in _row_pieces(a, b, R, 256):
                dst_ref[d0:d0 + hi - lo, :] = wg_ref[k, lo:hi, :]
                d0 += hi - lo

        put(wm_ref, 0, 0, f0)
        put(wm_ref, f0, f1, end)
        wf_ref[...] = jnp.zeros_like(wf_ref)
        put(wf_ref, 0, f0, f1)

    return _pcall(body, name="w_relayout",
                  out_shape=[jax.ShapeDtypeStruct((7 * FW, D), BF16), jax.ShapeDtypeStruct((LANES, D), BF16)])(wg)


def _dw_relayout(secs, dwf, *, R, H, shards, name):
    FW, D = next(iter(secs.values())).shape
    f0, f1, end = _section_rows(D, H)
    order = sorted(secs)

    def body(*refs):
        sec_refs = dict(zip(order, refs))
        dwf_ref, out_ref = (refs[-2] if dwf is not None else None), refs[-1]

        def src(g0, g1):
            if f0 <= g0 < f1:
                return dwf_ref[g0 - f0:g1 - f0, :]
            s0 = g0 if g0 < f0 else g0 - H
            return sec_refs[s0 // FW][s0 % FW:s0 % FW + (g1 - g0), :]

        cuts = sorted({0, end, f0, f1} | {s * FW for s in range(4)} | {f1 + s * FW for s in range(5)})
        for slot, k in enumerate(shards):
            g = k * R
            while g < (k + 1) * R:
                nxt = min(min(c for c in cuts if c > g), (k + 1) * R, g + 256)
                out_ref[slot, g - k * R:nxt - k * R, :] = src(g, nxt)
                g = nxt

    args = [secs[s] for s in order] + ([dwf] if dwf is not None else [])
    return _pcall(body, name=name, out_shape=jax.ShapeDtypeStruct((len(shards), R, D), BF16))(*args)


def _inproj_fwd(x, g, wmt, wft, *, tm, tn, comm=None):
    T, D = x.shape
    nj = wmt.shape[0] // tn
    grid = (T // tm, nj)

    def body(*refs):
        (x_ref, g_ref, wm_ref, wf_ref), (z_ref, zf_ref, h_ref), (h_scr,), rider = _split_refs(refs, 4, 3, comm)
        first, last = _grid_ends(grid)
        if comm:
            comm.begin(rider, first)

        @pl.when(pl.program_id(1) == 0)
        def _():
            xv = x_ref[...]
            r = lax.rsqrt(jnp.mean(xv * xv, axis=-1, keepdims=True) + EPS)
            h = ((xv * r) * g_ref[...]).astype(BF16)
            h_scr[...] = h
            h_ref[...] = h
            zf_ref[...] = _dot_nt(h, wf_ref[...])

        z_ref[...] = _dot_nt(h_scr[...], wm_ref[...])
        if comm:
            comm.end(rider, last)

    return _pcall(
        body, name="inproj_fwd", grid=grid,
        **_with_comm(
            comm,
            [pl.BlockSpec((tm, D), lambda i, j: (i, 0)), pl.BlockSpec((1, D), lambda i, j: (0, 0)),
             pl.BlockSpec((tn, D), lambda i, j: (j, 0)), pl.BlockSpec((LANES, D), lambda i, j: (0, 0))],
            [pl.BlockSpec((tm, tn), lambda i, j: (i, j)), pl.BlockSpec((tm, LANES), lambda i, j: (i, 0)),
             pl.BlockSpec((tm, D), lambda i, j: (i, 0))],
            [jax.ShapeDtypeStruct((T, wmt.shape[0]), F32), jax.ShapeDtypeStruct((T, LANES), F32), jax.ShapeDtypeStruct((T, D), BF16)],
            [pltpu.VMEM((tm, D), BF16)]),
    )(x, g, wmt, wft, *(comm.operands() if comm else ()))


def _attn_prep(z, zf, bfp, gq, gk, *, B, S, H, ts):
    FW = H * HEAD_DIM
    nts = S // ts
    scale = HEAD_DIM ** -0.5

    def body(zq_ref, zk_ref, zv_ref, zf_ref, bf_ref, gq_ref, gk_ref, qa_ref, ka_ref, va_ref, carry):
        @pl.when(pl.program_id(1) == 0)
        def _():
            carry[...] = jnp.zeros_like(carry)

        lane = _lane_iota(ts)
        xf = zf_ref[...] + bf_ref[...]
        logf = jnp.minimum(xf, 0.0) - jnp.log1p(jnp.exp(-jnp.abs(xf)))
        logf = jnp.where(lane < H, logf, 0.0)
        ri = lax.broadcasted_iota(jnp.int32, (ts, ts), 0)
        ci = lax.broadcasted_iota(jnp.int32, (ts, ts), 1)
        tri = (ri >= ci).astype(BF16)
        c = carry[...]
        for piece in _split3(logf):
            c = c + _dot(tri, piece)
        carry[...] = c[ts - 1:ts, :]
        c1, c2, c3 = _split3(c)
        cpk = c1.astype(F32) + pltpu.roll(c2.astype(F32), H, 1) + pltpu.roll(c3.astype(F32), 2 * H, 1)
        cq, ck = cpk.astype(BF16), (-cpk).astype(BF16)
        prow = lax.broadcasted_iota(jnp.int32, (LANES, LANES), 0)
        pcol = lax.broadcasted_iota(jnp.int32, (LANES, LANES), 1)
        ones_q = jnp.where((lane >= K0) & (lane < K0 + 3), 1.0, 0.0)
        ones_k = jnp.where(((lane >= C0) & (lane < C0 + 3)) | ((lane >= L0) & (lane < L0 + 3)), 1.0, 0.0)
        ones_v = jnp.where((lane >= C0) & (lane < C0 + 3), 1.0, 0.0)
        for h in range(H):
            p, e = divmod(h, 2)
            cols = slice(p * LANES, (p + 1) * LANES)
            piece = jnp.where(prow == h, 0, jnp.where(prow == H + h, 1, jnp.where(prow == 2 * H + h, 2, -LANES)))
            aug_q = _dot(cq, (pcol == C0 + piece).astype(BF16)) + ones_q
            aug_k = _dot(ck, (pcol == K0 + piece).astype(BF16)) + ones_k
            for z_ref, g_ref, o_ref, aug, mul in ((zq_ref, gq_ref, qa_ref, aug_q, scale), (zk_ref, gk_ref, ka_ref, aug_k, 1.0)):
                blk = z_ref[:, cols]
                r = lax.rsqrt(_half_sums(blk * blk, lane) * (1.0 / HEAD_DIM) + EPS)
                y = ((blk * r) * g_ref[:, cols]).astype(BF16).astype(F32) * mul
                o_ref[0, h] = (_head_to_low(y, e, lane) + aug).astype(BF16)
            va_ref[0, h] = (_head_to_low(zv_ref[:, cols], e, lane) + ones_v).astype(BF16)

    sec = lambda n: pl.BlockSpec((ts, FW), lambda b, i: (b * nts + i, n))
    row = lambda w: pl.BlockSpec((1, w), lambda b, i: (0, 0))
    head = pl.BlockSpec((1, H, ts, LANES), lambda b, i: (b, 0, i, 0))
    hshape = jax.ShapeDtypeStruct((B, H, S, LANES), BF16)
    return _pcall(
        body, name="attn_prep", grid=(B, nts),
        in_specs=[sec(0), sec(1), sec(2), pl.BlockSpec((ts, LANES), lambda b, i: (b * nts + i, 0)), row(LANES), row(FW), row(FW)],
        out_specs=[head, head, head], out_shape=[hshape, hshape, hshape],
        scratch_shapes=[pltpu.VMEM((1, LANES), F32)],
    )(z, z, z, zf, bfp, gq, gk)


def _attn_fwd(qa, ka, va, *, t, blk, G):
    B, H, S, _ = qa.shape
    n = S // t
    assert t % blk == 0

    def body(q_ref, k_ref, v_ref, a_ref, qb_ref):
        i = pl.program_id(2)
        lane = _lane_iota(t)
        row = lax.broadcasted_iota(jnp.int32, (t, blk), 0)
        col = lax.broadcasted_iota(jnp.int32, (t, blk), 1)
        qs = [q_ref[0, e] for e in range(G)]

        def step(start, carry, shift=None):
            rows = pl.ds(pl.multiple_of(start, blk), blk)
            new = []
            for e in range(G):
                m, acc = carry[e]
                s = _dot_nt(qs[e], k_ref[0, e, rows, :])
                if shift is not None:
                    s = jnp.where(col + shift <= row, s, NEG_INF)
                m_new = jnp.maximum(m, jnp.max(s, axis=-1, keepdims=True))
                p = jnp.exp(s - m_new)
                new.append((m_new, jnp.exp(m - m_new) * acc + _dot(p.astype(BF16), v_ref[0, e, rows, :])))
            return tuple(new)

        carry = tuple((jnp.full((t, 1), NEG_INF, F32), jnp.zeros((t, LANES), F32)) for _ in range(G))
        carry = lax.fori_loop(0, i * (t // blk), lambda j, c: step(j * blk, c), carry)
        for d in range(t // blk):
            carry = step(i * t + d * blk, carry, shift=d * blk)
        outs = []
        for e in range(G):
            m, acc = carry[e]
            l = jnp.sum(jnp.where(lane == C0, acc, 0.0), axis=-1, keepdims=True)
            outs.append(acc / l)
            qb_ref[0, e] = _put3(qs[e].astype(F32), lane, L0, _split3(-(m + jnp.log(l)))).astype(BF16)
        for pp in range(G // 2):
            a_ref[:, pp * LANES:(pp + 1) * LANES] = jnp.where(lane < HEAD_DIM, outs[2 * pp], pltpu.roll(outs[2 * pp + 1], HEAD_DIM, 1))

    return _pcall(
        body, name="attn_fwd", grid=(B, H // G, n),
        in_specs=[pl.BlockSpec((1, G, t, LANES), lambda b, p, i: (b, p, i, 0)),
                  pl.BlockSpec((1, G, S, LANES), lambda b, p, i: (b, p, 0, 0)),
                  pl.BlockSpec((1, G, S, LANES), lambda b, p, i: (b, p, 0, 0))],
        out_specs=[pl.BlockSpec((t, G * HEAD_DIM), lambda b, p, i: (b * n + i, p)),
                   pl.BlockSpec((1, G, t, LANES), lambda b, p, i: (b, p, i, 0))],
        out_shape=[jax.ShapeDtypeStruct((B * S, H * HEAD_DIM), F32), jax.ShapeDtypeStruct((B, H, S, LANES), BF16)],
    )(qa, ka, va)


def _conv_fwd(z, cw, cb, *, B, S, sec_w, rc):
    C = cw.shape[1]
    nct = C // LANES
    off_a, off_b = 4 * sec_w // LANES, 5 * sec_w // LANES

    def body(ga_ref, gb_ref, w_ref, b_ref, uc_ref, buf):
        buf[0:HALO, :] = jnp.zeros((HALO, LANES), F32)
        buf[HALO:HALO + S, :] = ga_ref[...] * _sig(gb_ref[...])

        def chunk(cidx, carry):
            r0 = pl.multiple_of(cidx * rc, rc)
            acc = jnp.zeros((rc, LANES), F32)
            for j in range(CONV_K):
                acc = acc + w_ref[j:j + 1, :] * buf[pl.ds(r0 + (HALO - CONV_K + 1 + j), rc), :]
            uc_ref[pl.ds(r0, rc), :] = acc + b_ref[...]
            return carry

        lax.fori_loop(0, S // rc, chunk, 0)

    return _pcall(
        body, name="conv_fwd", grid=(B, nct),
        in_specs=[pl.BlockSpec((S, LANES), lambda b, c: (b, off_a + c)), pl.BlockSpec((S, LANES), lambda b, c: (b, off_b + c)),
                  pl.BlockSpec((CONV_K, LANES), lambda b, c: (0, c)), pl.BlockSpec((1, LANES), lambda b, c: (0, c))],
        out_specs=pl.BlockSpec((S, LANES), lambda b, c: (b, c)),
        out_shape=jax.ShapeDtypeStruct((B * S, C), F32),
        scratch_shapes=[pltpu.VMEM((HALO + S, LANES), F32)],
    )(z, z, cw, cb)


def _ln_parts(uc, lg, lb):
    mu = jnp.mean(uc, axis=-1, keepdims=True)
    d = uc - mu
    rs = lax.rsqrt(jnp.mean(d * d, axis=-1, keepdims=True) + EPS)
    un = d * rs
    return rs, un, un * lg + lb


def _conv_post(uc, z, lg, lb, *, tm, sec_w):
    T, C = uc.shape
    off_c = 6 * sec_w // C

    def body(uc_ref, gc_ref, lg_ref, lb_ref, uo_ref):
        _, _, ul = _ln_parts(uc_ref[...], lg_ref[...], lb_ref[...])
        gc = gc_ref[...]
        uo_ref[...] = ((ul * _sig(ul)) * (gc * _sig(gc))).astype(BF16)

    return _pcall(
        body, name="conv_post", grid=(T // tm,),
        in_specs=[pl.BlockSpec((tm, C), lambda i: (i, 0)), pl.BlockSpec((tm, C), lambda i: (i, off_c)),
                  pl.BlockSpec((1, C), lambda i: (0, 0)), pl.BlockSpec((1, C), lambda i: (0, 0))],
        out_specs=pl.BlockSpec((tm, C), lambda i: (i, 0)), out_shape=jax.ShapeDtypeStruct((T, C), BF16),
    )(uc, z, lg, lb)


def _outproj(a, z, uo, x, tgt, wo, *, tm, sec_w):
    T, D = x.shape
    FW = a.shape[1]
    C = uo.shape[1]

    def body(a_ref, gf_ref, uo_ref, x_ref, t_ref, wo_ref, dout_ref, dy_ref, dwo_ref, loss_ref):
        @pl.when(pl.program_id(0) == 0)
        def _():
            dwo_ref[...] = jnp.zeros_like(dwo_ref)
            loss_ref[...] = jnp.zeros_like(loss_ref)

        gf = gf_ref[...]
        ao = (a_ref[...] * (gf * _sig(gf))).astype(BF16)
        uo = uo_ref[...]
        out = x_ref[...] + (_dot(ao, wo_ref[0:FW, :]) + _dot(uo, wo_ref[FW:FW + C, :]))
        err = out - t_ref[...]
        loss_ref[...] += jnp.sum(err * err) * (0.5 / D)
        dout = err * (1.0 / D)
        dout_ref[...] = dout
        db = dout.astype(BF16)
        dy_ref[:, 0:FW] = _dot_nt(db, wo_ref[0:FW, :])
        dy_ref[:, FW:FW + C] = _dot_nt(db, wo_ref[FW:FW + C, :])
        dwo_ref[0:FW, :] += _dot_tn(ao, db)
        dwo_ref[FW:FW + C, :] += _dot_tn(uo, db)

    tok = lambda w, n=0: pl.BlockSpec((tm, w), lambda i: (i, n))
    return _pcall(
        body, name="outproj", grid=(T // tm,),
        in_specs=[tok(FW), tok(sec_w, 3), tok(C), tok(D), tok(D), pl.BlockSpec((FW + C, D), lambda i: (0, 0))],
        out_specs=[tok(D), tok(FW + C), pl.BlockSpec((FW + C, D), lambda i: (0, 0)), pl.BlockSpec((8, LANES), lambda i: (0, 0))],
        out_shape=[jax.ShapeDtypeStruct((T, D), F32), jax.ShapeDtypeStruct((T, FW + C), F32),
                   jax.ShapeDtypeStruct((FW + C, D), F32), jax.ShapeDtypeStruct((8, LANES), F32)],
    )(a, z, uo, x, tgt, wo)


def _conv_post_bwd(dy, uc, z, lg, lb, *, tm, sec_w):
    T, C = uc.shape
    FW = dy.shape[1] - C
    off_c = 6 * sec_w // C

    def body(dy_ref, uc_ref, gc_ref, lg_ref, lb_ref, dgc_ref, duc_ref, small_ref):
        @pl.when(pl.program_id(0) == 0)
        def _():
            small_ref[...] = jnp.zeros_like(small_ref)

        lg = lg_ref[...]
        rs, un, ul = _ln_parts(uc_ref[...], lg, lb_ref[...])
        s_ul = _sig(ul)
        gc = gc_ref[...]
        s_gc = _sig(gc)
        duo = dy_ref[...]
        dgc_ref[...] = (duo * (ul * s_ul) * _dsilu(gc, s_gc)).astype(BF16)
        dul = duo * (gc * s_gc) * _dsilu(ul, s_ul)
        dun = dul * lg
        duc = rs * (dun - jnp.mean(dun, axis=-1, keepdims=True) - un * jnp.mean(dun * un, axis=-1, keepdims=True))
        duc_ref[...] = duc
        small_ref[0:1, :] += jnp.sum(dul * un, axis=0, keepdims=True)
        small_ref[8:9, :] += jnp.sum(dul, axis=0, keepdims=True)
        small_ref[16:17, :] += jnp.sum(duc, axis=0, keepdims=True)

    tok = lambda n: pl.BlockSpec((tm, C), lambda i: (i, n))
    row = pl.BlockSpec((1, C), lambda i: (0, 0))
    return _pcall(
        body, name="conv_post_bwd", grid=(T // tm,),
        in_specs=[tok(FW // C), tok(0), tok(off_c), row, row],
        out_specs=[tok(0), tok(0), pl.BlockSpec((24, C), lambda i: (0, 0))],
        out_shape=[jax.ShapeDtypeStruct((T, C), BF16), jax.ShapeDtypeStruct((T, C), F32), jax.ShapeDtypeStruct((24, C), F32)],
    )(dy, uc, z, lg, lb)


def _conv_bwd(duc, z, cw, *, B, S, sec_w, rc):
    C = cw.shape[1]
    nct = C // LANES
    off_a, off_b = 4 * sec_w // LANES, 5 * sec_w // LANES

    def body(duc_ref, ga_ref, gb_ref, w_ref, dga_ref, dgb_ref, dw_ref, ubuf, dbuf):
        @pl.when(pl.program_id(1) == 0)
        def _():
            dw_ref[...] = jnp.zeros_like(dw_ref)

        ubuf[0:HALO, :] = jnp.zeros((HALO, LANES), F32)
        ubuf[HALO:HALO + S, :] = ga_ref[...] * _sig(gb_ref[...])
        dbuf[0:S, :] = duc_ref[...]
        dbuf[S:S + HALO, :] = jnp.zeros((HALO, LANES), F32)

        def chunk(cidx, carry):
            r0 = pl.multiple_of(cidx * rc, rc)
            acc = jnp.zeros((rc, LANES), F32)
            for j in range(CONV_K):
                acc = acc + w_ref[j:j + 1, :] * dbuf[pl.ds(r0 + (CONV_K - 1 - j), rc), :]
            rows = pl.ds(r0, rc)
            ga, sg = ga_ref[rows, :], _sig(gb_ref[rows, :])
            dga_ref[rows, :] = (acc * sg).astype(BF16)
            dgb_ref[rows, :] = (acc * ga * sg * (1.0 - sg)).astype(BF16)
            return carry

        lax.fori_loop(0, S // rc, chunk, 0)
        d = duc_ref[...]
        for j in range(CONV_K):
            dw_ref[j:j + 1, :] += jnp.sum(d * ubuf[pl.ds(HALO - CONV_K + 1 + j, S), :], axis=0, keepdims=True)

    return _pcall(
        body, name="conv_bwd", grid=(nct, B),
        in_specs=[pl.BlockSpec((S, LANES), lambda c, b: (b, c)), pl.BlockSpec((S, LANES), lambda c, b: (b, off_a + c)),
                  pl.BlockSpec((S, LANES), lambda c, b: (b, off_b + c)), pl.BlockSpec((CONV_K, LANES), lambda c, b: (0, c))],
        out_specs=[pl.BlockSpec((S, LANES), lambda c, b: (b, c)), pl.BlockSpec((S, LANES), lambda c, b: (b, c)),
                   pl.BlockSpec((CONV_K, LANES), lambda c, b: (0, c))],
        out_shape=[jax.ShapeDtypeStruct((B * S, C), BF16), jax.ShapeDtypeStruct((B * S, C), BF16),
                   jax.ShapeDtypeStruct((CONV_K, C), F32)],
        scratch_shapes=[pltpu.VMEM((HALO + S, LANES), F32), pltpu.VMEM((S + HALO, LANES), F32)],
    )(duc, z, z, cw)


def _attn_bwd_prep(dy, a, z, *, B, S, H, ts, sec_w):
    FW = H * HEAD_DIM
    nts = S // ts

    def body(dy_ref, a_ref, gf_ref, dgf_ref, do_ref):
        lane = _lane_iota(ts)
        for p in range(H // 2):
            cols = slice(p * LANES, (p + 1) * LANES)
            gf, av, dya = gf_ref[:, cols], a_ref[:, cols], dy_ref[:, cols]
            sg = _sig(gf)
            dgf_ref[:, cols] = (dya * av * _dsilu(gf, sg)).astype(BF16)
            da = dya * (gf * sg)
            nd = -_half_sums(da * av, lane)
            for e in range(2):
                nd_e = nd[:, e * HEAD_DIM:e * HEAD_DIM + 1]
                do_ref[0, 2 * p + e] = _put3(_head_to_low(da, e, lane), lane, C0, _split3(nd_e)).astype(BF16)

    tok = lambda n: pl.BlockSpec((ts, FW), lambda b, i: (b * nts + i, n))
    return _pcall(
        body, name="attn_bwd_prep", grid=(B, nts),
        in_specs=[tok(0), tok(0), pl.BlockSpec((ts, sec_w), lambda b, i: (b * nts + i, 3))],
        out_specs=[tok(0), pl.BlockSpec((1, H, ts, LANES), lambda b, i: (b, 0, i, 0))],
        out_shape=[jax.ShapeDtypeStruct((B * S, FW), BF16), jax.ShapeDtypeStruct((B, H, S, LANES), BF16)],
    )(dy, a, z)


def _attn_bwd(qb, ka, va, doa, *, t, blk, G, comm=None):
    B, H, S, _ = qb.shape
    n = S // t
    assert t % blk == 0
    grid = (B, H // G, n)

    def body(*refs):
        (q_ref, k_ref, v_ref, do_ref), (dq_ref, dk_ref, dv_ref), _, rider = _split_refs(refs, 4, 3, comm)
        first, last = _grid_ends(grid)
        if comm:
            comm.begin(rider, first)
        j = pl.program_id(2)

        @pl.when(j == 0)
        def _():
            dq_ref[...] = jnp.zeros_like(dq_ref)

        row = lax.broadcasted_iota(jnp.int32, (t, blk), 0)
        col = lax.broadcasted_iota(jnp.int32, (t, blk), 1)
        ks = [k_ref[0, e] for e in range(G)]
        vs = [v_ref[0, e] for e in range(G)]

        def step(start, carry, shift=None):
            rows = pl.ds(pl.multiple_of(start, blk), blk)
            new = []
            for e in range(G):
                dk, dv = carry[e]
                q, do = q_ref[0, e, rows, :], do_ref[0, e, rows, :]
                st = _dot_nt(ks[e], q)
                dpt = _dot_nt(vs[e], do)
                if shift is not None:
                    st = jnp.where(row <= col + shift, st, NEG_INF)
                pt = jnp.exp(st)
                dsb = (pt * dpt).astype(BF16)
                dq_ref[0, e, rows, :] += _dot_tn(dsb, ks[e])
                new.append((dk + _dot(dsb, q), dv + _dot(pt.astype(BF16), do)))
            return tuple(new)

        zero = jnp.zeros((t, LANES), F32)
        carry = tuple((zero, zero) for _ in range(G))
        for d in range(t // blk):
            carry = step(j * t + d * blk, carry, shift=d * blk)
        carry = lax.fori_loop((j + 1) * (t // blk), S // blk, lambda i, c: step(i * blk, c), carry)
        for e in range(G):
            dk_ref[0, e], dv_ref[0, e] = carry[e]
        if comm:
            comm.end(rider, last)

    full = pl.BlockSpec((1, G, S, LANES), lambda b, h, j: (b, h, 0, 0))
    tile = pl.BlockSpec((1, G, t, LANES), lambda b, h, j: (b, h, j, 0))
    shape = jax.ShapeDtypeStruct((B, H, S, LANES), F32)
    return _pcall(
        body, name="attn_bwd", grid=grid,
        **_with_comm(comm, [full, tile, tile, full], [full, tile, tile], [shape, shape, shape], []),
    )(qb, ka, va, doa, *(comm.operands() if comm else ()))


def _attn_post_bwd(dq, dk, dv, z, zf, bfp, gq, gk, *, B, S, H, ts):
    FW = H * HEAD_DIM
    nts = S // ts
    scale = HEAD_DIM ** -0.5

    def body(dq_ref, dk_ref, dv_ref, zq_ref, zk_ref, zf_ref, bf_ref, gq_ref, gk_ref,
             dzq_ref, dzk_ref, dzv_ref, dzf_ref, small_ref, carry):
        first = (pl.program_id(0) == 0) & (pl.program_id(1) == 0)

        @pl.when(first)
        def _():
            small_ref[...] = jnp.zeros_like(small_ref)

        @pl.when(pl.program_id(1) == 0)
        def _():
            carry[...] = jnp.zeros_like(carry)

        lane = _lane_iota(ts)
        dcs = jnp.zeros((ts, LANES), F32)
        for p in range(H // 2):
            cols = slice(p * LANES, (p + 1) * LANES)
            for d_ref, z_ref, g_ref, o_ref, srow, mul in ((dq_ref, zq_ref, gq_ref, dzq_ref, 0, scale),
                                                          (dk_ref, zk_ref, gk_ref, dzk_ref, 1, 1.0)):
                dy = jnp.where(lane < HEAD_DIM, d_ref[0, 2 * p], pltpu.roll(d_ref[0, 2 * p + 1], HEAD_DIM, 1)) * mul
                xv, g = z_ref[:, cols], g_ref[:, cols]
                r = lax.rsqrt(_half_sums(xv * xv, lane) * (1.0 / HEAD_DIM) + EPS)
                dxh = dy * g
                mm = _half_sums(dxh * xv, lane) * (1.0 / HEAD_DIM)
                o_ref[:, cols] = (r * (dxh - xv * (r * r) * mm)).astype(BF16)
                small_ref[8 * srow:8 * srow + 1, cols] += jnp.sum(dy * xv * r, axis=0, keepdims=True)
            dzv_ref[:, cols] = jnp.where(lane < HEAD_DIM, dv_ref[0, 2 * p], pltpu.roll(dv_ref[0, 2 * p + 1], HEAD_DIM, 1)).astype(BF16)
            for e in range(2):
                h = 2 * p + e
                dc = jnp.sum(jnp.where(lane == C0, dq_ref[0, h], 0.0) - jnp.where(lane == K0, dk_ref[0, h], 0.0),
                             axis=-1, keepdims=True)
                dcs = jnp.where(lane == h, dc, dcs)
        ri = lax.broadcasted_iota(jnp.int32, (ts, ts), 0)
        ci = lax.broadcasted_iota(jnp.int32, (ts, ts), 1)
        tri = (ci >= ri).astype(BF16)
        dl = carry[...]
        for piece in _split3(dcs):
            dl = dl + _dot(tri, piece)
        carry[...] = dl[0:1, :]
        xf = zf_ref[...] + bf_ref[...]
        df = jnp.where(lane < H, dl * _sig(-xf), 0.0)
        dzf_ref[...] = df.astype(BF16)
        small_ref[16:17, 0:LANES] += jnp.sum(df, axis=0, keepdims=True)

    rev = lambda b, i: (b, 0, nts - 1 - i, 0)
    head = pl.BlockSpec((1, H, ts, LANES), rev)
    tokrow = lambda b, i: b * nts + nts - 1 - i
    sec = lambda n: pl.BlockSpec((ts, FW), lambda b, i: (tokrow(b, i), n))
    narrow = pl.BlockSpec((ts, LANES), lambda b, i: (tokrow(b, i), 0))
    row = lambda w: pl.BlockSpec((1, w), lambda b, i: (0, 0))
    return _pcall(
        body, name="attn_post_bwd", grid=(B, nts),
        in_specs=[head, head, head, sec(0), sec(1), narrow, row(LANES), row(FW), row(FW)],
        out_specs=[sec(0), sec(0), sec(0), narrow, pl.BlockSpec((24, FW), lambda b, i: (0, 0))],
        out_shape=[jax.ShapeDtypeStruct((B * S, FW), BF16)] * 3 + [jax.ShapeDtypeStruct((B * S, LANES), BF16),
                                                                   jax.ShapeDtypeStruct((24, FW), F32)],
        scratch_shapes=[pltpu.VMEM((1, LANES), F32)],
    )(dq, dk, dv, z, z, zf, bfp, gq, gk)


def _inproj_bwd_x(dzs, dzf, wmt, wft, x, g, dout, *, tm, comm=None):
    T, D = x.shape
    nj = len(dzs)
    tn = dzs[0].shape[1]
    grid = (T // tm, nj)

    def body(*refs):
        ins, (gx_ref, dg_ref), (acc,), rider = _split_refs(refs, nj + 6, 2, comm)
        dz_refs = ins[:nj]
        dzf_ref, wm_ref, wf_ref, x_ref, g_ref, dout_ref = ins[nj:]
        first, last = _grid_ends(grid)
        if comm:
            comm.begin(rider, first)
        i, j = pl.program_id(0), pl.program_id(1)

        @pl.when((i == 0) & (j == 0))
        def _():
            dg_ref[...] = jnp.zeros_like(dg_ref)

        @pl.when(j == 0)
        def _():
            acc[...] = _dot(dzf_ref[...], wf_ref[...])

        for s in range(nj):
            @pl.when(j == s)
            def _():
                acc[...] += _dot(dz_refs[s][...], wm_ref[...])

        @pl.when(j == nj - 1)
        def _():
            dh, xv, g = acc[...], x_ref[...], g_ref[...]
            r = lax.rsqrt(jnp.mean(xv * xv, axis=-1, keepdims=True) + EPS)
            dxh = dh * g
            mm = jnp.mean(dxh * xv, axis=-1, keepdims=True)
            gx_ref[...] = dout_ref[...] + r * (dxh - xv * (r * r) * mm)
            dg_ref[0:1, :] += jnp.sum(dh * xv * r, axis=0, keepdims=True)

        if comm:
            comm.end(rider, last)

    tok = pl.BlockSpec((tm, D), lambda i, j: (i, 0))
    sec = pl.BlockSpec((tm, tn), lambda i, j: (i, 0))
    return _pcall(
        body, name="inproj_bwd_x", grid=grid,
        **_with_comm(
            comm,
            [sec] * nj + [pl.BlockSpec((tm, LANES), lambda i, j: (i, 0)), pl.BlockSpec((tn, D), lambda i, j: (j, 0)),
                          pl.BlockSpec((LANES, D), lambda i, j: (0, 0)), tok, pl.BlockSpec((1, D), lambda i, j: (0, 0)), tok],
            [tok, pl.BlockSpec((8, D), lambda i, j: (0, 0))],
            [jax.ShapeDtypeStruct((T, D), F32), jax.ShapeDtypeStruct((8, D), F32)],
            [pltpu.VMEM((tm, D), F32)]),
    )(*dzs, dzf, wmt, wft, x, g, dout, *(comm.operands() if comm else ()))


def _inproj_bwd_w(h, dz, *, tm, name):
    T, D = h.shape
    W = dz.shape[1]
    ni = T // tm

    def body(h_ref, dz_ref, dw_ref, acc):
        i = pl.program_id(0)

        @pl.when(i == 0)
        def _():
            acc[...] = jnp.zeros_like(acc)

        acc[...] += _dot_tn(dz_ref[...], h_ref[...])

        @pl.when(i == ni - 1)
        def _():
            dw_ref[...] = acc[...].astype(BF16)

    return _pcall(
        body, name=name, grid=(ni,),
        in_specs=[pl.BlockSpec((tm, D), lambda i: (i, 0)), pl.BlockSpec((tm, W), lambda i: (i, 0))],
        out_specs=pl.BlockSpec((W, D), lambda i: (0, 0)), out_shape=jax.ShapeDtypeStruct((W, D), BF16),
        scratch_shapes=[pltpu.VMEM((W, D), F32)],
    )(h, dz)


def _adamw(w, g, m, v):
    m = ADAM_B1 * m + (1.0 - ADAM_B1) * g
    v = ADAM_B2 * v + (1.0 - ADAM_B2) * (g * g)
    m_hat = m / (1.0 - ADAM_B1 ** ADAM_STEP)
    v_hat = v / (1.0 - ADAM_B2 ** ADAM_STEP)
    return -ADAM_LR * (m_hat / (jnp.sqrt(v_hat) + ADAM_EPS) + ADAM_WD * w), m, v


def _adam_update(w, m, v, parts, *, name, tr):
    R, Cc = w.shape[0], w.shape[-1]

    def body(w_ref, m_ref, v_ref, p_ref, g_ref, d_ref, nm_ref, nv_ref):
        g = p_ref[0].astype(F32)
        for k in range(1, N_DEV):
            g = g + p_ref[k].astype(F32)
        g_ref[...] = g
        d_ref[...], nm_ref[...], nv_ref[...] = _adamw(w_ref[...], g, m_ref[...], v_ref[...])

    blk = pl.BlockSpec((tr, Cc), lambda i: (i, 0)) if w.ndim == 2 else pl.BlockSpec((tr, None, Cc), lambda i: (i, 0, 0))
    shape = jax.ShapeDtypeStruct(w.shape, F32)
    return _pcall(
        body, name=name, grid=(pl.cdiv(R, tr),),
        in_specs=[blk, blk, blk, pl.BlockSpec((N_DEV, tr, Cc), lambda i: (0, i, 0))],
        out_specs=[blk, blk, blk, blk], out_shape=[shape, shape, shape, shape],
    )(w, m, v, parts)


MESH = pl.DeviceIdType.MESH
ANY = pl.BlockSpec(memory_space=pl.ANY)


def _flip(v, bit):
    return 1 - v if bit else v


def _allgather(shards):
    n = len(shards)

    def body(*refs):
        srcs, outs = refs[:n], refs[n:2 * n]
        send_sems, recv_sems, local_sems = refs[2 * n:]
        x, y, c = lax.axis_index("x"), lax.axis_index("y"), lax.axis_index("c")
        sibling = (x, y, 1 - c)
        chips = [(1 - x, y), (x, 1 - y), (1 - x, 1 - y)]
        slot = lambda px, py, pc: 4 * px + 2 * py + pc

        def copy(a, k, block, to, src=None):
            dst = outs[a].at[slot(*block)]
            return pltpu.make_async_remote_copy(src_ref=dst if src is None else src, dst_ref=dst, send_sem=send_sems.at[7 * a + k],
                                                recv_sem=recv_sems.at[7 * a + k], device_id=to, device_id_type=MESH)

        mine = [pltpu.make_async_copy(srcs[a], outs[a].at[slot(x, y, c)], local_sems.at[a]) for a in range(n)]
        sends = []
        for a in range(n):
            mine[a].start()
            sends.append(copy(a, 0, (x, y, c), sibling, src=srcs[a]))
            sends += [copy(a, 1 + j, (x, y, c), (*chip, c), src=srcs[a]) for j, chip in enumerate(chips)]
        for cp in sends:
            cp.start()
        for a in range(n):
            for j, chip in enumerate(chips):
                copy(a, 1 + j, (*chip, c), (x, y, c)).wait_recv()
                fwd = copy(a, 4 + j, (*chip, c), sibling)
                fwd.start()
                sends.append(fwd)
        for a in range(n):
            copy(a, 0, (x, y, 1 - c), (x, y, c)).wait_recv()
            for j, chip in enumerate(chips):
                copy(a, 4 + j, (*chip, 1 - c), (x, y, c)).wait_recv()
        for cp in sends:
            cp.wait_send()
        for cp in mine:
            cp.wait()

    return pl.pallas_call(
        body, name="allgather_weights", in_specs=[ANY] * n, out_specs=[ANY] * n,
        out_shape=[jax.ShapeDtypeStruct((N_DEV,) + s.shape, s.dtype) for s in shards],
        scratch_shapes=[pltpu.SemaphoreType.DMA((7 * n,)), pltpu.SemaphoreType.DMA((7 * n,)), pltpu.SemaphoreType.DMA((n,))],
    )(*shards)


class _Comm:
    def __init__(self, arrays, owners, into=None):
        self.arrays, self.owners, self.n = list(arrays), list(owners), len(arrays)
        self.into = list(into) if into else [None] * self.n
        self.carried = [t for t in self.into if t is not None]

    def operands(self):
        return self.arrays + self.carried

    def aliases(self, n_in, n_out):
        out, pos = {}, 0
        for a, t in enumerate(self.into):
            if t is not None:
                out[n_in + self.n + pos] = n_out + a
                pos += 1
        return out

    def out_shape(self):
        return [jax.ShapeDtypeStruct((N_DEV,) + (a.shape if o is None else a.shape[1:]), a.dtype)
                for a, o in zip(self.arrays, self.owners)]

    def scratch(self):
        return [pltpu.SemaphoreType.DMA((7 * self.n,)), pltpu.SemaphoreType.DMA((7 * self.n,)), pltpu.SemaphoreType.DMA((self.n,))]

    def ops(self, srcs, outs, send_sems, recv_sems, local_sems):
        x, y, c = lax.axis_index("x"), lax.axis_index("y"), lax.axis_index("c")
        me = 4 * x + 2 * y + c
        when = lambda p, f: (lambda: pl.when(p)(f))
        starts, waits = [], []
        for a, owners in enumerate(self.owners):
            src, out = srcs[a], outs[a]

            def remote(piece, slot, rel, to, a=a, out=out):
                sem = 7 * a + rel - 1
                return pltpu.make_async_remote_copy(src_ref=piece, dst_ref=out.at[slot], send_sem=send_sems.at[sem],
                                                    recv_sem=recv_sems.at[sem], device_id=to, device_id_type=MESH)

            peers = [(d, (_flip(x, d & 4), _flip(y, d & 2), _flip(c, d & 1))) for d in range(1, N_DEV)]
            if owners is None:
                local = pltpu.make_async_copy(src, out.at[me], local_sems.at[a])
                starts.append(local.start)
                waits.append(local.wait)
                for d, peer in peers:
                    cp = remote(src, me, d, peer)
                    starts.append(cp.start)
                    waits.append(cp.wait_send)
                receives = True
            else:
                for s, dest in enumerate(owners):
                    to = (dest >> 2 & 1, dest >> 1 & 1, dest & 1)
                    rel = 4 * _flip(x, to[0]) + 2 * _flip(y, to[1]) + _flip(c, to[2])
                    cp = remote(src.at[s], me, jnp.maximum(rel, 1), to)
                    local = pltpu.make_async_copy(src.at[s], out.at[me], local_sems.at[a])
                    starts += [when(rel != 0, cp.start), when(rel == 0, local.start)]
                    waits += [when(rel != 0, cp.wait_send), when(rel == 0, local.wait)]
                receives = me == owners[0]
                for dest in owners[1:]:
                    receives = receives | (me == dest)
            piece = src if owners is None else src.at[0]
            for d, peer in peers:
                def arrival(d=d, peer=peer, piece=piece, remote=remote):
                    remote(piece, 4 * peer[0] + 2 * peer[1] + peer[2], d, peer).wait_recv()

                waits.append(arrival if receives is True else when(receives, arrival))
        return starts, waits

    def _phase(self, which, rider, cond):
        srcs, outs, sems = rider

        def go():
            for op in self.ops(srcs, outs, *sems)[which]:
                op()

        go() if cond is True else pl.when(cond)(go)

    def begin(self, rider, first=True):
        self._phase(0, rider, first)

    def end(self, rider, last=True):
        self._phase(1, rider, last)


def _split_refs(refs, n_in, n_out, comm):
    k = comm.n if comm else 0
    o0 = n_in + k + (len(comm.carried) if comm else 0)
    ins, c_src = refs[:n_in], refs[n_in:n_in + k]
    outs, c_out = refs[o0:o0 + n_out], refs[o0 + n_out:o0 + n_out + k]
    rest = refs[o0 + n_out + k:]
    scratch, c_sem = (rest[:-3], rest[-3:]) if comm else (rest, ())
    return ins, outs, scratch, (c_src, c_out, c_sem)


def _grid_ends(grid):
    first = functools.reduce(jnp.logical_and, [pl.program_id(a) == 0 for a in range(len(grid))])
    last = functools.reduce(jnp.logical_and, [pl.program_id(a) == g - 1 for a, g in enumerate(grid)])
    return first, last


def _with_comm(comm, in_specs, out_specs, out_shape, scratch):
    if comm is None:
        return dict(in_specs=in_specs, out_specs=out_specs, out_shape=out_shape, scratch_shapes=scratch)
    return dict(in_specs=in_specs + [ANY] * len(comm.operands()), out_specs=out_specs + [ANY] * comm.n,
                out_shape=out_shape + comm.out_shape(), scratch_shapes=scratch + comm.scratch(),
                input_output_aliases=comm.aliases(len(in_specs), len(out_specs)))


def _exchange(comm, *, name):
    n = comm.n

    def body(*refs):
        rider = (refs[:n], refs[n:2 * n], refs[2 * n:])
        comm.begin(rider)
        comm.end(rider)

    return pl.pallas_call(body, name=name, in_specs=[ANY] * n, out_specs=[ANY] * n, out_shape=comm.out_shape(),
                          scratch_shapes=comm.scratch())(*comm.arrays)


def kernel(x, norm_g, w_in, b_forget, q_norm_g, k_norm_g, conv_w, conv_b, conv_ln_g, conv_ln_b, w_out, loss_target, m_norm_g, m_w_in, m_b_forget, m_q_norm_g, m_k_norm_g, m_conv_w, m_conv_b, m_conv_ln_g, m_conv_ln_b, m_w_out, v_norm_g, v_w_in, v_b_forget, v_q_norm_g, v_k_norm_g, v_conv_w, v_conv_b, v_conv_ln_g, v_conv_ln_b, v_w_out):
    B, S, D = x.shape
    H = q_norm_g.shape[1]
    FW = H * HEAD_DIM
    assert q_norm_g.shape[2] == HEAD_DIM and FW == D and conv_b.shape[1] == D and H <= LANES // 3
    ncol = w_in.shape[2]
    T = B * S

    to_t = lambda t: jnp.transpose(t, (2, 0, 1))
    from_t = lambda t: jnp.transpose(t, (1, 2, 0))
    wt, mt, vt = to_t(w_in), to_t(m_w_in), to_t(v_w_in)

    x2, tgt = x.reshape(T, D), loss_target.reshape(T, D)
    tm, ts, rc = min(512, T), min(256, S), min(64, S)
    everyone = list(range(N_DEV))
    bfp = jnp.pad(b_forget, ((0, 0), (0, LANES - H)))
    gq, gk = q_norm_g.reshape(1, FW), k_norm_g.reshape(1, FW)

    (wg,) = _allgather([wt.reshape(ncol, D).astype(BF16)])
    wmt, wft = _w_relayout(wg, D=D, H=H)
    z, zf, h, wog, cwg = _inproj_fwd(x2, norm_g, wmt, wft, tm=tm, tn=FW,
                                     comm=_Comm([w_out[0].astype(BF16), conv_w[0]], [None, None]))
    wo = wog.reshape(N_DEV * wog.shape[1], D)
    cw = jnp.transpose(cwg, (1, 0, 2)).reshape(CONV_K, D)
    qa, ka, va = _attn_prep(z, zf, bfp, gq, gk, B=B, S=S, H=H, ts=ts)
    a, qb = _attn_fwd(qa, ka, va, t=min(512, S), blk=ts, G=4)
    uc = _conv_fwd(z, cw, conv_b, B=B, S=S, sec_w=FW, rc=rc)
    uo = _conv_post(uc, z, conv_ln_g, conv_ln_b, tm=ts, sec_w=FW)
    dout, dy, dwo, loss = _outproj(a, z, uo, x2, tgt, wo, tm=ts, sec_w=FW)

    dgc, duc, small_c = _conv_post_bwd(dy, uc, z, conv_ln_g, conv_ln_b, tm=ts, sec_w=FW)
    dga, dgb, dcw = _conv_bwd(duc, z, cw, B=B, S=S, sec_w=FW, rc=rc)
    dgf, doa = _attn_bwd_prep(dy, a, z, B=B, S=S, H=H, ts=ts, sec_w=FW)
    _, f1, _ = _section_rows(D, H)
    late_rows = [k for k in everyone if k * ncol >= f1]
    early_rows = [k for k in everyone if k not in late_rows]
    first_late_sec = (late_rows[0] * ncol - H) // FW
    dws = {s: _inproj_bwd_w(h, dz, tm=tm, name=f"inproj_bwd_w{s}") for s, dz in ((3, dgf), (4, dga), (5, dgb), (6, dgc))}
    dw_late = _dw_relayout({s: dws[s] for s in range(first_late_sec, 7)}, None, R=ncol, H=H, shards=late_rows, name="dw_relayout_late")
    dwo_parts = dwo.reshape(N_DEV, dwo.shape[0] // N_DEV, D).astype(BF16)
    dcw_parts = jnp.transpose(dcw.reshape(CONV_K, N_DEV, D // N_DEV), (1, 0, 2))
    dq, dk, dv, r_w, r_wo, r_cw = _attn_bwd(qb, ka, va, doa, t=min(512, S), blk=ts, G=4,
                                            comm=_Comm([dw_late, dwo_parts, dcw_parts], [late_rows, everyone, everyone]))
    dzq, dzk, dzv, dzf, small_a = _attn_post_bwd(dq, dk, dv, z, zf, bfp, gq, gk, B=B, S=S, H=H, ts=ts)
    dws.update({s: _inproj_bwd_w(h, dz, tm=tm, name=f"inproj_bwd_w{s}") for s, dz in ((0, dzq), (1, dzk), (2, dzv))})
    dwf = _inproj_bwd_w(h, dzf, tm=tm, name="inproj_bwd_wf")
    last_early_sec = (early_rows[-1] * ncol + ncol - 1 - H) // FW
    dw_early = _dw_relayout({s: dws[s] for s in range(last_early_sec + 1)}, dwf, R=ncol, H=H, shards=early_rows, name="dw_relayout_early")
    gx, dg, r_w = _inproj_bwd_x([dzq, dzk, dzv, dgf, dga, dgb, dgc], dzf, wmt, wft, x2, norm_g, dout, tm=tm,
                                comm=_Comm([dw_early], [early_rows], into=[r_w]))
    small = jnp.concatenate([dg, small_a[16:24], small_a[0:8], small_a[8:16], small_c[16:24], small_c[0:8], small_c[8:16],
                             jnp.pad(loss, ((0, 0), (0, D - LANES)))], axis=0)
    (r_small,) = _exchange(_Comm([small], [None]), name="exchange_small")

    g_w, d_w, nm_w, nv_w = [from_t(t) for t in _adam_update(wt, mt, vt, r_w, name="adam_w_in", tr=min(128, D))]
    g_wo, d_wo, nm_wo, nv_wo = [t[None] for t in _adam_update(w_out[0], m_w_out[0], v_w_out[0], r_wo, name="adam_w_out",
                                                               tr=min(128, w_out.shape[1]))]
    g_cw, d_cw, nm_cw, nv_cw = [t[None] for t in _adam_update(conv_w[0], m_conv_w[0], v_conv_w[0], r_cw, name="adam_conv_w", tr=CONV_K)]
    tile = lambda t: jnp.pad(t.reshape(1, -1), ((0, 7), (0, D - t.size)))
    pack = lambda *ps: jnp.concatenate([tile(p) for p in ps] + [jnp.zeros((8, D), F32)], axis=0)
    small_w = pack(norm_g, b_forget, q_norm_g, k_norm_g, conv_b, conv_ln_g, conv_ln_b)
    small_m = pack(m_norm_g, m_b_forget, m_q_norm_g, m_k_norm_g, m_conv_b, m_conv_ln_g, m_conv_ln_b)
    small_v = pack(v_norm_g, v_b_forget, v_q_norm_g, v_k_norm_g, v_conv_b, v_conv_ln_g, v_conv_ln_b)
    g_s, d_s, nm_s, nv_s = _adam_update(small_w, small_m, small_v, r_small, name="adam_small", tr=small_w.shape[0])

    def leaves(small_t, w_t, cw_t, wo_t):
        row = lambda r, like: small_t[8 * r:8 * r + 1, :like.size].reshape(like.shape)
        return [row(0, norm_g), w_t, row(1, b_forget), row(2, q_norm_g), row(3, k_norm_g), cw_t, row(4, conv_b),
                row(5, conv_ln_g), row(6, conv_ln_b), wo_t]

    return (g_s[56, 0], gx.reshape(B, S, D), *leaves(g_s, g_w, g_cw, g_wo), *leaves(d_s, d_w, d_cw, d_wo),
            *leaves(nm_s, nm_w, nm_cw, nm_wo), *leaves(nv_s, nv_w, nv_cw, nv_wo))
```

```python
import functools

import jax
import jax.numpy as jnp
from jax import lax
from jax.experimental import pallas as pl
from jax.experimental.pallas import tpu as pltpu

F32, BF16 = jnp.float32, jnp.bfloat16
EPS = 1e-6
NEG_INF = -1e30
CONV_K = 31
HEAD_DIM = 64
ADAM_LR, ADAM_B1, ADAM_B2, ADAM_EPS, ADAM_WD, ADAM_STEP = 0.001, 0.9, 0.999, 1e-08, 0.01, 10

LANES = 128
HALO = 32
N_DEV = 8
VMEM_LIMIT = 56 * 1024 * 1024

C0, K0, L0 = HEAD_DIM, HEAD_DIM + 3, HEAD_DIM + 6


def _pcall(body, *, name, vmem=VMEM_LIMIT, **kw):
    return pl.pallas_call(body, name=name, compiler_params=pltpu.CompilerParams(vmem_limit_bytes=vmem), **kw)


def _dot(a, b):
    return jnp.dot(a, b, preferred_element_type=F32)


def _dot_nt(a, b):
    return lax.dot_general(a, b, (((1,), (1,)), ((), ())), preferred_element_type=F32)


def _dot_tn(a, b):
    return lax.dot_general(a, b, (((0,), (0,)), ((), ())), preferred_element_type=F32)


def _split3(x):
    a = x.astype(BF16)
    r = x - a.astype(F32)
    b = r.astype(BF16)
    c = (r - b.astype(F32)).astype(BF16)
    return a, b, c


def _sig(x):
    return jax.nn.sigmoid(x)


def _dsilu(x, s):
    return s * (1.0 + x * (1.0 - s))


def _lane_iota(rows):
    return lax.broadcasted_iota(jnp.int32, (rows, LANES), 1)


def _half_sums(v):
    r = lax.broadcasted_iota(jnp.int32, (LANES, LANES), 0) < HEAD_DIM
    c = lax.broadcasted_iota(jnp.int32, (LANES, LANES), 1) < HEAD_DIM
    ones = (r == c).astype(BF16)
    a, b, d = _split3(v)
    return _dot(a, ones) + _dot(b, ones) + _dot(d, ones)


def _head_to_low(vb, e):
    r = lax.broadcasted_iota(jnp.int32, (LANES, LANES), 0)
    c = lax.broadcasted_iota(jnp.int32, (LANES, LANES), 1)
    return _dot(vb, ((r == c + e * HEAD_DIM) & (c < HEAD_DIM)).astype(BF16))


def _head_to_low_roll(v, e, lane):
    if e:
        v = pltpu.roll(v, HEAD_DIM, 1)
    return jnp.where(lane < HEAD_DIM, v, 0.0)


def _put3(base, lane, first, pieces):
    for n, p in enumerate(pieces):
        base = jnp.where(lane == first + n, p.astype(F32), base)
    return base


def _section_rows(D, H):
    FW = H * HEAD_DIM
    return 3 * FW, 3 * FW + H, 7 * FW + H


def _row_pieces(a, b, R, step):
    out = []
    while a < b:
        k = a // R
        lo = a - k * R
        hi = min(R, lo + min(step, b - a))
        out.append((k, lo, hi))
        a += hi - lo
    return out


def _w_relayout(wg, *, D, H):
    R = wg.shape[1]
    FW = H * HEAD_DIM
    f0, f1, end = _section_rows(D, H)

    def body(wg_ref, wm_ref, wf_ref):
        def put(dst_ref, d0, a, b):
            for k, lo, hi in _row_pieces(a, b, R, 256):
                dst_ref[d0:d0 + hi - lo, :] = wg_ref[k, lo:hi, :]
                d0 += hi - lo

        put(wm_ref, 0, 0, f0)
        put(wm_ref, f0, f1, end)
        wf_ref[...] = jnp.zeros_like(wf_ref)
        put(wf_ref, 0, f0, f1)

    return _pcall(body, name="w_relayout",
                  out_shape=[jax.ShapeDtypeStruct((7 * FW, D), BF16), jax.ShapeDtypeStruct((LANES, D), BF16)])(wg)


def _dw_relayout(secs, dwf, *, R, H, shards, name):
    FW, D = next(iter(secs.values())).shape
    f0, f1, end = _section_rows(D, H)
    order = sorted(secs)

    def body(*refs):
        sec_refs = dict(zip(order, refs))
        dwf_ref, out_ref = (refs[-2] if dwf is not None else None), refs[-1]

        def src(g0, g1):
            if f0 <= g0 < f1:
                return dwf_ref[g0 - f0:g1 - f0, :]
            s0 = g0 if g0 < f0 else g0 - H
            return sec_refs[s0 // FW][s0 % FW:s0 % FW + (g1 - g0), :]

        cuts = sorted({0, end, f0, f1} | {s * FW for s in range(4)} | {f1 + s * FW for s in range(5)})
        for slot, k in enumerate(shards):
            g = k * R
            while g < (k + 1) * R:
                nxt = min(min(c for c in cuts if c > g), (k + 1) * R, g + 256)
                out_ref[slot, g - k * R:nxt - k * R, :] = src(g, nxt)
                g = nxt

    args = [secs[s] for s in order] + ([dwf] if dwf is not None else [])
    return _pcall(body, name=name, out_shape=jax.ShapeDtypeStruct((len(shards), R, D), BF16))(*args)


def _inproj_fwd(x, g, wmt, wft, *, tm, tn, comm=None):
    T, D = x.shape
    nj = wmt.shape[0] // tn
    grid = (T // tm, nj)

    def body(*refs):
        (x_ref, g_ref, wm_ref, wf_ref), (z_ref, zf_ref, h_ref), (h_scr,), rider = _split_refs(refs, 4, 3, comm)
        first, last = _grid_ends(grid)
        if comm:
            comm.begin(rider, first)

        @pl.when(pl.program_id(1) == 0)
        def _():
            xv = x_ref[...]
            r = lax.rsqrt(jnp.mean(xv * xv, axis=-1, keepdims=True) + EPS)
            h = ((xv * r) * g_ref[...]).astype(BF16)
            h_scr[...] = h
            h_ref[...] = h
            zf_ref[...] = _dot_nt(h, wf_ref[...])

        z_ref[...] = _dot_nt(h_scr[...], wm_ref[...])
        if comm:
            comm.end(rider, last)

    return _pcall(
        body, name="inproj_fwd", grid=grid,
        **_with_comm(
            comm,
            [pl.BlockSpec((tm, D), lambda i, j: (i, 0)), pl.BlockSpec((1, D), lambda i, j: (0, 0)),
             pl.BlockSpec((tn, D), lambda i, j: (j, 0)), pl.BlockSpec((LANES, D), lambda i, j: (0, 0))],
            [pl.BlockSpec((tm, tn), lambda i, j: (i, j)), pl.BlockSpec((tm, LANES), lambda i, j: (i, 0)),
             pl.BlockSpec((tm, D), lambda i, j: (i, 0))],
            [jax.ShapeDtypeStruct((T, wmt.shape[0]), F32), jax.ShapeDtypeStruct((T, LANES), F32), jax.ShapeDtypeStruct((T, D), BF16)],
            [pltpu.VMEM((tm, D), BF16)]),
    )(x, g, wmt, wft, *(comm.operands() if comm else ()))


def _attn_prep(z, zf, bfp, gq, gk, *, B, S, H, ts):
    FW = H * HEAD_DIM
    nts = S // ts
    scale = HEAD_DIM ** -0.5

    def body(zq_ref, zk_ref, zv_ref, zf_ref, bf_ref, gq_ref, gk_ref, qa_ref, ka_ref, va_ref, carry):
        @pl.when(pl.program_id(1) == 0)
        def _():
            carry[...] = jnp.zeros_like(carry)

        lane = _lane_iota(ts)
        xf = zf_ref[...] + bf_ref[...]
        logf = jnp.minimum(xf, 0.0) - jnp.log1p(jnp.exp(-jnp.abs(xf)))
        logf = jnp.where(lane < H, logf, 0.0)
        ri = lax.broadcasted_iota(jnp.int32, (ts, ts), 0)
        ci = lax.broadcasted_iota(jnp.int32, (ts, ts), 1)
        tri = (ri >= ci).astype(BF16)
        c = carry[...]
        for piece in _split3(logf):
            c = c + _dot(tri, piece)
        carry[...] = c[ts - 1:ts, :]
        c1, c2, c3 = _split3(c)
        cpk = c1.astype(F32) + pltpu.roll(c2.astype(F32), H, 1) + pltpu.roll(c3.astype(F32), 2 * H, 1)
        cq, ck = cpk.astype(BF16), (-cpk).astype(BF16)
        prow = lax.broadcasted_iota(jnp.int32, (LANES, LANES), 0)
        pcol = lax.broadcasted_iota(jnp.int32, (LANES, LANES), 1)
        ones_q = jnp.where((lane >= K0) & (lane < K0 + 3), 1.0, 0.0)
        ones_k = jnp.where(((lane >= C0) & (lane < C0 + 3)) | ((lane >= L0) & (lane < L0 + 3)), 1.0, 0.0)
        ones_v = jnp.where((lane >= C0) & (lane < C0 + 3), 1.0, 0.0)
        for h in range(H):
            p, e = divmod(h, 2)
            cols = slice(p * LANES, (p + 1) * LANES)
            piece = jnp.where(prow == h, 0, jnp.where(prow == H + h, 1, jnp.where(prow == 2 * H + h, 2, -LANES)))
            aug_q = _dot(cq, (pcol == C0 + piece).astype(BF16)) + ones_q
            aug_k = _dot(ck, (pcol == K0 + piece).astype(BF16)) + ones_k
            for z_ref, g_ref, o_ref, aug, mul in ((zq_ref, gq_ref, qa_ref, aug_q, scale), (zk_ref, gk_ref, ka_ref, aug_k, 1.0)):
                blk = z_ref[:, cols]
                r = lax.rsqrt(_half_sums(blk * blk) * (1.0 / HEAD_DIM) + EPS)
                y = ((blk * r) * g_ref[:, cols]).astype(BF16).astype(F32) * mul
                o_ref[0, h] = (_head_to_low_roll(y, e, lane) + aug).astype(BF16)
            va_ref[0, h] = (_head_to_low_roll(zv_ref[:, cols], e, lane) + ones_v).astype(BF16)

    sec = lambda n: pl.BlockSpec((ts, FW), lambda b, i: (b * nts + i, n))
    row = lambda w: pl.BlockSpec((1, w), lambda b, i: (0, 0))
    head = pl.BlockSpec((1, H, ts, LANES), lambda b, i: (b, 0, i, 0))
    hshape = jax.ShapeDtypeStruct((B, H, S, LANES), BF16)
    return _pcall(
        body, name="attn_prep", grid=(B, nts),
        in_specs=[sec(0), sec(1), sec(2), pl.BlockSpec((ts, LANES), lambda b, i: (b * nts + i, 0)), row(LANES), row(FW), row(FW)],
        out_specs=[head, head, head], out_shape=[hshape, hshape, hshape],
        scratch_shapes=[pltpu.VMEM((1, LANES), F32)],
    )(z, z, z, zf, bfp, gq, gk)


def _attn_fwd(qa, ka, va, *, t, blk, G):
    B, H, S, _ = qa.shape
    n = S // t
    assert t % blk == 0

    def body(q_ref, k_ref, v_ref, a_ref, qb_ref):
        i = pl.program_id(2)
        lane = _lane_iota(t)
        row = lax.broadcasted_iota(jnp.int32, (t, blk), 0)
        col = lax.broadcasted_iota(jnp.int32, (t, blk), 1)
        qs = [q_ref[0, e] for e in range(G)]

        def step(start, carry, shift=None):
            rows = pl.ds(pl.multiple_of(start, blk), blk)
            new = []
            for e in range(G):
                m, acc = carry[e]
                s = _dot_nt(qs[e], k_ref[0, e, rows, :])
                if shift is not None:
                    s = jnp.where(col + shift <= row, s, NEG_INF)
                m_new = jnp.maximum(m, jnp.max(s, axis=-1, keepdims=True))
                p = jnp.exp(s - m_new)
                new.append((m_new, jnp.exp(m - m_new) * acc + _dot(p.astype(BF16), v_ref[0, e, rows, :])))
            return tuple(new)

        carry = tuple((jnp.full((t, 1), NEG_INF, F32), jnp.zeros((t, LANES), F32)) for _ in range(G))
        carry = lax.fori_loop(0, i * (t // blk), lambda j, c: step(j * blk, c), carry)
        for d in range(t // blk):
            carry = step(i * t + d * blk, carry, shift=d * blk)
        outs = []
        for e in range(G):
            m, acc = carry[e]
            l = jnp.sum(jnp.where(lane == C0, acc, 0.0), axis=-1, keepdims=True)
            outs.append(acc / l)
            qb_ref[0, e] = _put3(qs[e].astype(F32), lane, L0, _split3(-(m + jnp.log(l)))).astype(BF16)
        for pp in range(G // 2):
            a_ref[:, pp * LANES:(pp + 1) * LANES] = jnp.where(lane < HEAD_DIM, outs[2 * pp], pltpu.roll(outs[2 * pp + 1], HEAD_DIM, 1))

    return _pcall(
        body, name="attn_fwd", grid=(B, H // G, n),
        in_specs=[pl.BlockSpec((1, G, t, LANES), lambda b, p, i: (b, p, i, 0)),
                  pl.BlockSpec((1, G, S, LANES), lambda b, p, i: (b, p, 0, 0)),
                  pl.BlockSpec((1, G, S, LANES), lambda b, p, i: (b, p, 0, 0))],
        out_specs=[pl.BlockSpec((t, G * HEAD_DIM), lambda b, p, i: (b * n + i, p)),
                   pl.BlockSpec((1, G, t, LANES), lambda b, p, i: (b, p, i, 0))],
        out_shape=[jax.ShapeDtypeStruct((B * S, H * HEAD_DIM), F32), jax.ShapeDtypeStruct((B, H, S, LANES), BF16)],
    )(qa, ka, va)


def _conv_fwd(z, cw, cb, *, B, S, sec_w, rc):
    C = cw.shape[1]
    nct = C // LANES
    off_a, off_b = 4 * sec_w // LANES, 5 * sec_w // LANES

    def body(ga_ref, gb_ref, w_ref, b_ref, uc_ref, buf):
        buf[0:HALO, :] = jnp.zeros((HALO, LANES), F32)
        buf[HALO:HALO + S, :] = ga_ref[...] * _sig(gb_ref[...])

        def chunk(cidx, carry):
            r0 = pl.multiple_of(cidx * rc, rc)
            acc = jnp.zeros((rc, LANES), F32)
            for j in range(CONV_K):
                acc = acc + w_ref[j:j + 1, :] * buf[pl.ds(r0 + (HALO - CONV_K + 1 + j), rc), :]
            uc_ref[pl.ds(r0, rc), :] = acc + b_ref[...]
            return carry

        lax.fori_loop(0, S // rc, chunk, 0)

    return _pcall(
        body, name="conv_fwd", grid=(B, nct),
        in_specs=[pl.BlockSpec((S, LANES), lambda b, c: (b, off_a + c)), pl.BlockSpec((S, LANES), lambda b, c: (b, off_b + c)),
                  pl.BlockSpec((CONV_K, LANES), lambda b, c: (0, c)), pl.BlockSpec((1, LANES), lambda b, c: (0, c))],
        out_specs=pl.BlockSpec((S, LANES), lambda b, c: (b, c)),
        out_shape=jax.ShapeDtypeStruct((B * S, C), F32),
        scratch_shapes=[pltpu.VMEM((HALO + S, LANES), F32)],
    )(z, z, cw, cb)


def _ln_parts(uc, lg, lb):
    mu = jnp.mean(uc, axis=-1, keepdims=True)
    d = uc - mu
    rs = lax.rsqrt(jnp.mean(d * d, axis=-1, keepdims=True) + EPS)
    un = d * rs
    return rs, un, un * lg + lb


def _conv_post(uc, z, lg, lb, *, tm, sec_w):
    T, C = uc.shape
    off_c = 6 * sec_w // C

    def body(uc_ref, gc_ref, lg_ref, lb_ref, uo_ref):
        _, _, ul = _ln_parts(uc_ref[...], lg_ref[...], lb_ref[...])
        gc = gc_ref[...]
        uo_ref[...] = ((ul * _sig(ul)) * (gc * _sig(gc))).astype(BF16)

    return _pcall(
        body, name="conv_post", grid=(T // tm,),
        in_specs=[pl.BlockSpec((tm, C), lambda i: (i, 0)), pl.BlockSpec((tm, C), lambda i: (i, off_c)),
                  pl.BlockSpec((1, C), lambda i: (0, 0)), pl.BlockSpec((1, C), lambda i: (0, 0))],
        out_specs=pl.BlockSpec((tm, C), lambda i: (i, 0)), out_shape=jax.ShapeDtypeStruct((T, C), BF16),
    )(uc, z, lg, lb)


def _outproj(a, z, uo, x, tgt, wo, *, tm, sec_w):
    T, D = x.shape
    FW = a.shape[1]
    C = uo.shape[1]

    def body(a_ref, gf_ref, uo_ref, x_ref, t_ref, wo_ref, dout_ref, dy_ref, dwo_ref, loss_ref):
        @pl.when(pl.program_id(0) == 0)
        def _():
            dwo_ref[...] = jnp.zeros_like(dwo_ref)
            loss_ref[...] = jnp.zeros_like(loss_ref)

        gf = gf_ref[...]
        ao = (a_ref[...] * (gf * _sig(gf))).astype(BF16)
        uo = uo_ref[...]
        out = x_ref[...] + (_dot(ao, wo_ref[0:FW, :]) + _dot(uo, wo_ref[FW:FW + C, :]))
        err = out - t_ref[...]
        loss_ref[...] += jnp.sum(err * err) * (0.5 / D)
        dout = err * (1.0 / D)
        dout_ref[...] = dout
        db = dout.astype(BF16)
        dy_ref[:, 0:FW] = _dot_nt(db, wo_ref[0:FW, :])
        dy_ref[:, FW:FW + C] = _dot_nt(db, wo_ref[FW:FW + C, :])
        dwo_ref[0:FW, :] += _dot_tn(ao, db)
        dwo_ref[FW:FW + C, :] += _dot_tn(uo, db)

    tok = lambda w, n=0: pl.BlockSpec((tm, w), lambda i: (i, n))
    return _pcall(
        body, name="outproj", grid=(T // tm,),
        in_specs=[tok(FW), tok(sec_w, 3), tok(C), tok(D), tok(D), pl.BlockSpec((FW + C, D), lambda i: (0, 0))],
        out_specs=[tok(D), tok(FW + C), pl.BlockSpec((FW + C, D), lambda i: (0, 0)), pl.BlockSpec((8, LANES), lambda i: (0, 0))],
        out_shape=[jax.ShapeDtypeStruct((T, D), F32), jax.ShapeDtypeStruct((T, FW + C), F32),
                   jax.ShapeDtypeStruct((FW + C, D), F32), jax.ShapeDtypeStruct((8, LANES), F32)],
    )(a, z, uo, x, tgt, wo)


def _conv_post_bwd(dy, uc, z, lg, lb, *, tm, sec_w):
    T, C = uc.shape
    FW = dy.shape[1] - C
    off_c = 6 * sec_w // C

    def body(dy_ref, uc_ref, gc_ref, lg_ref, lb_ref, dgc_ref, duc_ref, small_ref):
        @pl.when(pl.program_id(0) == 0)
        def _():
            small_ref[...] = jnp.zeros_like(small_ref)

        lg = lg_ref[...]
        rs, un, ul = _ln_parts(uc_ref[...], lg, lb_ref[...])
        s_ul = _sig(ul)
        gc = gc_ref[...]
        s_gc = _sig(gc)
        duo = dy_ref[...]
        dgc_ref[...] = (duo * (ul * s_ul) * _dsilu(gc, s_gc)).astype(BF16)
        dul = duo * (gc * s_gc) * _dsilu(ul, s_ul)
        dun = dul * lg
        duc = rs * (dun - jnp.mean(dun, axis=-1, keepdims=True) - un * jnp.mean(dun * un, axis=-1, keepdims=True))
        duc_ref[...] = duc
        small_ref[0:1, :] += jnp.sum(dul * un, axis=0, keepdims=True)
        small_ref[8:9, :] += jnp.sum(dul, axis=0, keepdims=True)
        small_ref[16:17, :] += jnp.sum(duc, axis=0, keepdims=True)

    tok = lambda n: pl.BlockSpec((tm, C), lambda i: (i, n))
    row = pl.BlockSpec((1, C), lambda i: (0, 0))
    return _pcall(
        body, name="conv_post_bwd", grid=(T // tm,),
        in_specs=[tok(FW // C), tok(0), tok(off_c), row, row],
        out_specs=[tok(0), tok(0), pl.BlockSpec((24, C), lambda i: (0, 0))],
        out_shape=[jax.ShapeDtypeStruct((T, C), BF16), jax.ShapeDtypeStruct((T, C), F32), jax.ShapeDtypeStruct((24, C), F32)],
    )(dy, uc, z, lg, lb)


def _conv_bwd(duc, z, cw, *, B, S, sec_w, rc):
    C = cw.shape[1]
    nct = C // LANES
    off_a, off_b = 4 * sec_w // LANES, 5 * sec_w // LANES

    def body(duc_ref, ga_ref, gb_ref, w_ref, dga_ref, dgb_ref, dw_ref, ubuf, dbuf):
        @pl.when(pl.program_id(1) == 0)
        def _():
            dw_ref[...] = jnp.zeros_like(dw_ref)

        ubuf[0:HALO, :] = jnp.zeros((HALO, LANES), F32)
        ubuf[HALO:HALO + S, :] = ga_ref[...] * _sig(gb_ref[...])
        dbuf[0:S, :] = duc_ref[...]
        dbuf[S:S + HALO, :] = jnp.zeros((HALO, LANES), F32)

        def chunk(cidx, carry):
            r0 = pl.multiple_of(cidx * rc, rc)
            acc = jnp.zeros((rc, LANES), F32)
            for j in range(CONV_K):
                acc = acc + w_ref[j:j + 1, :] * dbuf[pl.ds(r0 + (CONV_K - 1 - j), rc), :]
            rows = pl.ds(r0, rc)
            ga, sg = ga_ref[rows, :], _sig(gb_ref[rows, :])
            dga_ref[rows, :] = (acc * sg).astype(BF16)
            dgb_ref[rows, :] = (acc * ga * sg * (1.0 - sg)).astype(BF16)
            return carry

        lax.fori_loop(0, S // rc, chunk, 0)
        d = duc_ref[...]
        for j in range(CONV_K):
            dw_ref[j:j + 1, :] += jnp.sum(d * ubuf[pl.ds(HALO - CONV_K + 1 + j, S), :], axis=0, keepdims=True)

    return _pcall(
        body, name="conv_bwd", grid=(nct, B),
        in_specs=[pl.BlockSpec((S, LANES), lambda c, b: (b, c)), pl.BlockSpec((S, LANES), lambda c, b: (b, off_a + c)),
                  pl.BlockSpec((S, LANES), lambda c, b: (b, off_b + c)), pl.BlockSpec((CONV_K, LANES), lambda c, b: (0, c))],
        out_specs=[pl.BlockSpec((S, LANES), lambda c, b: (b, c)), pl.BlockSpec((S, LANES), lambda c, b: (b, c)),
                   pl.BlockSpec((CONV_K, LANES), lambda c, b: (0, c))],
        out_shape=[jax.ShapeDtypeStruct((B * S, C), BF16), jax.ShapeDtypeStruct((B * S, C), BF16),
                   jax.ShapeDtypeStruct((CONV_K, C), F32)],
        scratch_shapes=[pltpu.VMEM((HALO + S, LANES), F32), pltpu.VMEM((S + HALO, LANES), F32)],
    )(duc, z, z, cw)


def _attn_bwd_prep(dy, a, z, *, B, S, H, ts, sec_w):
    FW = H * HEAD_DIM
    nts = S // ts

    def body(dy_ref, a_ref, gf_ref, dgf_ref, do_ref):
        lane = _lane_iota(ts)
        for p in range(H // 2):
            cols = slice(p * LANES, (p + 1) * LANES)
            gf, av, dya = gf_ref[:, cols], a_ref[:, cols], dy_ref[:, cols]
            sg = _sig(gf)
            dgf_ref[:, cols] = (dya * av * _dsilu(gf, sg)).astype(BF16)
            da = dya * (gf * sg)
            nd = -_half_sums(da * av)
            dab = da.astype(BF16)
            for e in range(2):
                nd_e = nd[:, e * HEAD_DIM:e * HEAD_DIM + 1]
                do_ref[0, 2 * p + e] = _put3(_head_to_low(dab, e), lane, C0, _split3(nd_e)).astype(BF16)

    tok = lambda n: pl.BlockSpec((ts, FW), lambda b, i: (b * nts + i, n))
    return _pcall(
        body, name="attn_bwd_prep", grid=(B, nts),
        in_specs=[tok(0), tok(0), pl.BlockSpec((ts, sec_w), lambda b, i: (b * nts + i, 3))],
        out_specs=[tok(0), pl.BlockSpec((1, H, ts, LANES), lambda b, i: (b, 0, i, 0))],
        out_shape=[jax.ShapeDtypeStruct((B * S, FW), BF16), jax.ShapeDtypeStruct((B, H, S, LANES), BF16)],
    )(dy, a, z)


def _attn_bwd(qb, ka, va, doa, *, t, blk, G, comm=None):
    B, H, S, _ = qb.shape
    n = S // t
    assert t % blk == 0
    grid = (B, H // G, n)

    def body(*refs):
        (q_ref, k_ref, v_ref, do_ref), (dq_ref, dk_ref, dv_ref), _, rider = _split_refs(refs, 4, 3, comm)
        first, last = _grid_ends(grid)
        if comm:
            comm.begin(rider, first)
        j = pl.program_id(2)

        @pl.when(j == 0)
        def _():
            dq_ref[...] = jnp.zeros_like(dq_ref)

        row = lax.broadcasted_iota(jnp.int32, (t, blk), 0)
        col = lax.broadcasted_iota(jnp.int32, (t, blk), 1)
        ks = [k_ref[0, e] for e in range(G)]
        vs = [v_ref[0, e] for e in range(G)]

        def step(start, carry, shift=None):
            rows = pl.ds(pl.multiple_of(start, blk), blk)
            new = []
            for e in range(G):
                dk, dv = carry[e]
                q, do = q_ref[0, e, rows, :], do_ref[0, e, rows, :]
                st = _dot_nt(ks[e], q)
                dpt = _dot_nt(vs[e], do)
                if shift is not None:
                    st = jnp.where(row <= col + shift, st, NEG_INF)
                pt = jnp.exp(st)
                dsb = (pt * dpt).astype(BF16)
                dq_ref[0, e, rows, :] += _dot_tn(dsb, ks[e])
                new.append((dk + _dot(dsb, q), dv + _dot(pt.astype(BF16), do)))
            return tuple(new)

        zero = jnp.zeros((t, LANES), F32)
        carry = tuple((zero, zero) for _ in range(G))
        for d in range(t // blk):
            carry = step(j * t + d * blk, carry, shift=d * blk)
        carry = lax.fori_loop((j + 1) * (t // blk), S // blk, lambda i, c: step(i * blk, c), carry)
        for e in range(G):
            dk_ref[0, e], dv_ref[0, e] = carry[e]
        if comm:
            comm.end(rider, last)

    full = pl.BlockSpec((1, G, S, LANES), lambda b, h, j: (b, h, 0, 0))
    tile = pl.BlockSpec((1, G, t, LANES), lambda b, h, j: (b, h, j, 0))
    shape = jax.ShapeDtypeStruct((B, H, S, LANES), F32)
    return _pcall(
        body, name="attn_bwd", grid=grid,
        **_with_comm(comm, [full, tile, tile, full], [full, tile, tile], [shape, shape, shape], []),
    )(qb, ka, va, doa, *(comm.operands() if comm else ()))


def _attn_post_bwd(dq, dk, dv, z, zf, bfp, gq, gk, *, B, S, H, ts):
    FW = H * HEAD_DIM
    nts = S // ts
    scale = HEAD_DIM ** -0.5

    def body(dq_ref, dk_ref, dv_ref, zq_ref, zk_ref, zf_ref, bf_ref, gq_ref, gk_ref,
             dzq_ref, dzk_ref, dzv_ref, dzf_ref, small_ref, carry):
        first = (pl.program_id(0) == 0) & (pl.program_id(1) == 0)

        @pl.when(first)
        def _():
            small_ref[...] = jnp.zeros_like(small_ref)

        @pl.when(pl.program_id(1) == 0)
        def _():
            carry[...] = jnp.zeros_like(carry)

        lane = _lane_iota(ts)
        dcs = jnp.zeros((ts, LANES), F32)
        for p in range(H // 2):
            cols = slice(p * LANES, (p + 1) * LANES)
            for d_ref, z_ref, g_ref, o_ref, srow, mul in ((dq_ref, zq_ref, gq_ref, dzq_ref, 0, scale),
                                                          (dk_ref, zk_ref, gk_ref, dzk_ref, 1, 1.0)):
                dy = jnp.where(lane < HEAD_DIM, d_ref[0, 2 * p], pltpu.roll(d_ref[0, 2 * p + 1], HEAD_DIM, 1)) * mul
                xv, g = z_ref[:, cols], g_ref[:, cols]
                r = lax.rsqrt(_half_sums(xv * xv) * (1.0 / HEAD_DIM) + EPS)
                dxh = dy * g
                mm = _half_sums(dxh * xv) * (1.0 / HEAD_DIM)
                o_ref[:, cols] = (r * (dxh - xv * (r * r) * mm)).astype(BF16)
                small_ref[8 * srow:8 * srow + 1, cols] += jnp.sum(dy * xv * r, axis=0, keepdims=True)
            dzv_ref[:, cols] = jnp.where(lane < HEAD_DIM, dv_ref[0, 2 * p], pltpu.roll(dv_ref[0, 2 * p + 1], HEAD_DIM, 1)).astype(BF16)
            for e in range(2):
                h = 2 * p + e
                dc = jnp.sum(jnp.where(lane == C0, dq_ref[0, h], 0.0) - jnp.where(lane == K0, dk_ref[0, h], 0.0),
                             axis=-1, keepdims=True)
                dcs = jnp.where(lane == h, dc, dcs)
        ri = lax.broadcasted_iota(jnp.int32, (ts, ts), 0)
        ci = lax.broadcasted_iota(jnp.int32, (ts, ts), 1)
        tri = (ci >= ri).astype(BF16)
        dl = carry[...]
        for piece in _split3(dcs):
            dl = dl + _dot(tri, piece)
        carry[...] = dl[0:1, :]
        xf = zf_ref[...] + bf_ref[...]
        df = jnp.where(lane < H, dl * _sig(-xf), 0.0)
        dzf_ref[...] = df.astype(BF16)
        small_ref[16:17, 0:LANES] += jnp.sum(df, axis=0, keepdims=True)

    rev = lambda b, i: (b, 0, nts - 1 - i, 0)
    head = pl.BlockSpec((1, H, ts, LANES), rev)
    tokrow = lambda b, i: b * nts + nts - 1 - i
    sec = lambda n: pl.BlockSpec((ts, FW), lambda b, i: (tokrow(b, i), n))
    narrow = pl.BlockSpec((ts, LANES), lambda b, i: (tokrow(b, i), 0))
    row = lambda w: pl.BlockSpec((1, w), lambda b, i: (0, 0))
    return _pcall(
        body, name="attn_post_bwd", grid=(B, nts),
        in_specs=[head, head, head, sec(0), sec(1), narrow, row(LANES), row(FW), row(FW)],
        out_specs=[sec(0), sec(0), sec(0), narrow, pl.BlockSpec((24, FW), lambda b, i: (0, 0))],
        out_shape=[jax.ShapeDtypeStruct((B * S, FW), BF16)] * 3 + [jax.ShapeDtypeStruct((B * S, LANES), BF16),
                                                                   jax.ShapeDtypeStruct((24, FW), F32)],
        scratch_shapes=[pltpu.VMEM((1, LANES), F32)],
    )(dq, dk, dv, z, z, zf, bfp, gq, gk)


def _inproj_bwd_x(dzs, dzf, wmt, wft, x, g, dout, *, tm, comm=None):
    T, D = x.shape
    nj = len(dzs)
    tn = dzs[0].shape[1]
    grid = (T // tm, nj)

    def body(*refs):
        ins, (gx_ref, dg_ref), (acc,), rider = _split_refs(refs, nj + 6, 2, comm)
        dz_refs = ins[:nj]
        dzf_ref, wm_ref, wf_ref, x_ref, g_ref, dout_ref = ins[nj:]
        first, last = _grid_ends(grid)
        if comm:
            comm.begin(rider, first)
        i, j = pl.program_id(0), pl.program_id(1)

        @pl.when((i == 0) & (j == 0))
        def _():
            dg_ref[...] = jnp.zeros_like(dg_ref)

        @pl.when(j == 0)
        def _():
            acc[...] = _dot(dzf_ref[...], wf_ref[...])

        for s in range(nj):
            @pl.when(j == s)
            def _():
                acc[...] += _dot(dz_refs[s][...], wm_ref[...])

        @pl.when(j == nj - 1)
        def _():
            dh, xv, g = acc[...], x_ref[...], g_ref[...]
            r = lax.rsqrt(jnp.mean(xv * xv, axis=-1, keepdims=True) + EPS)
            dxh = dh * g
            mm = jnp.mean(dxh * xv, axis=-1, keepdims=True)
            gx_ref[...] = dout_ref[...] + r * (dxh - xv * (r * r) * mm)
            dg_ref[0:1, :] += jnp.sum(dh * xv * r, axis=0, keepdims=True)

        if comm:
            comm.end(rider, last)

    tok = pl.BlockSpec((tm, D), lambda i, j: (i, 0))
    sec = pl.BlockSpec((tm, tn), lambda i, j: (i, 0))
    return _pcall(
        body, name="inproj_bwd_x", grid=grid,
        **_with_comm(
            comm,
            [sec] * nj + [pl.BlockSpec((tm, LANES), lambda i, j: (i, 0)), pl.BlockSpec((tn, D), lambda i, j: (j, 0)),
                          pl.BlockSpec((LANES, D), lambda i, j: (0, 0)), tok, pl.BlockSpec((1, D), lambda i, j: (0, 0)), tok],
            [tok, pl.BlockSpec((8, D), lambda i, j: (0, 0))],
            [jax.ShapeDtypeStruct((T, D), F32), jax.ShapeDtypeStruct((8, D), F32)],
            [pltpu.VMEM((tm, D), F32)]),
    )(*dzs, dzf, wmt, wft, x, g, dout, *(comm.operands() if comm else ()))


def _inproj_bwd_w(h, dz, *, tm, name):
    T, D = h.shape
    W = dz.shape[1]
    ni = T // tm

    def body(h_ref, dz_ref, dw_ref, acc):
        i = pl.program_id(0)

        @pl.when(i == 0)
        def _():
            acc[...] = jnp.zeros_like(acc)

        acc[...] += _dot_tn(dz_ref[...], h_ref[...])

        @pl.when(i == ni - 1)
        def _():
            dw_ref[...] = acc[...].astype(BF16)

    return _pcall(
        body, name=name, grid=(ni,),
        in_specs=[pl.BlockSpec((tm, D), lambda i: (i, 0)), pl.BlockSpec((tm, W), lambda i: (i, 0))],
        out_specs=pl.BlockSpec((W, D), lambda i: (0, 0)), out_shape=jax.ShapeDtypeStruct((W, D), BF16),
        scratch_shapes=[pltpu.VMEM((W, D), F32)],
    )(h, dz)


def _adamw(w, g, m, v):
    m = ADAM_B1 * m + (1.0 - ADAM_B1) * g
    v = ADAM_B2 * v + (1.0 - ADAM_B2) * (g * g)
    m_hat = m / (1.0 - ADAM_B1 ** ADAM_STEP)
    v_hat = v / (1.0 - ADAM_B2 ** ADAM_STEP)
    return -ADAM_LR * (m_hat / (jnp.sqrt(v_hat) + ADAM_EPS) + ADAM_WD * w), m, v


def _adam_update(w, m, v, parts, *, name, tr):
    R, Cc = w.shape[0], w.shape[-1]

    def body(w_ref, m_ref, v_ref, p_ref, g_ref, d_ref, nm_ref, nv_ref):
        g = p_ref[0].astype(F32)
        for k in range(1, N_DEV):
            g = g + p_ref[k].astype(F32)
        g_ref[...] = g
        d_ref[...], nm_ref[...], nv_ref[...] = _adamw(w_ref[...], g, m_ref[...], v_ref[...])

    blk = pl.BlockSpec((tr, Cc), lambda i: (i, 0)) if w.ndim == 2 else pl.BlockSpec((tr, None, Cc), lambda i: (i, 0, 0))
    shape = jax.ShapeDtypeStruct(w.shape, F32)
    return _pcall(
        body, name=name, grid=(pl.cdiv(R, tr),),
        in_specs=[blk, blk, blk, pl.BlockSpec((N_DEV, tr, Cc), lambda i: (0, i, 0))],
        out_specs=[blk, blk, blk, blk], out_shape=[shape, shape, shape, shape],
    )(w, m, v, parts)


MESH = pl.DeviceIdType.MESH
ANY = pl.BlockSpec(memory_space=pl.ANY)


def _flip(v, bit):
    return 1 - v if bit else v


def _allgather(shards):
    n = len(shards)

    def body(*refs):
        srcs, outs = refs[:n], refs[n:2 * n]
        send_sems, recv_sems, local_sems = refs[2 * n:]
        x, y, c = lax.axis_index("x"), lax.axis_index("y"), lax.axis_index("c")
        sibling = (x, y, 1 - c)
        chips = [(1 - x, y), (x, 1 - y), (1 - x, 1 - y)]
        slot = lambda px, py, pc: 4 * px + 2 * py + pc

        def copy(a, k, block, to, src=None):
            dst = outs[a].at[slot(*block)]
            return pltpu.make_async_remote_copy(src_ref=dst if src is None else src, dst_ref=dst, send_sem=send_sems.at[7 * a + k],
                                                recv_sem=recv_sems.at[7 * a + k], device_id=to, device_id_type=MESH)

        mine = [pltpu.make_async_copy(srcs[a], outs[a].at[slot(x, y, c)], local_sems.at[a]) for a in range(n)]
        sends = []
        for a in range(n):
            mine[a].start()
            sends.append(copy(a, 0, (x, y, c), sibling, src=srcs[a]))
            sends += [copy(a, 1 + j, (x, y, c), (*chip, c), src=srcs[a]) for j, chip in enumerate(chips)]
        for cp in sends:
            cp.start()
        for a in range(n):
            for j, chip in enumerate(chips):
                copy(a, 1 + j, (*chip, c), (x, y, c)).wait_recv()
                fwd = copy(a, 4 + j, (*chip, c), sibling)
                fwd.start()
                sends.append(fwd)
        for a in range(n):
            copy(a, 0, (x, y, 1 - c), (x, y, c)).wait_recv()
            for j, chip in enumerate(chips):
                copy(a, 4 + j, (*chip, 1 - c), (x, y, c)).wait_recv()
        for cp in sends:
            cp.wait_send()
        for cp in mine:
            cp.wait()

    return pl.pallas_call(
        body, name="allgather_weights", in_specs=[ANY] * n, out_specs=[ANY] * n,
        out_shape=[jax.ShapeDtypeStruct((N_DEV,) + s.shape, s.dtype) for s in shards],
        scratch_shapes=[pltpu.SemaphoreType.DMA((7 * n,)), pltpu.SemaphoreType.DMA((7 * n,)), pltpu.SemaphoreType.DMA((n,))],
    )(*shards)


class _Comm:
    def __init__(self, arrays, owners, into=None):
        self.arrays, self.owners, self.n = list(arrays), list(owners), len(arrays)
        self.into = list(into) if into else [None] * self.n
        self.carried = [t for t in self.into if t is not None]

    def operands(self):
        return self.arrays + self.carried

    def aliases(self, n_in, n_out):
        out, pos = {}, 0
        for a, t in enumerate(self.into):
            if t is not None:
                out[n_in + self.n + pos] = n_out + a
                pos += 1
        return out

    def out_shape(self):
        return [jax.ShapeDtypeStruct((N_DEV,) + (a.shape if o is None else a.shape[1:]), a.dtype)
                for a, o in zip(self.arrays, self.owners)]

    def scratch(self):
        return [pltpu.SemaphoreType.DMA((7 * self.n,)), pltpu.SemaphoreType.DMA((7 * self.n,)), pltpu.SemaphoreType.DMA((self.n,))]

    def ops(self, srcs, outs, send_sems, recv_sems, local_sems):
        x, y, c = lax.axis_index("x"), lax.axis_index("y"), lax.axis_index("c")
        me = 4 * x + 2 * y + c
        when = lambda p, f: (lambda: pl.when(p)(f))
        starts, waits = [], []
        for a, owners in enumerate(self.owners):
            src, out = srcs[a], outs[a]

            def remote(piece, slot, rel, to, a=a, out=out):
                sem = 7 * a + rel - 1
                return pltpu.make_async_remote_copy(src_ref=piece, dst_ref=out.at[slot], send_sem=send_sems.at[sem],
                                                    recv_sem=recv_sems.at[sem], device_id=to, device_id_type=MESH)

            peers = [(d, (_flip(x, d & 4), _flip(y, d & 2), _flip(c, d & 1))) for d in range(1, N_DEV)]
            if owners is None:
                local = pltpu.make_async_copy(src, out.at[me], local_sems.at[a])
                starts.append(local.start)
                waits.append(local.wait)
                for d, peer in peers:
                    cp = remote(src, me, d, peer)
                    starts.append(cp.start)
                    waits.append(cp.wait_send)
                receives = True
            else:
                for s, dest in enumerate(owners):
                    to = (dest >> 2 & 1, dest >> 1 & 1, dest & 1)
                    rel = 4 * _flip(x, to[0]) + 2 * _flip(y, to[1]) + _flip(c, to[2])
                    cp = remote(src.at[s], me, jnp.maximum(rel, 1), to)
                    local = pltpu.make_async_copy(src.at[s], out.at[me], local_sems.at[a])
                    starts += [when(rel != 0, cp.start), when(rel == 0, local.start)]
                    waits += [when(rel != 0, cp.wait_send), when(rel == 0, local.wait)]
                receives = me == owners[0]
                for dest in owners[1:]:
                    receives = receives | (me == dest)
            piece = src if owners is None else src.at[0]
            for d, peer in peers:
                def arrival(d=d, peer=peer, piece=piece, remote=remote):
                    remote(piece, 4 * peer[0] + 2 * peer[1] + peer[2], d, peer).wait_recv()

                waits.append(arrival if receives is True else when(receives, arrival))
        return starts, waits

    def _phase(self, which, rider, cond):
        srcs, outs, sems = rider

        def go():
            for op in self.ops(srcs, outs, *sems)[which]:
                op()

        go() if cond is True else pl.when(cond)(go)

    def begin(self, rider, first=True):
        self._phase(0, rider, first)

    def end(self, rider, last=True):
        self._phase(1, rider, last)


def _split_refs(refs, n_in, n_out, comm):
    k = comm.n if comm else 0
    o0 = n_in + k + (len(comm.carried) if comm else 0)
    ins, c_src = refs[:n_in], refs[n_in:n_in + k]
    outs, c_out = refs[o0:o0 + n_out], refs[o0 + n_out:o0 + n_out + k]
    rest = refs[o0 + n_out + k:]
    scratch, c_sem = (rest[:-3], rest[-3:]) if comm else (rest, ())
    return ins, outs, scratch, (c_src, c_out, c_sem)


def _grid_ends(grid):
    first = functools.reduce(jnp.logical_and, [pl.program_id(a) == 0 for a in range(len(grid))])
    last = functools.reduce(jnp.logical_and, [pl.program_id(a) == g - 1 for a, g in enumerate(grid)])
    return first, last


def _with_comm(comm, in_specs, out_specs, out_shape, scratch):
    if comm is None:
        return dict(in_specs=in_specs, out_specs=out_specs, out_shape=out_shape, scratch_shapes=scratch)
    return dict(in_specs=in_specs + [ANY] * len(comm.operands()), out_specs=out_specs + [ANY] * comm.n,
                out_shape=out_shape + comm.out_shape(), scratch_shapes=scratch + comm.scratch(),
                input_output_aliases=comm.aliases(len(in_specs), len(out_specs)))


def _exchange(comm, *, name):
    n = comm.n

    def body(*refs):
        rider = (refs[:n], refs[n:2 * n], refs[2 * n:])
        comm.begin(rider)
        comm.end(rider)

    return pl.pallas_call(body, name=name, in_specs=[ANY] * n, out_specs=[ANY] * n, out_shape=comm.out_shape(),
                          scratch_shapes=comm.scratch())(*comm.arrays)


def kernel(x, norm_g, w_in, b_forget, q_norm_g, k_norm_g, conv_w, conv_b, conv_ln_g, conv_ln_b, w_out, loss_target, m_norm_g, m_w_in, m_b_forget, m_q_norm_g, m_k_norm_g, m_conv_w, m_conv_b, m_conv_ln_g, m_conv_ln_b, m_w_out, v_norm_g, v_w_in, v_b_forget, v_q_norm_g, v_k_norm_g, v_conv_w, v_conv_b, v_conv_ln_g, v_conv_ln_b, v_w_out):
    B, S, D = x.shape
    H = q_norm_g.shape[1]
    FW = H * HEAD_DIM
    assert q_norm_g.shape[2] == HEAD_DIM and FW == D and conv_b.shape[1] == D and H <= LANES // 3
    ncol = w_in.shape[2]
    T = B * S

    to_t = lambda t: jnp.transpose(t, (2, 0, 1))
    from_t = lambda t: jnp.transpose(t, (1, 2, 0))
    wt, mt, vt = to_t(w_in), to_t(m_w_in), to_t(v_w_in)

    x2, tgt = x.reshape(T, D), loss_target.reshape(T, D)
    tm, ts, rc = min(512, T), min(256, S), min(64, S)
    everyone = list(range(N_DEV))
    bfp = jnp.pad(b_forget, ((0, 0), (0, LANES - H)))
    gq, gk = q_norm_g.reshape(1, FW), k_norm_g.reshape(1, FW)

    (wg,) = _allgather([wt.reshape(ncol, D).astype(BF16)])
    wmt, wft = _w_relayout(wg, D=D, H=H)
    z, zf, h, wog, cwg = _inproj_fwd(x2, norm_g, wmt, wft, tm=min(1024, T), tn=FW,
                                     comm=_Comm([w_out[0].astype(BF16), conv_w[0]], [None, None]))
    wo = wog.reshape(N_DEV * wog.shape[1], D)
    cw = jnp.transpose(cwg, (1, 0, 2)).reshape(CONV_K, D)
    qa, ka, va = _attn_prep(z, zf, bfp, gq, gk, B=B, S=S, H=H, ts=ts)
    a, qb = _attn_fwd(qa, ka, va, t=min(512, S), blk=ts, G=4)
    uc = _conv_fwd(z, cw, conv_b, B=B, S=S, sec_w=FW, rc=rc)
    uo = _conv_post(uc, z, conv_ln_g, conv_ln_b, tm=ts, sec_w=FW)
    dout, dy, dwo, loss = _outproj(a, z, uo, x2, tgt, wo, tm=ts, sec_w=FW)

    dgc, duc, small_c = _conv_post_bwd(dy, uc, z, conv_ln_g, conv_ln_b, tm=ts, sec_w=FW)
    dga, dgb, dcw = _conv_bwd(duc, z, cw, B=B, S=S, sec_w=FW, rc=rc)
    dgf, doa = _attn_bwd_prep(dy, a, z, B=B, S=S, H=H, ts=ts, sec_w=FW)
    _, f1, _ = _section_rows(D, H)
    late_rows = [k for k in everyone if k * ncol >= f1]
    early_rows = [k for k in everyone if k not in late_rows]
    first_late_sec = (late_rows[0] * ncol - H) // FW
    dws = {s: _inproj_bwd_w(h, dz, tm=tm, name=f"inproj_bwd_w{s}") for s, dz in ((3, dgf), (4, dga), (5, dgb), (6, dgc))}
    dw_late = _dw_relayout({s: dws[s] for s in range(first_late_sec, 7)}, None, R=ncol, H=H, shards=late_rows, name="dw_relayout_late")
    dwo_parts = dwo.reshape(N_DEV, dwo.shape[0] // N_DEV, D).astype(BF16)
    dcw_parts = jnp.transpose(dcw.reshape(CONV_K, N_DEV, D // N_DEV), (1, 0, 2))
    dq, dk, dv, r_w, r_wo, r_cw = _attn_bwd(qb, ka, va, doa, t=min(512, S), blk=ts, G=4,
                                            comm=_Comm([dw_late, dwo_parts, dcw_parts], [late_rows, everyone, everyone]))
    dzq, dzk, dzv, dzf, small_a = _attn_post_bwd(dq, dk, dv, z, zf, bfp, gq, gk, B=B, S=S, H=H, ts=ts)
    dws.update({s: _inproj_bwd_w(h, dz, tm=tm, name=f"inproj_bwd_w{s}") for s, dz in ((0, dzq), (1, dzk), (2, dzv))})
    dwf = _inproj_bwd_w(h, dzf, tm=tm, name="inproj_bwd_wf")
    last_early_sec = (early_rows[-1] * ncol + ncol - 1 - H) // FW
    dw_early = _dw_relayout({s: dws[s] for s in range(last_early_sec + 1)}, dwf, R=ncol, H=H, shards=early_rows, name="dw_relayout_early")
    gx, dg, r_w = _inproj_bwd_x([dzq, dzk, dzv, dgf, dga, dgb, dgc], dzf, wmt, wft, x2, norm_g, dout, tm=tm,
                                comm=_Comm([dw_early], [early_rows], into=[r_w]))
    small = jnp.concatenate([dg, small_a[16:24], small_a[0:8], small_a[8:16], small_c[16:24], small_c[0:8], small_c[8:16],
                             jnp.pad(loss, ((0, 0), (0, D - LANES)))], axis=0)
    (r_small,) = _exchange(_Comm([small], [None]), name="exchange_small")

    g_w, d_w, nm_w, nv_w = [from_t(t) for t in _adam_update(wt, mt, vt, r_w, name="adam_w_in", tr=min(128, D))]
    g_wo, d_wo, nm_wo, nv_wo = [t[None] for t in _adam_update(w_out[0], m_w_out[0], v_w_out[0], r_wo, name="adam_w_out",
                                                               tr=min(128, w_out.shape[1]))]
    g_cw, d_cw, nm_cw, nv_cw = [t[None] for t in _adam_update(conv_w[0], m_conv_w[0], v_conv_w[0], r_cw, name="adam_conv_w", tr=CONV_K)]
    tile = lambda t: jnp.pad(t.reshape(1, -1), ((0, 7), (0, D - t.size)))
    pack = lambda *ps: jnp.concatenate([tile(p) for p in ps] + [jnp.zeros((8, D), F32)], axis=0)
    small_w = pack(norm_g, b_forget, q_norm_g, k_norm_g, conv_b, conv_ln_g, conv_ln_b)
    small_m = pack(m_norm_g, m_b_forget, m_q_norm_g, m_k_norm_g, m_conv_b, m_conv_ln_g, m_conv_ln_b)
    small_v = pack(v_norm_g, v_b_forget, v_q_norm_g, v_k_norm_g, v_conv_b, v_conv_ln_g, v_conv_ln_b)
    g_s, d_s, nm_s, nv_s = _adam_update(small_w, small_m, small_v, r_small, name="adam_small", tr=small_w.shape[0])

    def leaves(small_t, w_t, cw_t, wo_t):
        row = lambda r, like: small_t[8 * r:8 * r + 1, :like.size].reshape(like.shape)
        return [row(0, norm_g), w_t, row(1, b_forget), row(2, q_norm_g), row(3, k_norm_g), cw_t, row(4, conv_b),
                row(5, conv_ln_g), row(6, conv_ln_b), wo_t]

    return (g_s[56, 0], gx.reshape(B, S, D), *leaves(g_s, g_w, g_cw, g_wo), *leaves(d_s, d_w, d_cw, d_wo),
            *leaves(nm_s, nm_w, nm_cw, nm_wo), *leaves(nv_s, nv_w, nv_cw, nv_wo))
```

```python
import functools

import jax
import jax.numpy as jnp
from jax import lax
from jax.experimental import pallas as pl
from jax.experimental.pallas import tpu as pltpu

F32, BF16 = jnp.float32, jnp.bfloat16
EPS = 1e-6
NEG_INF = -1e30
CONV_K = 31
HEAD_DIM = 64
ADAM_LR, ADAM_B1, ADAM_B2, ADAM_EPS, ADAM_WD, ADAM_STEP = 0.001, 0.9, 0.999, 1e-08, 0.01, 10

LANES = 128
HALO = 32
N_DEV = 8
VMEM_LIMIT = 56 * 1024 * 1024

C0, K0, L0 = HEAD_DIM, HEAD_DIM + 3, HEAD_DIM + 6


def _pcall(body, *, name, vmem=VMEM_LIMIT, **kw):
    return pl.pallas_call(body, name=name, compiler_params=pltpu.CompilerParams(vmem_limit_bytes=vmem), **kw)


def _dot(a, b):
    return jnp.dot(a, b, preferred_element_type=F32)


def _dot_nt(a, b):
    return lax.dot_general(a, b, (((1,), (1,)), ((), ())), preferred_element_type=F32)


def _dot_tn(a, b):
    return lax.dot_general(a, b, (((0,), (0,)), ((), ())), preferred_element_type=F32)


def _split3(x):
    a = x.astype(BF16)
    r = x - a.astype(F32)
    b = r.astype(BF16)
    c = (r - b.astype(F32)).astype(BF16)
    return a, b, c


def _sig(x):
    return jax.nn.sigmoid(x)


def _dsilu(x, s):
    return s * (1.0 + x * (1.0 - s))


def _lane_iota(rows):
    return lax.broadcasted_iota(jnp.int32, (rows, LANES), 1)


def _half_sums(v):
    r = lax.broadcasted_iota(jnp.int32, (LANES, LANES), 0) < HEAD_DIM
    c = lax.broadcasted_iota(jnp.int32, (LANES, LANES), 1) < HEAD_DIM
    ones = (r == c).astype(BF16)
    a, b, d = _split3(v)
    return _dot(a, ones) + _dot(b, ones) + _dot(d, ones)


def _head_to_low(vb, e):
    r = lax.broadcasted_iota(jnp.int32, (LANES, LANES), 0)
    c = lax.broadcasted_iota(jnp.int32, (LANES, LANES), 1)
    return _dot(vb, ((r == c + e * HEAD_DIM) & (c < HEAD_DIM)).astype(BF16))


def _head_to_low_roll(v, e, lane):
    if e:
        v = pltpu.roll(v, HEAD_DIM, 1)
    return jnp.where(lane < HEAD_DIM, v, 0.0)


def _put3(base, lane, first, pieces):
    for n, p in enumerate(pieces):
        base = jnp.where(lane == first + n, p.astype(F32), base)
    return base


def _section_rows(D, H):
    FW = H * HEAD_DIM
    return 3 * FW, 3 * FW + H, 7 * FW + H


def _row_pieces(a, b, R, step):
    out = []
    while a < b:
        k = a // R
        lo = a - k * R
        hi = min(R, lo + min(step, b - a))
        out.append((k, lo, hi))
        a += hi - lo
    return out


def _w_relayout(wg, *, D, H):
    R = wg.shape[1]
    FW = H * HEAD_DIM
    f0, f1, end = _section_rows(D, H)

    def body(wg_ref, wm_ref, wf_ref):
        def put(dst_ref, d0, a, b):
            for k, lo, hi in _row_pieces(a, b, R, 256):
                dst_ref[d0:d0 + hi - lo, :] = wg_ref[k, lo:hi, :]
                d0 += hi - lo

        put(wm_ref, 0, 0, f0)
        put(wm_ref, f0, f1, end)
        wf_ref[...] = jnp.zeros_like(wf_ref)
        put(wf_ref, 0, f0, f1)

    return _pcall(body, name="w_relayout",
                  out_shape=[jax.ShapeDtypeStruct((7 * FW, D), BF16), jax.ShapeDtypeStruct((LANES, D), BF16)])(wg)


def _dw_relayout(secs, dwf, *, R, H, shards, name):
    FW, D = next(iter(secs.values())).shape
    f0, f1, end = _section_rows(D, H)
    order = sorted(secs)

    def body(*refs):
        sec_refs = dict(zip(order, refs))
        dwf_ref, out_ref = (refs[-2] if dwf is not None else None), refs[-1]

        def src(g0, g1):
            if f0 <= g0 < f1:
                return dwf_ref[g0 - f0:g1 - f0, :]
            s0 = g0 if g0 < f0 else g0 - H
            return sec_refs[s0 // FW][s0 % FW:s0 % FW + (g1 - g0), :]

        cuts = sorted({0, end, f0, f1} | {s * FW for s in range(4)} | {f1 + s * FW for s in range(5)})
        for slot, k in enumerate(shards):
            g = k * R
            while g < (k + 1) * R:
                nxt = min(min(c for c in cuts if c > g), (k + 1) * R, g + 256)
                out_ref[slot, g - k * R:nxt - k * R, :] = src(g, nxt)
                g = nxt

    args = [secs[s] for s in order] + ([dwf] if dwf is not None else [])
    return _pcall(body, name=name, out_shape=jax.ShapeDtypeStruct((len(shards), R, D), BF16))(*args)


def _inproj_fwd(x, g, wmt, wft, *, tm, tn, comm=None):
    T, D = x.shape
    nj = wmt.shape[0] // tn
    grid = (T // tm, nj)

    def body(*refs):
        (x_ref, g_ref, wm_ref, wf_ref), (z_ref, zf_ref, h_ref), (h_scr,), rider = _split_refs(refs, 4, 3, comm)
        first, last = _grid_ends(grid)
        if comm:
            comm.begin(rider, first)

        @pl.when(pl.program_id(1) == 0)
        def _():
            xv = x_ref[...]
            r = lax.rsqrt(jnp.mean(xv * xv, axis=-1, keepdims=True) + EPS)
            h = ((xv * r) * g_ref[...]).astype(BF16)
            h_scr[...] = h
            h_ref[...] = h
            zf_ref[...] = _dot_nt(h, wf_ref[...])

        z_ref[...] = _dot_nt(h_scr[...], wm_ref[...])
        if comm:
            comm.end(rider, last)

    return _pcall(
        body, name="inproj_fwd", grid=grid,
        **_with_comm(
            comm,
            [pl.BlockSpec((tm, D), lambda i, j: (i, 0)), pl.BlockSpec((1, D), lambda i, j: (0, 0)),
             pl.BlockSpec((tn, D), lambda i, j: (j, 0)), pl.BlockSpec((LANES, D), lambda i, j: (0, 0))],
            [pl.BlockSpec((tm, tn), lambda i, j: (i, j)), pl.BlockSpec((tm, LANES), lambda i, j: (i, 0)),
             pl.BlockSpec((tm, D), lambda i, j: (i, 0))],
            [jax.ShapeDtypeStruct((T, wmt.shape[0]), F32), jax.ShapeDtypeStruct((T, LANES), F32), jax.ShapeDtypeStruct((T, D), BF16)],
            [pltpu.VMEM((tm, D), BF16)]),
    )(x, g, wmt, wft, *(comm.operands() if comm else ()))


def _attn_prep(z, zf, bfp, gq, gk, *, B, S, H, ts):
    FW = H * HEAD_DIM
    nts = S // ts
    scale = HEAD_DIM ** -0.5

    def body(zq_ref, zk_ref, zv_ref, zf_ref, bf_ref, gq_ref, gk_ref, qa_ref, ka_ref, va_ref, carry):
        @pl.when(pl.program_id(1) == 0)
        def _():
            carry[...] = jnp.zeros_like(carry)

        lane = _lane_iota(ts)
        xf = zf_ref[...] + bf_ref[...]
        logf = jnp.minimum(xf, 0.0) - jnp.log1p(jnp.exp(-jnp.abs(xf)))
        logf = jnp.where(lane < H, logf, 0.0)
        ri = lax.broadcasted_iota(jnp.int32, (ts, ts), 0)
        ci = lax.broadcasted_iota(jnp.int32, (ts, ts), 1)
        tri = (ri >= ci).astype(BF16)
        c = carry[...]
        for piece in _split3(logf):
            c = c + _dot(tri, piece)
        carry[...] = c[ts - 1:ts, :]
        c1, c2, c3 = _split3(c)
        cpk = c1.astype(F32) + pltpu.roll(c2.astype(F32), H, 1) + pltpu.roll(c3.astype(F32), 2 * H, 1)
        cq, ck = cpk.astype(BF16), (-cpk).astype(BF16)
        prow = lax.broadcasted_iota(jnp.int32, (LANES, LANES), 0)
        pcol = lax.broadcasted_iota(jnp.int32, (LANES, LANES), 1)
        ones_q = jnp.where((lane >= K0) & (lane < K0 + 3), 1.0, 0.0)
        ones_k = jnp.where(((lane >= C0) & (lane < C0 + 3)) | ((lane >= L0) & (lane < L0 + 3)), 1.0, 0.0)
        ones_v = jnp.where((lane >= C0) & (lane < C0 + 3), 1.0, 0.0)
        for h in range(H):
            p, e = divmod(h, 2)
            cols = slice(p * LANES, (p + 1) * LANES)
            piece = jnp.where(prow == h, 0, jnp.where(prow == H + h, 1, jnp.where(prow == 2 * H + h, 2, -LANES)))
            aug_q = _dot(cq, (pcol == C0 + piece).astype(BF16)) + ones_q
            aug_k = _dot(ck, (pcol == K0 + piece).astype(BF16)) + ones_k
            for z_ref, g_ref, o_ref, aug, mul in ((zq_ref, gq_ref, qa_ref, aug_q, scale), (zk_ref, gk_ref, ka_ref, aug_k, 1.0)):
                blk = z_ref[:, cols]
                r = lax.rsqrt(_half_sums(blk * blk) * (1.0 / HEAD_DIM) + EPS)
                y = ((blk * r) * g_ref[:, cols]).astype(BF16).astype(F32) * mul
                o_ref[0, h] = (_head_to_low_roll(y, e, lane) + aug).astype(BF16)
            va_ref[0, h] = (_head_to_low_roll(zv_ref[:, cols], e, lane) + ones_v).astype(BF16)

    sec = lambda n: pl.BlockSpec((ts, FW), lambda b, i: (b * nts + i, n))
    row = lambda w: pl.BlockSpec((1, w), lambda b, i: (0, 0))
    head = pl.BlockSpec((1, H, ts, LANES), lambda b, i: (b, 0, i, 0))
    hshape = jax.ShapeDtypeStruct((B, H, S, LANES), BF16)
    return _pcall(
        body, name="attn_prep", grid=(B, nts),
        in_specs=[sec(0), sec(1), sec(2), pl.BlockSpec((ts, LANES), lambda b, i: (b * nts + i, 0)), row(LANES), row(FW), row(FW)],
        out_specs=[head, head, head], out_shape=[hshape, hshape, hshape],
        scratch_shapes=[pltpu.VMEM((1, LANES), F32)],
    )(z, z, z, zf, bfp, gq, gk)


def _attn_fwd(qa, ka, va, *, t, blk, G):
    B, H, S, _ = qa.shape
    n = S // t
    assert t % blk == 0

    def body(q_ref, k_ref, v_ref, a_ref, qb_ref):
        i = pl.program_id(2)
        lane = _lane_iota(t)
        row = lax.broadcasted_iota(jnp.int32, (t, blk), 0)
        col = lax.broadcasted_iota(jnp.int32, (t, blk), 1)
        qs = [q_ref[0, e] for e in range(G)]

        def step(start, carry, shift=None):
            rows = pl.ds(pl.multiple_of(start, blk), blk)
            new = []
            for e in range(G):
                m, acc = carry[e]
                s = _dot_nt(qs[e], k_ref[0, e, rows, :])
                if shift is not None:
                    s = jnp.where(col + shift <= row, s, NEG_INF)
                m_new = jnp.maximum(m, jnp.max(s, axis=-1, keepdims=True))
                p = jnp.exp(s - m_new)
                new.append((m_new, jnp.exp(m - m_new) * acc + _dot(p.astype(BF16), v_ref[0, e, rows, :])))
            return tuple(new)

        carry = tuple((jnp.full((t, 1), NEG_INF, F32), jnp.zeros((t, LANES), F32)) for _ in range(G))
        carry = lax.fori_loop(0, i * (t // blk), lambda j, c: step(j * blk, c), carry)
        for d in range(t // blk):
            carry = step(i * t + d * blk, carry, shift=d * blk)
        outs = []
        for e in range(G):
            m, acc = carry[e]
            l = jnp.sum(jnp.where(lane == C0, acc, 0.0), axis=-1, keepdims=True)
            outs.append(acc / l)
            qb_ref[0, e] = _put3(qs[e].astype(F32), lane, L0, _split3(-(m + jnp.log(l)))).astype(BF16)
        for pp in range(G // 2):
            a_ref[:, pp * LANES:(pp + 1) * LANES] = jnp.where(lane < HEAD_DIM, outs[2 * pp], pltpu.roll(outs[2 * pp + 1], HEAD_DIM, 1))

    return _pcall(
        body, name="attn_fwd", grid=(B, H // G, n),
        in_specs=[pl.BlockSpec((1, G, t, LANES), lambda b, p, i: (b, p, i, 0)),
                  pl.BlockSpec((1, G, S, LANES), lambda b, p, i: (b, p, 0, 0)),
                  pl.BlockSpec((1, G, S, LANES), lambda b, p, i: (b, p, 0, 0))],
        out_specs=[pl.BlockSpec((t, G * HEAD_DIM), lambda b, p, i: (b * n + i, p)),
                   pl.BlockSpec((1, G, t, LANES), lambda b, p, i: (b, p, i, 0))],
        out_shape=[jax.ShapeDtypeStruct((B * S, H * HEAD_DIM), F32), jax.ShapeDtypeStruct((B, H, S, LANES), BF16)],
    )(qa, ka, va)


def _conv_fwd(z, cw, cb, *, B, S, sec_w, rc):
    C = cw.shape[1]
    nct = C // LANES
    off_a, off_b = 4 * sec_w // LANES, 5 * sec_w // LANES

    def body(ga_ref, gb_ref, w_ref, b_ref, uc_ref, buf):
        buf[0:HALO, :] = jnp.zeros((HALO, LANES), F32)
        buf[HALO:HALO + S, :] = ga_ref[...] * _sig(gb_ref[...])

        def chunk(cidx, carry):
            r0 = pl.multiple_of(cidx * rc, rc)
            acc = jnp.zeros((rc, LANES), F32)
            for j in range(CONV_K):
                acc = acc + w_ref[j:j + 1, :] * buf[pl.ds(r0 + (HALO - CONV_K + 1 + j), rc), :]
            uc_ref[pl.ds(r0, rc), :] = acc + b_ref[...]
            return carry

        lax.fori_loop(0, S // rc, chunk, 0)

    return _pcall(
        body, name="conv_fwd", grid=(B, nct),
        in_specs=[pl.BlockSpec((S, LANES), lambda b, c: (b, off_a + c)), pl.BlockSpec((S, LANES), lambda b, c: (b, off_b + c)),
                  pl.BlockSpec((CONV_K, LANES), lambda b, c: (0, c)), pl.BlockSpec((1, LANES), lambda b, c: (0, c))],
        out_specs=pl.BlockSpec((S, LANES), lambda b, c: (b, c)),
        out_shape=jax.ShapeDtypeStruct((B * S, C), F32),
        scratch_shapes=[pltpu.VMEM((HALO + S, LANES), F32)],
    )(z, z, cw, cb)


def _ln_parts(uc, lg, lb):
    mu = jnp.mean(uc, axis=-1, keepdims=True)
    d = uc - mu
    rs = lax.rsqrt(jnp.mean(d * d, axis=-1, keepdims=True) + EPS)
    un = d * rs
    return rs, un, un * lg + lb


def _conv_post(uc, z, lg, lb, *, tm, sec_w):
    T, C = uc.shape
    off_c = 6 * sec_w // C

    def body(uc_ref, gc_ref, lg_ref, lb_ref, uo_ref):
        _, _, ul = _ln_parts(uc_ref[...], lg_ref[...], lb_ref[...])
        gc = gc_ref[...]
        uo_ref[...] = ((ul * _sig(ul)) * (gc * _sig(gc))).astype(BF16)

    return _pcall(
        body, name="conv_post", grid=(T // tm,),
        in_specs=[pl.BlockSpec((tm, C), lambda i: (i, 0)), pl.BlockSpec((tm, C), lambda i: (i, off_c)),
                  pl.BlockSpec((1, C), lambda i: (0, 0)), pl.BlockSpec((1, C), lambda i: (0, 0))],
        out_specs=pl.BlockSpec((tm, C), lambda i: (i, 0)), out_shape=jax.ShapeDtypeStruct((T, C), BF16),
    )(uc, z, lg, lb)


def _outproj(a, z, uo, x, tgt, wo, *, tm, sec_w):
    T, D = x.shape
    FW = a.shape[1]
    C = uo.shape[1]

    def body(a_ref, gf_ref, uo_ref, x_ref, t_ref, wo_ref, dout_ref, dy_ref, dwo_ref, loss_ref):
        @pl.when(pl.program_id(0) == 0)
        def _():
            dwo_ref[...] = jnp.zeros_like(dwo_ref)
            loss_ref[...] = jnp.zeros_like(loss_ref)

        gf = gf_ref[...]
        ao = (a_ref[...] * (gf * _sig(gf))).astype(BF16)
        uo = uo_ref[...]
        out = x_ref[...] + (_dot(ao, wo_ref[0:FW, :]) + _dot(uo, wo_ref[FW:FW + C, :]))
        err = out - t_ref[...]
        loss_ref[...] += jnp.sum(err * err) * (0.5 / D)
        dout = err * (1.0 / D)
        dout_ref[...] = dout
        db = dout.astype(BF16)
        dy_ref[:, 0:FW] = _dot_nt(db, wo_ref[0:FW, :])
        dy_ref[:, FW:FW + C] = _dot_nt(db, wo_ref[FW:FW + C, :])
        dwo_ref[0:FW, :] += _dot_tn(ao, db)
        dwo_ref[FW:FW + C, :] += _dot_tn(uo, db)

    tok = lambda w, n=0: pl.BlockSpec((tm, w), lambda i: (i, n))
    return _pcall(
        body, name="outproj", grid=(T // tm,),
        in_specs=[tok(FW), tok(sec_w, 3), tok(C), tok(D), tok(D), pl.BlockSpec((FW + C, D), lambda i: (0, 0))],
        out_specs=[tok(D), tok(FW + C), pl.BlockSpec((FW + C, D), lambda i: (0, 0)), pl.BlockSpec((8, LANES), lambda i: (0, 0))],
        out_shape=[jax.ShapeDtypeStruct((T, D), F32), jax.ShapeDtypeStruct((T, FW + C), F32),
                   jax.ShapeDtypeStruct((FW + C, D), F32), jax.ShapeDtypeStruct((8, LANES), F32)],
    )(a, z, uo, x, tgt, wo)


def _conv_post_bwd(dy, uc, z, lg, lb, *, tm, sec_w):
    T, C = uc.shape
    FW = dy.shape[1] - C
    off_c = 6 * sec_w // C

    def body(dy_ref, uc_ref, gc_ref, lg_ref, lb_ref, dgc_ref, duc_ref, small_ref):
        @pl.when(pl.program_id(0) == 0)
        def _():
            small_ref[...] = jnp.zeros_like(small_ref)

        lg = lg_ref[...]
        rs, un, ul = _ln_parts(uc_ref[...], lg, lb_ref[...])
        s_ul = _sig(ul)
        gc = gc_ref[...]
        s_gc = _sig(gc)
        duo = dy_ref[...]
        dgc_ref[...] = (duo * (ul * s_ul) * _dsilu(gc, s_gc)).astype(BF16)
        dul = duo * (gc * s_gc) * _dsilu(ul, s_ul)
        dun = dul * lg
        duc = rs * (dun - jnp.mean(dun, axis=-1, keepdims=True) - un * jnp.mean(dun * un, axis=-1, keepdims=True))
        duc_ref[...] = duc
        small_ref[0:1, :] += jnp.sum(dul * un, axis=0, keepdims=True)
        small_ref[8:9, :] += jnp.sum(dul, axis=0, keepdims=True)
        small_ref[16:17, :] += jnp.sum(duc, axis=0, keepdims=True)

    tok = lambda n: pl.BlockSpec((tm, C), lambda i: (i, n))
    row = pl.BlockSpec((1, C), lambda i: (0, 0))
    return _pcall(
        body, name="conv_post_bwd", grid=(T // tm,),
        in_specs=[tok(FW // C), tok(0), tok(off_c), row, row],
        out_specs=[tok(0), tok(0), pl.BlockSpec((24, C), lambda i: (0, 0))],
        out_shape=[jax.ShapeDtypeStruct((T, C), BF16), jax.ShapeDtypeStruct((T, C), F32), jax.ShapeDtypeStruct((24, C), F32)],
    )(dy, uc, z, lg, lb)


def _conv_bwd(duc, z, cw, *, B, S, sec_w, rc):
    C = cw.shape[1]
    nct = C // LANES
    off_a, off_b = 4 * sec_w // LANES, 5 * sec_w // LANES

    def body(duc_ref, ga_ref, gb_ref, w_ref, dga_ref, dgb_ref, dw_ref, ubuf, dbuf):
        @pl.when(pl.program_id(1) == 0)
        def _():
            dw_ref[...] = jnp.zeros_like(dw_ref)

        ubuf[0:HALO, :] = jnp.zeros((HALO, LANES), F32)
        ubuf[HALO:HALO + S, :] = ga_ref[...] * _sig(gb_ref[...])
        dbuf[0:S, :] = duc_ref[...]
        dbuf[S:S + HALO, :] = jnp.zeros((HALO, LANES), F32)

        def chunk(cidx, carry):
            r0 = pl.multiple_of(cidx * rc, rc)
            acc = jnp.zeros((rc, LANES), F32)
            for j in range(CONV_K):
                acc = acc + w_ref[j:j + 1, :] * dbuf[pl.ds(r0 + (CONV_K - 1 - j), rc), :]
            rows = pl.ds(r0, rc)
            ga, sg = ga_ref[rows, :], _sig(gb_ref[rows, :])
            dga_ref[rows, :] = (acc * sg).astype(BF16)
            dgb_ref[rows, :] = (acc * ga * sg * (1.0 - sg)).astype(BF16)
            return carry

        lax.fori_loop(0, S // rc, chunk, 0)
        d = duc_ref[...]
        for j in range(CONV_K):
            dw_ref[j:j + 1, :] += jnp.sum(d * ubuf[pl.ds(HALO - CONV_K + 1 + j, S), :], axis=0, keepdims=True)

    return _pcall(
        body, name="conv_bwd", grid=(nct, B),
        in_specs=[pl.BlockSpec((S, LANES), lambda c, b: (b, c)), pl.BlockSpec((S, LANES), lambda c, b: (b, off_a + c)),
                  pl.BlockSpec((S, LANES), lambda c, b: (b, off_b + c)), pl.BlockSpec((CONV_K, LANES), lambda c, b: (0, c))],
        out_specs=[pl.BlockSpec((S, LANES), lambda c, b: (b, c)), pl.BlockSpec((S, LANES), lambda c, b: (b, c)),
                   pl.BlockSpec((CONV_K, LANES), lambda c, b: (0, c))],
        out_shape=[jax.ShapeDtypeStruct((B * S, C), BF16), jax.ShapeDtypeStruct((B * S, C), BF16),
                   jax.ShapeDtypeStruct((CONV_K, C), F32)],
        scratch_shapes=[pltpu.VMEM((HALO + S, LANES), F32), pltpu.VMEM((S + HALO, LANES), F32)],
    )(duc, z, z, cw)


def _attn_bwd_prep(dy, a, z, *, B, S, H, ts, sec_w):
    FW = H * HEAD_DIM
    nts = S // ts

    def body(dy_ref, a_ref, gf_ref, dgf_ref, do_ref):
        lane = _lane_iota(ts)
        for p in range(H // 2):
            cols = slice(p * LANES, (p + 1) * LANES)
            gf, av, dya = gf_ref[:, cols], a_ref[:, cols], dy_ref[:, cols]
            sg = _sig(gf)
            dgf_ref[:, cols] = (dya * av * _dsilu(gf, sg)).astype(BF16)
            da = dya * (gf * sg)
            nd = -_half_sums(da * av)
            dab = da.astype(BF16)
            for e in range(2):
                nd_e = nd[:, e * HEAD_DIM:e * HEAD_DIM + 1]
                do_ref[0, 2 * p + e] = _put3(_head_to_low(dab, e), lane, C0, _split3(nd_e)).astype(BF16)

    tok = lambda n: pl.BlockSpec((ts, FW), lambda b, i: (b * nts + i, n))
    return _pcall(
        body, name="attn_bwd_prep", grid=(B, nts),
        in_specs=[tok(0), tok(0), pl.BlockSpec((ts, sec_w), lambda b, i: (b * nts + i, 3))],
        out_specs=[tok(0), pl.BlockSpec((1, H, ts, LANES), lambda b, i: (b, 0, i, 0))],
        out_shape=[jax.ShapeDtypeStruct((B * S, FW), BF16), jax.ShapeDtypeStruct((B, H, S, LANES), BF16)],
    )(dy, a, z)


def _attn_bwd(qb, ka, va, doa, *, t, blk, G, comm=None):
    B, H, S, _ = qb.shape
    n = S // t
    assert t % blk == 0
    grid = (B, H // G, n)

    def body(*refs):
        (q_ref, k_ref, v_ref, do_ref), (dq_ref, dk_ref, dv_ref), _, rider = _split_refs(refs, 4, 3, comm)
        first, last = _grid_ends(grid)
        if comm:
            comm.begin(rider, first)
        j = pl.program_id(2)

        @pl.when(j == 0)
        def _():
            dq_ref[...] = jnp.zeros_like(dq_ref)

        row = lax.broadcasted_iota(jnp.int32, (t, blk), 0)
        col = lax.broadcasted_iota(jnp.int32, (t, blk), 1)
        ks = [k_ref[0, e] for e in range(G)]
        vs = [v_ref[0, e] for e in range(G)]

        def step(start, carry, shift=None):
            rows = pl.ds(pl.multiple_of(start, blk), blk)
            new = []
            for e in range(G):
                dk, dv = carry[e]
                q, do = q_ref[0, e, rows, :], do_ref[0, e, rows, :]
                st = _dot_nt(ks[e], q)
                dpt = _dot_nt(vs[e], do)
                if shift is not None:
                    st = jnp.where(row <= col + shift, st, NEG_INF)
                pt = jnp.exp(st)
                dsb = (pt * dpt).astype(BF16)
                dq_ref[0, e, rows, :] += _dot_tn(dsb, ks[e])
                new.append((dk + _dot(dsb, q), dv + _dot(pt.astype(BF16), do)))
            return tuple(new)

        zero = jnp.zeros((t, LANES), F32)
        carry = tuple((zero, zero) for _ in range(G))
        for d in range(t // blk):
            carry = step(j * t + d * blk, carry, shift=d * blk)
        carry = lax.fori_loop((j + 1) * (t // blk), S // blk, lambda i, c: step(i * blk, c), carry)
        for e in range(G):
            dk_ref[0, e], dv_ref[0, e] = carry[e]
        if comm:
            comm.end(rider, last)

    full = pl.BlockSpec((1, G, S, LANES), lambda b, h, j: (b, h, 0, 0))
    tile = pl.BlockSpec((1, G, t, LANES), lambda b, h, j: (b, h, j, 0))
    shape = jax.ShapeDtypeStruct((B, H, S, LANES), F32)
    return _pcall(
        body, name="attn_bwd", grid=grid,
        **_with_comm(comm, [full, tile, tile, full], [full, tile, tile], [shape, shape, shape], []),
    )(qb, ka, va, doa, *(comm.operands() if comm else ()))


def _attn_post_bwd(dq, dk, dv, z, zf, bfp, gq, gk, *, B, S, H, ts):
    FW = H * HEAD_DIM
    nts = S // ts
    scale = HEAD_DIM ** -0.5

    def body(dq_ref, dk_ref, dv_ref, zq_ref, zk_ref, zf_ref, bf_ref, gq_ref, gk_ref,
             dzq_ref, dzk_ref, dzv_ref, dzf_ref, small_ref, carry):
        first = (pl.program_id(0) == 0) & (pl.program_id(1) == 0)

        @pl.when(first)
        def _():
            small_ref[...] = jnp.zeros_like(small_ref)

        @pl.when(pl.program_id(1) == 0)
        def _():
            carry[...] = jnp.zeros_like(carry)

        lane = _lane_iota(ts)
        dcs = jnp.zeros((ts, LANES), F32)
        for p in range(H // 2):
            cols = slice(p * LANES, (p + 1) * LANES)
            for d_ref, z_ref, g_ref, o_ref, srow, mul in ((dq_ref, zq_ref, gq_ref, dzq_ref, 0, scale),
                                                          (dk_ref, zk_ref, gk_ref, dzk_ref, 1, 1.0)):
                dy = jnp.where(lane < HEAD_DIM, d_ref[0, 2 * p], pltpu.roll(d_ref[0, 2 * p + 1], HEAD_DIM, 1)) * mul
                xv, g = z_ref[:, cols], g_ref[:, cols]
                r = lax.rsqrt(_half_sums(xv * xv) * (1.0 / HEAD_DIM) + EPS)
                dxh = dy * g
                mm = _half_sums(dxh * xv) * (1.0 / HEAD_DIM)
                o_ref[:, cols] = (r * (dxh - xv * (r * r) * mm)).astype(BF16)
                small_ref[8 * srow:8 * srow + 1, cols] += jnp.sum(dy * xv * r, axis=0, keepdims=True)
            dzv_ref[:, cols] = jnp.where(lane < HEAD_DIM, dv_ref[0, 2 * p], pltpu.roll(dv_ref[0, 2 * p + 1], HEAD_DIM, 1)).astype(BF16)
            for e in range(2):
                h = 2 * p + e
                dc = jnp.sum(jnp.where(lane == C0, dq_ref[0, h], 0.0) - jnp.where(lane == K0, dk_ref[0, h], 0.0),
                             axis=-1, keepdims=True)
                dcs = jnp.where(lane == h, dc, dcs)
        ri = lax.broadcasted_iota(jnp.int32, (ts, ts), 0)
        ci = lax.broadcasted_iota(jnp.int32, (ts, ts), 1)
        tri = (ci >= ri).astype(BF16)
        dl = carry[...]
        for piece in _split3(dcs):
            dl = dl + _dot(tri, piece)
        carry[...] = dl[0:1, :]
        xf = zf_ref[...] + bf_ref[...]
        df = jnp.where(lane < H, dl * _sig(-xf), 0.0)
        dzf_ref[...] = df.astype(BF16)
        small_ref[16:17, 0:LANES] += jnp.sum(df, axis=0, keepdims=True)

    rev = lambda b, i: (b, 0, nts - 1 - i, 0)
    head = pl.BlockSpec((1, H, ts, LANES), rev)
    tokrow = lambda b, i: b * nts + nts - 1 - i
    sec = lambda n: pl.BlockSpec((ts, FW), lambda b, i: (tokrow(b, i), n))
    narrow = pl.BlockSpec((ts, LANES), lambda b, i: (tokrow(b, i), 0))
    row = lambda w: pl.BlockSpec((1, w), lambda b, i: (0, 0))
    return _pcall(
        body, name="attn_post_bwd", grid=(B, nts),
        in_specs=[head, head, head, sec(0), sec(1), narrow, row(LANES), row(FW), row(FW)],
        out_specs=[sec(0), sec(0), sec(0), narrow, pl.BlockSpec((24, FW), lambda b, i: (0, 0))],
        out_shape=[jax.ShapeDtypeStruct((B * S, FW), BF16)] * 3 + [jax.ShapeDtypeStruct((B * S, LANES), BF16),
                                                                   jax.ShapeDtypeStruct((24, FW), F32)],
        scratch_shapes=[pltpu.VMEM((1, LANES), F32)],
    )(dq, dk, dv, z, z, zf, bfp, gq, gk)


def _inproj_bwd_x(dzs, dzf, wmt, wft, x, g, dout, *, tm, comm=None):
    T, D = x.shape
    nj = len(dzs)
    tn = dzs[0].shape[1]
    grid = (T // tm, nj)

    def body(*refs):
        ins, (gx_ref, dg_ref), (acc,), rider = _split_refs(refs, nj + 6, 2, comm)
        dz_refs = ins[:nj]
        dzf_ref, wm_ref, wf_ref, x_ref, g_ref, dout_ref = ins[nj:]
        first, last = _grid_ends(grid)
        if comm:
            comm.begin(rider, first)
        i, j = pl.program_id(0), pl.program_id(1)

        @pl.when((i == 0) & (j == 0))
        def _():
            dg_ref[...] = jnp.zeros_like(dg_ref)

        @pl.when(j == 0)
        def _():
            acc[...] = _dot(dzf_ref[...], wf_ref[...])

        for s in range(nj):
            @pl.when(j == s)
            def _():
                acc[...] += _dot(dz_refs[s][...], wm_ref[...])

        @pl.when(j == nj - 1)
        def _():
            dh, xv, g = acc[...], x_ref[...], g_ref[...]
            r = lax.rsqrt(jnp.mean(xv * xv, axis=-1, keepdims=True) + EPS)
            dxh = dh * g
            mm = jnp.mean(dxh * xv, axis=-1, keepdims=True)
            gx_ref[...] = dout_ref[...] + r * (dxh - xv * (r * r) * mm)
            dg_ref[0:1, :] += jnp.sum(dh * xv * r, axis=0, keepdims=True)

        if comm:
            comm.end(rider, last)

    tok = pl.BlockSpec((tm, D), lambda i, j: (i, 0))
    sec = pl.BlockSpec((tm, tn), lambda i, j: (i, 0))
    return _pcall(
        body, name="inproj_bwd_x", grid=grid,
        **_with_comm(
            comm,
            [sec] * nj + [pl.BlockSpec((tm, LANES), lambda i, j: (i, 0)), pl.BlockSpec((tn, D), lambda i, j: (j, 0)),
                          pl.BlockSpec((LANES, D), lambda i, j: (0, 0)), tok, pl.BlockSpec((1, D), lambda i, j: (0, 0)), tok],
            [tok, pl.BlockSpec((8, D), lambda i, j: (0, 0))],
            [jax.ShapeDtypeStruct((T, D), F32), jax.ShapeDtypeStruct((8, D), F32)],
            [pltpu.VMEM((tm, D), F32)]),
    )(*dzs, dzf, wmt, wft, x, g, dout, *(comm.operands() if comm else ()))


def _inproj_bwd_w(h, dz, *, tm, name):
    T, D = h.shape
    W = dz.shape[1]
    ni = T // tm

    def body(h_ref, dz_ref, dw_ref, acc):
        i = pl.program_id(0)

        @pl.when(i == 0)
        def _():
            acc[...] = jnp.zeros_like(acc)

        acc[...] += _dot_tn(dz_ref[...], h_ref[...])

        @pl.when(i == ni - 1)
        def _():
            dw_ref[...] = acc[...].astype(BF16)

    return _pcall(
        body, name=name, grid=(ni,),
        in_specs=[pl.BlockSpec((tm, D), lambda i: (i, 0)), pl.BlockSpec((tm, W), lambda i: (i, 0))],
        out_specs=pl.BlockSpec((W, D), lambda i: (0, 0)), out_shape=jax.ShapeDtypeStruct((W, D), BF16),
        scratch_shapes=[pltpu.VMEM((W, D), F32)],
    )(h, dz)


def _adamw(w, g, m, v):
    m = ADAM_B1 * m + (1.0 - ADAM_B1) * g
    v = ADAM_B2 * v + (1.0 - ADAM_B2) * (g * g)
    m_hat = m / (1.0 - ADAM_B1 ** ADAM_STEP)
    v_hat = v / (1.0 - ADAM_B2 ** ADAM_STEP)
    return -ADAM_LR * (m_hat / (jnp.sqrt(v_hat) + ADAM_EPS) + ADAM_WD * w), m, v


def _adam_update(w, m, v, parts, *, name, tr):
    R, Cc = w.shape[0], w.shape[-1]

    def body(w_ref, m_ref, v_ref, p_ref, g_ref, d_ref, nm_ref, nv_ref):
        g = p_ref[0].astype(F32)
        for k in range(1, N_DEV):
            g = g + p_ref[k].astype(F32)
        g_ref[...] = g
        d_ref[...], nm_ref[...], nv_ref[...] = _adamw(w_ref[...], g, m_ref[...], v_ref[...])

    blk = pl.BlockSpec((tr, Cc), lambda i: (i, 0)) if w.ndim == 2 else pl.BlockSpec((tr, None, Cc), lambda i: (i, 0, 0))
    shape = jax.ShapeDtypeStruct(w.shape, F32)
    return _pcall(
        body, name=name, grid=(pl.cdiv(R, tr),),
        in_specs=[blk, blk, blk, pl.BlockSpec((N_DEV, tr, Cc), lambda i: (0, i, 0))],
        out_specs=[blk, blk, blk, blk], out_shape=[shape, shape, shape, shape],
    )(w, m, v, parts)


MESH = pl.DeviceIdType.MESH
ANY = pl.BlockSpec(memory_space=pl.ANY)


def _flip(v, bit):
    return 1 - v if bit else v


def _allgather(shards):
    n = len(shards)

    def body(*refs):
        srcs, outs = refs[:n], refs[n:2 * n]
        send_sems, recv_sems, local_sems = refs[2 * n:]
        x, y, c = lax.axis_index("x"), lax.axis_index("y"), lax.axis_index("c")
        sibling = (x, y, 1 - c)
        chips = [(1 - x, y), (x, 1 - y), (1 - x, 1 - y)]
        slot = lambda px, py, pc: 4 * px + 2 * py + pc

        def copy(a, k, block, to, src=None):
            dst = outs[a].at[slot(*block)]
            return pltpu.make_async_remote_copy(src_ref=dst if src is None else src, dst_ref=dst, send_sem=send_sems.at[7 * a + k],
                                                recv_sem=recv_sems.at[7 * a + k], device_id=to, device_id_type=MESH)

        mine = [pltpu.make_async_copy(srcs[a], outs[a].at[slot(x, y, c)], local_sems.at[a]) for a in range(n)]
        sends = []
        for a in range(n):
            mine[a].start()
            sends.append(copy(a, 0, (x, y, c), sibling, src=srcs[a]))
            sends += [copy(a, 1 + j, (x, y, c), (*chip, c), src=srcs[a]) for j, chip in enumerate(chips)]
        for cp in sends:
            cp.start()
        for a in range(n):
            for j, chip in enumerate(chips):
                copy(a, 1 + j, (*chip, c), (x, y, c)).wait_recv()
                fwd = copy(a, 4 + j, (*chip, c), sibling)
                fwd.start()
                sends.append(fwd)
        for a in range(n):
            copy(a, 0, (x, y, 1 - c), (x, y, c)).wait_recv()
            for j, chip in enumerate(chips):
                copy(a, 4 + j, (*chip, 1 - c), (x, y, c)).wait_recv()
        for cp in sends:
            cp.wait_send()
        for cp in mine:
            cp.wait()

    return pl.pallas_call(
        body, name="allgather_weights", in_specs=[ANY] * n, out_specs=[ANY] * n,
        out_shape=[jax.ShapeDtypeStruct((N_DEV,) + s.shape, s.dtype) for s in shards],
        scratch_shapes=[pltpu.SemaphoreType.DMA((7 * n,)), pltpu.SemaphoreType.DMA((7 * n,)), pltpu.SemaphoreType.DMA((n,))],
    )(*shards)


class _Comm:
    def __init__(self, arrays, owners, into=None):
        self.arrays, self.owners, self.n = list(arrays), list(owners), len(arrays)
        self.into = list(into) if into else [None] * self.n
        self.carried = [t for t in self.into if t is not None]

    def operands(self):
        return self.arrays + self.carried

    def aliases(self, n_in, n_out):
        out, pos = {}, 0
        for a, t in enumerate(self.into):
            if t is not None:
                out[n_in + self.n + pos] = n_out + a
                pos += 1
        return out

    def out_shape(self):
        return [jax.ShapeDtypeStruct((N_DEV,) + (a.shape if o is None else a.shape[1:]), a.dtype)
                for a, o in zip(self.arrays, self.owners)]

    def scratch(self):
        return [pltpu.SemaphoreType.DMA((7 * self.n,)), pltpu.SemaphoreType.DMA((7 * self.n,)), pltpu.SemaphoreType.DMA((self.n,))]

    def ops(self, srcs, outs, send_sems, recv_sems, local_sems):
        x, y, c = lax.axis_index("x"), lax.axis_index("y"), lax.axis_index("c")
        me = 4 * x + 2 * y + c
        when = lambda p, f: (lambda: pl.when(p)(f))
        starts, waits = [], []
        for a, owners in enumerate(self.owners):
            src, out = srcs[a], outs[a]

            def remote(piece, slot, rel, to, a=a, out=out):
                sem = 7 * a + rel - 1
                return pltpu.make_async_remote_copy(src_ref=piece, dst_ref=out.at[slot], send_sem=send_sems.at[sem],
                                                    recv_sem=recv_sems.at[sem], device_id=to, device_id_type=MESH)

            peers = [(d, (_flip(x, d & 4), _flip(y, d & 2), _flip(c, d & 1))) for d in range(1, N_DEV)]
            if owners is None:
                local = pltpu.make_async_copy(src, out.at[me], local_sems.at[a])
                starts.append(local.start)
                waits.append(local.wait)
                for d, peer in peers:
                    cp = remote(src, me, d, peer)
                    starts.append(cp.start)
                    waits.append(cp.wait_send)
                receives = True
            else:
                for s, dest in enumerate(owners):
                    to = (dest >> 2 & 1, dest >> 1 & 1, dest & 1)
                    rel = 4 * _flip(x, to[0]) + 2 * _flip(y, to[1]) + _flip(c, to[2])
                    cp = remote(src.at[s], me, jnp.maximum(rel, 1), to)
                    local = pltpu.make_async_copy(src.at[s], out.at[me], local_sems.at[a])
                    starts += [when(rel != 0, cp.start), when(rel == 0, local.start)]
                    waits += [when(rel != 0, cp.wait_send), when(rel == 0, local.wait)]
                receives = me == owners[0]
                for dest in owners[1:]:
                    receives = receives | (me == dest)
            piece = src if owners is None else src.at[0]
            for d, peer in peers:
                def arrival(d=d, peer=peer, piece=piece, remote=remote):
                    remote(piece, 4 * peer[0] + 2 * peer[1] + peer[2], d, peer).wait_recv()

                waits.append(arrival if receives is True else when(receives, arrival))
        return starts, waits

    def _phase(self, which, rider, cond):
        srcs, outs, sems = rider

        def go():
            for op in self.ops(srcs, outs, *sems)[which]:
                op()

        go() if cond is True else pl.when(cond)(go)

    def begin(self, rider, first=True):
        self._phase(0, rider, first)

    def end(self, rider, last=True):
        self._phase(1, rider, last)


def _split_refs(refs, n_in, n_out, comm):
    k = comm.n if comm else 0
    o0 = n_in + k + (len(comm.carried) if comm else 0)
    ins, c_src = refs[:n_in], refs[n_in:n_in + k]
    outs, c_out = refs[o0:o0 + n_out], refs[o0 + n_out:o0 + n_out + k]
    rest = refs[o0 + n_out + k:]
    scratch, c_sem = (rest[:-3], rest[-3:]) if comm else (rest, ())
    return ins, outs, scratch, (c_src, c_out, c_sem)


def _grid_ends(grid):
    first = functools.reduce(jnp.logical_and, [pl.program_id(a) == 0 for a in range(len(grid))])
    last = functools.reduce(jnp.logical_and, [pl.program_id(a) == g - 1 for a, g in enumerate(grid)])
    return first, last


def _with_comm(comm, in_specs, out_specs, out_shape, scratch):
    if comm is None:
        return dict(in_specs=in_specs, out_specs=out_specs, out_shape=out_shape, scratch_shapes=scratch)
    return dict(in_specs=in_specs + [ANY] * len(comm.operands()), out_specs=out_specs + [ANY] * comm.n,
                out_shape=out_shape + comm.out_shape(), scratch_shapes=scratch + comm.scratch(),
                input_output_aliases=comm.aliases(len(in_specs), len(out_specs)))


def _exchange(comm, *, name):
    n = comm.n

    def body(*refs):
        rider = (refs[:n], refs[n:2 * n], refs[2 * n:])
        comm.begin(rider)
        comm.end(rider)

    return pl.pallas_call(body, name=name, in_specs=[ANY] * n, out_specs=[ANY] * n, out_shape=comm.out_shape(),
                          scratch_shapes=comm.scratch())(*comm.arrays)


def kernel(x, norm_g, w_in, b_forget, q_norm_g, k_norm_g, conv_w, conv_b, conv_ln_g, conv_ln_b, w_out, loss_target, m_norm_g, m_w_in, m_b_forget, m_q_norm_g, m_k_norm_g, m_conv_w, m_conv_b, m_conv_ln_g, m_conv_ln_b, m_w_out, v_norm_g, v_w_in, v_b_forget, v_q_norm_g, v_k_norm_g, v_conv_w, v_conv_b, v_conv_ln_g, v_conv_ln_b, v_w_out):
    B, S, D = x.shape
    H = q_norm_g.shape[1]
    FW = H * HEAD_DIM
    assert q_norm_g.shape[2] == HEAD_DIM and FW == D and conv_b.shape[1] == D and H <= LANES // 3
    ncol = w_in.shape[2]
    T = B * S

    to_t = lambda t: jnp.transpose(t, (2, 0, 1))
    from_t = lambda t: jnp.transpose(t, (1, 2, 0))
    wt, mt, vt = to_t(w_in), to_t(m_w_in), to_t(v_w_in)

    x2, tgt = x.reshape(T, D), loss_target.reshape(T, D)
    tm, ts, rc = min(512, T), min(256, S), min(64, S)
    everyone = list(range(N_DEV))
    bfp = jnp.pad(b_forget, ((0, 0), (0, LANES - H)))
    gq, gk = q_norm_g.reshape(1, FW), k_norm_g.reshape(1, FW)

    (wg,) = _allgather([wt.reshape(ncol, D).astype(BF16)])
    wmt, wft = _w_relayout(wg, D=D, H=H)
    z, zf, h, wog, cwg = _inproj_fwd(x2, norm_g, wmt, wft, tm=min(1024, T), tn=FW,
                                     comm=_Comm([w_out[0].astype(BF16), conv_w[0]], [None, None]))
    wo = wog.reshape(N_DEV * wog.shape[1], D)
    cw = jnp.transpose(cwg, (1, 0, 2)).reshape(CONV_K, D)
    qa, ka, va = _attn_prep(z, zf, bfp, gq, gk, B=B, S=S, H=H, ts=ts)
    a, qb = _attn_fwd(qa, ka, va, t=min(512, S), blk=min(512, S), G=4)
    uc = _conv_fwd(z, cw, conv_b, B=B, S=S, sec_w=FW, rc=rc)
    uo = _conv_post(uc, z, conv_ln_g, conv_ln_b, tm=ts, sec_w=FW)
    dout, dy, dwo, loss = _outproj(a, z, uo, x2, tgt, wo, tm=ts, sec_w=FW)

    dgc, duc, small_c = _conv_post_bwd(dy, uc, z, conv_ln_g, conv_ln_b, tm=ts, sec_w=FW)
    dga, dgb, dcw = _conv_bwd(duc, z, cw, B=B, S=S, sec_w=FW, rc=rc)
    dgf, doa = _attn_bwd_prep(dy, a, z, B=B, S=S, H=H, ts=ts, sec_w=FW)
    _, f1, _ = _section_rows(D, H)
    late_rows = [k for k in everyone if k * ncol >= f1]
    early_rows = [k for k in everyone if k not in late_rows]
    first_late_sec = (late_rows[0] * ncol - H) // FW
    dws = {s: _inproj_bwd_w(h, dz, tm=tm, name=f"inproj_bwd_w{s}") for s, dz in ((3, dgf), (4, dga), (5, dgb), (6, dgc))}
    dw_late = _dw_relayout({s: dws[s] for s in range(first_late_sec, 7)}, None, R=ncol, H=H, shards=late_rows, name="dw_relayout_late")
    dwo_parts = dwo.reshape(N_DEV, dwo.shape[0] // N_DEV, D).astype(BF16)
    dcw_parts = jnp.transpose(dcw.reshape(CONV_K, N_DEV, D // N_DEV), (1, 0, 2))
    dq, dk, dv, r_w, r_wo, r_cw = _attn_bwd(qb, ka, va, doa, t=min(512, S), blk=min(512, S), G=4,
                                            comm=_Comm([dw_late, dwo_parts, dcw_parts], [late_rows, everyone, everyone]))
    dzq, dzk, dzv, dzf, small_a = _attn_post_bwd(dq, dk, dv, z, zf, bfp, gq, gk, B=B, S=S, H=H, ts=ts)
    dws.update({s: _inproj_bwd_w(h, dz, tm=tm, name=f"inproj_bwd_w{s}") for s, dz in ((0, dzq), (1, dzk), (2, dzv))})
    dwf = _inproj_bwd_w(h, dzf, tm=tm, name="inproj_bwd_wf")
    last_early_sec = (early_rows[-1] * ncol + ncol - 1 - H) // FW
    dw_early = _dw_relayout({s: dws[s] for s in range(last_early_sec + 1)}, dwf, R=ncol, H=H, shards=early_rows, name="dw_relayout_early")
    gx, dg, r_w = _inproj_bwd_x([dzq, dzk, dzv, dgf, dga, dgb, dgc], dzf, wmt, wft, x2, norm_g, dout, tm=tm,
                                comm=_Comm([dw_early], [early_rows], into=[r_w]))
    small = jnp.concatenate([dg, small_a[16:24], small_a[0:8], small_a[8:16], small_c[16:24], small_c[0:8], small_c[8:16],
                             jnp.pad(loss, ((0, 0), (0, D - LANES)))], axis=0)
    (r_small,) = _exchange(_Comm([small], [None]), name="exchange_small")

    g_w, d_w, nm_w, nv_w = [from_t(t) for t in _adam_update(wt, mt, vt, r_w, name="adam_w_in", tr=min(128, D))]
    g_wo, d_wo, nm_wo, nv_wo = [t[None] for t in _adam_update(w_out[0], m_w_out[0], v_w_out[0], r_wo, name="adam_w_out",
                                                               tr=min(128, w_out.shape[1]))]
    g_cw, d_cw, nm_cw, nv_cw = [t[None] for t in _adam_update(conv_w[0], m_conv_w[0], v_conv_w[0], r_cw, name="adam_conv_w", tr=CONV_K)]
    tile = lambda t: jnp.pad(t.reshape(1, -1), ((0, 7), (0, D - t.size)))
    pack = lambda *ps: jnp.concatenate([tile(p) for p in ps] + [jnp.zeros((8, D), F32)], axis=0)
    small_w = pack(norm_g, b_forget, q_norm_g, k_norm_g, conv_b, conv_ln_g, conv_ln_b)
    small_m = pack(m_norm_g, m_b_forget, m_q_norm_g, m_k_norm_g, m_conv_b, m_conv_ln_g, m_conv_ln_b)
    small_v = pack(v_norm_g, v_b_forget, v_q_norm_g, v_k_norm_g, v_conv_b, v_conv_ln_g, v_conv_ln_b)
    g_s, d_s, nm_s, nv_s = _adam_update(small_w, small_m, small_v, r_small, name="adam_small", tr=small_w.shape[0])

    def leaves(small_t, w_t, cw_t, wo_t):
        row = lambda r, like: small_t[8 * r:8 * r + 1, :like.size].reshape(like.shape)
        return [row(0, norm_g), w_t, row(1, b_forget), row(2, q_norm_g), row(3, k_norm_g), cw_t, row(4, conv_b),
                row(5, conv_ln_g), row(6, conv_ln_b), wo_t]

    return (g_s[56, 0], gx.reshape(B, S, D), *leaves(g_s, g_w, g_cw, g_wo), *leaves(d_s, d_w, d_cw, d_wo),
            *leaves(nm_s, nm_w, nm_cw, nm_wo), *leaves(nv_s, nv_w, nv_cw, nv_wo))
```

```python
import functools

import jax
import jax.numpy as jnp
from jax import lax
from jax.experimental import pallas as pl
from jax.experimental.pallas import tpu as pltpu

F32, BF16 = jnp.float32, jnp.bfloat16
EPS = 1e-6
NEG_INF = -1e30
CONV_K = 31
HEAD_DIM = 64
ADAM_LR, ADAM_B1, ADAM_B2, ADAM_EPS, ADAM_WD, ADAM_STEP = 0.001, 0.9, 0.999, 1e-08, 0.01, 10

LANES = 128
HALO = 32
N_DEV = 8
VMEM_LIMIT = 56 * 1024 * 1024

C0, K0, L0 = HEAD_DIM, HEAD_DIM + 3, HEAD_DIM + 6


def _pcall(body, *, name, vmem=VMEM_LIMIT, **kw):
    return pl.pallas_call(body, name=name, compiler_params=pltpu.CompilerParams(vmem_limit_bytes=vmem), **kw)


def _dot(a, b):
    return jnp.dot(a, b, preferred_element_type=F32)


def _dot_nt(a, b):
    return lax.dot_general(a, b, (((1,), (1,)), ((), ())), preferred_element_type=F32)


def _dot_tn(a, b):
    return lax.dot_general(a, b, (((0,), (0,)), ((), ())), preferred_element_type=F32)


def _split3(x):
    a = x.astype(BF16)
    r = x - a.astype(F32)
    b = r.astype(BF16)
    c = (r - b.astype(F32)).astype(BF16)
    return a, b, c


def _sig(x):
    return jax.nn.sigmoid(x)


def _dsilu(x, s):
    return s * (1.0 + x * (1.0 - s))


def _lane_iota(rows):
    return lax.broadcasted_iota(jnp.int32, (rows, LANES), 1)


def _half_sums(v):
    r = lax.broadcasted_iota(jnp.int32, (LANES, LANES), 0) < HEAD_DIM
    c = lax.broadcasted_iota(jnp.int32, (LANES, LANES), 1) < HEAD_DIM
    ones = (r == c).astype(BF16)
    a, b, d = _split3(v)
    return _dot(a, ones) + _dot(b, ones) + _dot(d, ones)


def _head_to_low(vb, e):
    r = lax.broadcasted_iota(jnp.int32, (LANES, LANES), 0)
    c = lax.broadcasted_iota(jnp.int32, (LANES, LANES), 1)
    return _dot(vb, ((r == c + e * HEAD_DIM) & (c < HEAD_DIM)).astype(BF16))


def _head_to_low_roll(v, e, lane):
    if e:
        v = pltpu.roll(v, HEAD_DIM, 1)
    return jnp.where(lane < HEAD_DIM, v, 0.0)


def _put3(base, lane, first, pieces):
    for n, p in enumerate(pieces):
        base = jnp.where(lane == first + n, p.astype(F32), base)
    return base


def _section_rows(D, H):
    FW = H * HEAD_DIM
    return 3 * FW, 3 * FW + H, 7 * FW + H


def _row_pieces(a, b, R, step):
    out = []
    while a < b:
        k = a // R
        lo = a - k * R
        hi = min(R, lo + min(step, b - a))
        out.append((k, lo, hi))
        a += hi - lo
    return out


def _w_relayout(wg, *, D, H):
    R = wg.shape[1]
    FW = H * HEAD_DIM
    f0, f1, end = _section_rows(D, H)

    def body(wg_ref, wm_ref, wf_ref):
        def put(dst_ref, d0, a, b):
            for k, lo, hi in _row_pieces(a, b, R, 256):
                dst_ref[d0:d0 + hi - lo, :] = wg_ref[k, lo:hi, :]
                d0 += hi - lo

        put(wm_ref, 0, 0, f0)
        put(wm_ref, f0, f1, end)
        wf_ref[...] = jnp.zeros_like(wf_ref)
        put(wf_ref, 0, f0, f1)

    return _pcall(body, name="w_relayout",
                  out_shape=[jax.ShapeDtypeStruct((7 * FW, D), BF16), jax.ShapeDtypeStruct((LANES, D), BF16)])(wg)


def _dw_relayout(secs, dwf, *, R, H, shards, name):
    FW, D = next(iter(secs.values())).shape
    f0, f1, end = _section_rows(D, H)
    order = sorted(secs)

    def body(*refs):
        sec_refs = dict(zip(order, refs))
        dwf_ref, out_ref = (refs[-2] if dwf is not None else None), refs[-1]

        def src(g0, g1):
            if f0 <= g0 < f1:
                return dwf_ref[g0 - f0:g1 - f0, :]
            s0 = g0 if g0 < f0 else g0 - H
            return sec_refs[s0 // FW][s0 % FW:s0 % FW + (g1 - g0), :]

        cuts = sorted({0, end, f0, f1} | {s * FW for s in range(4)} | {f1 + s * FW for s in range(5)})
        for slot, k in enumerate(shards):
            g = k * R
            while g < (k + 1) * R:
                nxt = min(min(c for c in cuts if c > g), (k + 1) * R, g + 256)
                out_ref[slot, g - k * R:nxt - k * R, :] = src(g, nxt)
                g = nxt

    args = [secs[s] for s in order] + ([dwf] if dwf is not None else [])
    return _pcall(body, name=name, out_shape=jax.ShapeDtypeStruct((len(shards), R, D), BF16))(*args)


def _inproj_fwd(x, g, wmt, wft, *, tm, tn, comm=None):
    T, D = x.shape
    nj = wmt.shape[0] // tn
    grid = (T // tm, nj)

    def body(*refs):
        (x_ref, g_ref, wm_ref, wf_ref), (z_ref, zf_ref, h_ref), (h_scr,), rider = _split_refs(refs, 4, 3, comm)
        first, last = _grid_ends(grid)
        if comm:
            comm.begin(rider, first)

        @pl.when(pl.program_id(1) == 0)
        def _():
            xv = x_ref[...]
            r = lax.rsqrt(jnp.mean(xv * xv, axis=-1, keepdims=True) + EPS)
            h = ((xv * r) * g_ref[...]).astype(BF16)
            h_scr[...] = h
            h_ref[...] = h
            zf_ref[...] = _dot_nt(h, wf_ref[...])

        z_ref[...] = _dot_nt(h_scr[...], wm_ref[...])
        if comm:
            comm.end(rider, last)

    return _pcall(
        body, name="inproj_fwd", grid=grid,
        **_with_comm(
            comm,
            [pl.BlockSpec((tm, D), lambda i, j: (i, 0)), pl.BlockSpec((1, D), lambda i, j: (0, 0)),
             pl.BlockSpec((tn, D), lambda i, j: (j, 0)), pl.BlockSpec((LANES, D), lambda i, j: (0, 0))],
            [pl.BlockSpec((tm, tn), lambda i, j: (i, j)), pl.BlockSpec((tm, LANES), lambda i, j: (i, 0)),
             pl.BlockSpec((tm, D), lambda i, j: (i, 0))],
            [jax.ShapeDtypeStruct((T, wmt.shape[0]), F32), jax.ShapeDtypeStruct((T, LANES), F32), jax.ShapeDtypeStruct((T, D), BF16)],
            [pltpu.VMEM((tm, D), BF16)]),
    )(x, g, wmt, wft, *(comm.operands() if comm else ()))


def _attn_prep(z, zf, bfp, gq, gk, *, B, S, H, ts):
    FW = H * HEAD_DIM
    nts = S // ts
    scale = HEAD_DIM ** -0.5

    def body(zq_ref, zk_ref, zv_ref, zf_ref, bf_ref, gq_ref, gk_ref, qa_ref, ka_ref, va_ref, carry):
        @pl.when(pl.program_id(1) == 0)
        def _():
            carry[...] = jnp.zeros_like(carry)

        lane = _lane_iota(ts)
        xf = zf_ref[...] + bf_ref[...]
        logf = jnp.minimum(xf, 0.0) - jnp.log1p(jnp.exp(-jnp.abs(xf)))
        logf = jnp.where(lane < H, logf, 0.0)
        ri = lax.broadcasted_iota(jnp.int32, (ts, ts), 0)
        ci = lax.broadcasted_iota(jnp.int32, (ts, ts), 1)
        tri = (ri >= ci).astype(BF16)
        c = carry[...]
        for piece in _split3(logf):
            c = c + _dot(tri, piece)
        carry[...] = c[ts - 1:ts, :]
        c1, c2, c3 = _split3(c)
        cpk = c1.astype(F32) + pltpu.roll(c2.astype(F32), H, 1) + pltpu.roll(c3.astype(F32), 2 * H, 1)
        cq, ck = cpk.astype(BF16), (-cpk).astype(BF16)
        prow = lax.broadcasted_iota(jnp.int32, (LANES, LANES), 0)
        pcol = lax.broadcasted_iota(jnp.int32, (LANES, LANES), 1)
        ones_q = jnp.where((lane >= K0) & (lane < K0 + 3), 1.0, 0.0)
        ones_k = jnp.where(((lane >= C0) & (lane < C0 + 3)) | ((lane >= L0) & (lane < L0 + 3)), 1.0, 0.0)
        ones_v = jnp.where((lane >= C0) & (lane < C0 + 3), 1.0, 0.0)
        for h in range(H):
            p, e = divmod(h, 2)
            cols = slice(p * LANES, (p + 1) * LANES)
            piece = jnp.where(prow == h, 0, jnp.where(prow == H + h, 1, jnp.where(prow == 2 * H + h, 2, -LANES)))
            aug_q = _dot(cq, (pcol == C0 + piece).astype(BF16)) + ones_q
            aug_k = _dot(ck, (pcol == K0 + piece).astype(BF16)) + ones_k
            for z_ref, g_ref, o_ref, aug, mul in ((zq_ref, gq_ref, qa_ref, aug_q, scale), (zk_ref, gk_ref, ka_ref, aug_k, 1.0)):
                blk = z_ref[:, cols]
                r = lax.rsqrt(_half_sums(blk * blk) * (1.0 / HEAD_DIM) + EPS)
                y = ((blk * r) * g_ref[:, cols]).astype(BF16).astype(F32) * mul
                o_ref[0, h] = (_head_to_low_roll(y, e, lane) + aug).astype(BF16)
            va_ref[0, h] = (_head_to_low_roll(zv_ref[:, cols], e, lane) + ones_v).astype(BF16)

    sec = lambda n: pl.BlockSpec((ts, FW), lambda b, i: (b * nts + i, n))
    row = lambda w: pl.BlockSpec((1, w), lambda b, i: (0, 0))
    head = pl.BlockSpec((1, H, ts, LANES), lambda b, i: (b, 0, i, 0))
    hshape = jax.ShapeDtypeStruct((B, H, S, LANES), BF16)
    return _pcall(
        body, name="attn_prep", grid=(B, nts),
        in_specs=[sec(0), sec(1), sec(2), pl.BlockSpec((ts, LANES), lambda b, i: (b * nts + i, 0)), row(LANES), row(FW), row(FW)],
        out_specs=[head, head, head], out_shape=[hshape, hshape, hshape],
        scratch_shapes=[pltpu.VMEM((1, LANES), F32)],
    )(z, z, z, zf, bfp, gq, gk)


def _attn_fwd(qa, ka, va, *, t, blk, dblk, G):
    B, H, S, _ = qa.shape
    n = S // t
    assert t % blk == 0

    def body(q_ref, k_ref, v_ref, a_ref, qb_ref):
        i = pl.program_id(2)
        lane = _lane_iota(t)
        qs = [q_ref[0, e] for e in range(G)]

        def step(start, width, carry, q0=None):
            rows = pl.ds(pl.multiple_of(start, dblk), width)
            lo = q0 or 0
            new = []
            for e in range(G):
                m, acc = carry[e]
                s = _dot_nt(qs[e][lo:], k_ref[0, e, rows, :])
                if q0 is not None:
                    row = lax.broadcasted_iota(jnp.int32, (t - lo, width), 0)
                    col = lax.broadcasted_iota(jnp.int32, (t - lo, width), 1)
                    s = jnp.where(col <= row, s, NEG_INF)
                m_new = jnp.maximum(m[lo:], jnp.max(s, axis=-1, keepdims=True))
                p = jnp.exp(s - m_new)
                acc_new = jnp.exp(m[lo:] - m_new) * acc[lo:] + _dot(p.astype(BF16), v_ref[0, e, rows, :])
                if lo:
                    m_new, acc_new = jnp.concatenate([m[:lo], m_new], axis=0), jnp.concatenate([acc[:lo], acc_new], axis=0)
                new.append((m_new, acc_new))
            return tuple(new)

        carry = tuple((jnp.full((t, 1), NEG_INF, F32), jnp.zeros((t, LANES), F32)) for _ in range(G))
        carry = lax.fori_loop(0, i * (t // blk), lambda j, c: step(j * blk, blk, c), carry)
        for d in range(t // dblk):
            carry = step(i * t + d * dblk, dblk, carry, q0=d * dblk)
        outs = []
        for e in range(G):
            m, acc = carry[e]
            l = jnp.sum(jnp.where(lane == C0, acc, 0.0), axis=-1, keepdims=True)
            outs.append(acc / l)
            qb_ref[0, e] = _put3(qs[e].astype(F32), lane, L0, _split3(-(m + jnp.log(l)))).astype(BF16)
        for pp in range(G // 2):
            a_ref[:, pp * LANES:(pp + 1) * LANES] = jnp.where(lane < HEAD_DIM, outs[2 * pp], pltpu.roll(outs[2 * pp + 1], HEAD_DIM, 1))

    return _pcall(
        body, name="attn_fwd", grid=(B, H // G, n),
        in_specs=[pl.BlockSpec((1, G, t, LANES), lambda b, p, i: (b, p, i, 0)),
                  pl.BlockSpec((1, G, S, LANES), lambda b, p, i: (b, p, 0, 0)),
                  pl.BlockSpec((1, G, S, LANES), lambda b, p, i: (b, p, 0, 0))],
        out_specs=[pl.BlockSpec((t, G * HEAD_DIM), lambda b, p, i: (b * n + i, p)),
                   pl.BlockSpec((1, G, t, LANES), lambda b, p, i: (b, p, i, 0))],
        out_shape=[jax.ShapeDtypeStruct((B * S, H * HEAD_DIM), F32), jax.ShapeDtypeStruct((B, H, S, LANES), BF16)],
    )(qa, ka, va)


def _conv_fwd(z, cw, cb, *, B, S, sec_w, rc):
    C = cw.shape[1]
    nct = C // LANES
    off_a, off_b = 4 * sec_w // LANES, 5 * sec_w // LANES

    def body(ga_ref, gb_ref, w_ref, b_ref, uc_ref, buf):
        buf[0:HALO, :] = jnp.zeros((HALO, LANES), F32)
        buf[HALO:HALO + S, :] = ga_ref[...] * _sig(gb_ref[...])

        def chunk(cidx, carry):
            r0 = pl.multiple_of(cidx * rc, rc)
            acc = jnp.zeros((rc, LANES), F32)
            for j in range(CONV_K):
                acc = acc + w_ref[j:j + 1, :] * buf[pl.ds(r0 + (HALO - CONV_K + 1 + j), rc), :]
            uc_ref[pl.ds(r0, rc), :] = acc + b_ref[...]
            return carry

        lax.fori_loop(0, S // rc, chunk, 0)

    return _pcall(
        body, name="conv_fwd", grid=(B, nct),
        in_specs=[pl.BlockSpec((S, LANES), lambda b, c: (b, off_a + c)), pl.BlockSpec((S, LANES), lambda b, c: (b, off_b + c)),
                  pl.BlockSpec((CONV_K, LANES), lambda b, c: (0, c)), pl.BlockSpec((1, LANES), lambda b, c: (0, c))],
        out_specs=pl.BlockSpec((S, LANES), lambda b, c: (b, c)),
        out_shape=jax.ShapeDtypeStruct((B * S, C), F32),
        scratch_shapes=[pltpu.VMEM((HALO + S, LANES), F32)],
    )(z, z, cw, cb)


def _ln_parts(uc, lg, lb):
    mu = jnp.mean(uc, axis=-1, keepdims=True)
    d = uc - mu
    rs = lax.rsqrt(jnp.mean(d * d, axis=-1, keepdims=True) + EPS)
    un = d * rs
    return rs, un, un * lg + lb


def _conv_post(uc, z, lg, lb, *, tm, sec_w):
    T, C = uc.shape
    off_c = 6 * sec_w // C

    def body(uc_ref, gc_ref, lg_ref, lb_ref, uo_ref):
        _, _, ul = _ln_parts(uc_ref[...], lg_ref[...], lb_ref[...])
        gc = gc_ref[...]
        uo_ref[...] = ((ul * _sig(ul)) * (gc * _sig(gc))).astype(BF16)

    return _pcall(
        body, name="conv_post", grid=(T // tm,),
        in_specs=[pl.BlockSpec((tm, C), lambda i: (i, 0)), pl.BlockSpec((tm, C), lambda i: (i, off_c)),
                  pl.BlockSpec((1, C), lambda i: (0, 0)), pl.BlockSpec((1, C), lambda i: (0, 0))],
        out_specs=pl.BlockSpec((tm, C), lambda i: (i, 0)), out_shape=jax.ShapeDtypeStruct((T, C), BF16),
    )(uc, z, lg, lb)


def _outproj(a, z, uo, x, tgt, wo, *, tm, sec_w):
    T, D = x.shape
    FW = a.shape[1]
    C = uo.shape[1]

    def body(a_ref, gf_ref, uo_ref, x_ref, t_ref, wo_ref, dout_ref, dy_ref, dwo_ref, loss_ref):
        @pl.when(pl.program_id(0) == 0)
        def _():
            dwo_ref[...] = jnp.zeros_like(dwo_ref)
            loss_ref[...] = jnp.zeros_like(loss_ref)

        gf = gf_ref[...]
        ao = (a_ref[...] * (gf * _sig(gf))).astype(BF16)
        uo = uo_ref[...]
        out = x_ref[...] + (_dot(ao, wo_ref[0:FW, :]) + _dot(uo, wo_ref[FW:FW + C, :]))
        err = out - t_ref[...]
        loss_ref[...] += jnp.sum(err * err) * (0.5 / D)
        dout = err * (1.0 / D)
        dout_ref[...] = dout
        db = dout.astype(BF16)
        dy_ref[:, 0:FW] = _dot_nt(db, wo_ref[0:FW, :])
        dy_ref[:, FW:FW + C] = _dot_nt(db, wo_ref[FW:FW + C, :])
        dwo_ref[0:FW, :] += _dot_tn(ao, db)
        dwo_ref[FW:FW + C, :] += _dot_tn(uo, db)

    tok = lambda w, n=0: pl.BlockSpec((tm, w), lambda i: (i, n))
    return _pcall(
        body, name="outproj", grid=(T // tm,),
        in_specs=[tok(FW), tok(sec_w, 3), tok(C), tok(D), tok(D), pl.BlockSpec((FW + C, D), lambda i: (0, 0))],
        out_specs=[tok(D), tok(FW + C), pl.BlockSpec((FW + C, D), lambda i: (0, 0)), pl.BlockSpec((8, LANES), lambda i: (0, 0))],
        out_shape=[jax.ShapeDtypeStruct((T, D), F32), jax.ShapeDtypeStruct((T, FW + C), F32),
                   jax.ShapeDtypeStruct((FW + C, D), F32), jax.ShapeDtypeStruct((8, LANES), F32)],
    )(a, z, uo, x, tgt, wo)


def _conv_post_bwd(dy, uc, z, lg, lb, *, tm, sec_w):
    T, C = uc.shape
    FW = dy.shape[1] - C
    off_c = 6 * sec_w // C

    def body(dy_ref, uc_ref, gc_ref, lg_ref, lb_ref, dgc_ref, duc_ref, small_ref):
        @pl.when(pl.program_id(0) == 0)
        def _():
            small_ref[...] = jnp.zeros_like(small_ref)

        lg = lg_ref[...]
        rs, un, ul = _ln_parts(uc_ref[...], lg, lb_ref[...])
        s_ul = _sig(ul)
        gc = gc_ref[...]
        s_gc = _sig(gc)
        duo = dy_ref[...]
        dgc_ref[...] = (duo * (ul * s_ul) * _dsilu(gc, s_gc)).astype(BF16)
        dul = duo * (gc * s_gc) * _dsilu(ul, s_ul)
        dun = dul * lg
        duc = rs * (dun - jnp.mean(dun, axis=-1, keepdims=True) - un * jnp.mean(dun * un, axis=-1, keepdims=True))
        duc_ref[...] = duc
        small_ref[0:1, :] += jnp.sum(dul * un, axis=0, keepdims=True)
        small_ref[8:9, :] += jnp.sum(dul, axis=0, keepdims=True)
        small_ref[16:17, :] += jnp.sum(duc, axis=0, keepdims=True)

    tok = lambda n: pl.BlockSpec((tm, C), lambda i: (i, n))
    row = pl.BlockSpec((1, C), lambda i: (0, 0))
    return _pcall(
        body, name="conv_post_bwd", grid=(T // tm,),
        in_specs=[tok(FW // C), tok(0), tok(off_c), row, row],
        out_specs=[tok(0), tok(0), pl.BlockSpec((24, C), lambda i: (0, 0))],
        out_shape=[jax.ShapeDtypeStruct((T, C), BF16), jax.ShapeDtypeStruct((T, C), F32), jax.ShapeDtypeStruct((24, C), F32)],
    )(dy, uc, z, lg, lb)


def _conv_bwd(duc, z, cw, *, B, S, sec_w, rc):
    C = cw.shape[1]
    nct = C // LANES
    off_a, off_b = 4 * sec_w // LANES, 5 * sec_w // LANES

    def body(duc_ref, ga_ref, gb_ref, w_ref, dga_ref, dgb_ref, dw_ref, ubuf, dbuf):
        @pl.when(pl.program_id(1) == 0)
        def _():
            dw_ref[...] = jnp.zeros_like(dw_ref)

        ubuf[0:HALO, :] = jnp.zeros((HALO, LANES), F32)
        ubuf[HALO:HALO + S, :] = ga_ref[...] * _sig(gb_ref[...])
        dbuf[0:S, :] = duc_ref[...]
        dbuf[S:S + HALO, :] = jnp.zeros((HALO, LANES), F32)

        def chunk(cidx, carry):
            r0 = pl.multiple_of(cidx * rc, rc)
            acc = jnp.zeros((rc, LANES), F32)
            for j in range(CONV_K):
                acc = acc + w_ref[j:j + 1, :] * dbuf[pl.ds(r0 + (CONV_K - 1 - j), rc), :]
            rows = pl.ds(r0, rc)
            ga, sg = ga_ref[rows, :], _sig(gb_ref[rows, :])
            dga_ref[rows, :] = (acc * sg).astype(BF16)
            dgb_ref[rows, :] = (acc * ga * sg * (1.0 - sg)).astype(BF16)
            return carry

        lax.fori_loop(0, S // rc, chunk, 0)
        d = duc_ref[...]
        for j in range(CONV_K):
            dw_ref[j:j + 1, :] += jnp.sum(d * ubuf[pl.ds(HALO - CONV_K + 1 + j, S), :], axis=0, keepdims=True)

    return _pcall(
        body, name="conv_bwd", grid=(nct, B),
        in_specs=[pl.BlockSpec((S, LANES), lambda c, b: (b, c)), pl.BlockSpec((S, LANES), lambda c, b: (b, off_a + c)),
                  pl.BlockSpec((S, LANES), lambda c, b: (b, off_b + c)), pl.BlockSpec((CONV_K, LANES), lambda c, b: (0, c))],
        out_specs=[pl.BlockSpec((S, LANES), lambda c, b: (b, c)), pl.BlockSpec((S, LANES), lambda c, b: (b, c)),
                   pl.BlockSpec((CONV_K, LANES), lambda c, b: (0, c))],
        out_shape=[jax.ShapeDtypeStruct((B * S, C), BF16), jax.ShapeDtypeStruct((B * S, C), BF16),
                   jax.ShapeDtypeStruct((CONV_K, C), F32)],
        scratch_shapes=[pltpu.VMEM((HALO + S, LANES), F32), pltpu.VMEM((S + HALO, LANES), F32)],
    )(duc, z, z, cw)


def _attn_bwd_prep(dy, a, z, *, B, S, H, ts, sec_w):
    FW = H * HEAD_DIM
    nts = S // ts

    def body(dy_ref, a_ref, gf_ref, dgf_ref, do_ref):
        lane = _lane_iota(ts)
        for p in range(H // 2):
            cols = slice(p * LANES, (p + 1) * LANES)
            gf, av, dya = gf_ref[:, cols], a_ref[:, cols], dy_ref[:, cols]
            sg = _sig(gf)
            dgf_ref[:, cols] = (dya * av * _dsilu(gf, sg)).astype(BF16)
            da = dya * (gf * sg)
            nd = -_half_sums(da * av)
            dab = da.astype(BF16)
            for e in range(2):
                nd_e = nd[:, e * HEAD_DIM:e * HEAD_DIM + 1]
                do_ref[0, 2 * p + e] = _put3(_head_to_low(dab, e), lane, C0, _split3(nd_e)).astype(BF16)

    tok = lambda n: pl.BlockSpec((ts, FW), lambda b, i: (b * nts + i, n))
    return _pcall(
        body, name="attn_bwd_prep", grid=(B, nts),
        in_specs=[tok(0), tok(0), pl.BlockSpec((ts, sec_w), lambda b, i: (b * nts + i, 3))],
        out_specs=[tok(0), pl.BlockSpec((1, H, ts, LANES), lambda b, i: (b, 0, i, 0))],
        out_shape=[jax.ShapeDtypeStruct((B * S, FW), BF16), jax.ShapeDtypeStruct((B, H, S, LANES), BF16)],
    )(dy, a, z)


def _attn_bwd(qb, ka, va, doa, *, t, blk, dblk, G, comm=None):
    B, H, S, _ = qb.shape
    n = S // t
    assert t % blk == 0
    grid = (B, H // G, n)

    def body(*refs):
        (q_ref, k_ref, v_ref, do_ref), (dq_ref, dk_ref, dv_ref), _, rider = _split_refs(refs, 4, 3, comm)
        first, last = _grid_ends(grid)
        if comm:
            comm.begin(rider, first)
        j = pl.program_id(2)

        @pl.when(j == 0)
        def _():
            dq_ref[...] = jnp.zeros_like(dq_ref)

        ks = [k_ref[0, e] for e in range(G)]
        vs = [v_ref[0, e] for e in range(G)]

        def step(start, width, carry, shift=None):
            rows = pl.ds(pl.multiple_of(start, dblk), width)
            hi = t if shift is None else shift + width
            new = []
            for e in range(G):
                dk, dv = carry[e]
                k, v = ks[e][:hi], vs[e][:hi]
                q, do = q_ref[0, e, rows, :], do_ref[0, e, rows, :]
                st = _dot_nt(k, q)
                dpt = _dot_nt(v, do)
                if shift is not None:
                    row = lax.broadcasted_iota(jnp.int32, (hi, width), 0)
                    col = lax.broadcasted_iota(jnp.int32, (hi, width), 1)
                    st = jnp.where(row <= col + shift, st, NEG_INF)
                pt = jnp.exp(st)
                dsb = (pt * dpt).astype(BF16)
                dq_ref[0, e, rows, :] += _dot_tn(dsb, k)
                dk_new, dv_new = dk[:hi] + _dot(dsb, q), dv[:hi] + _dot(pt.astype(BF16), do)
                if hi < t:
                    dk_new, dv_new = jnp.concatenate([dk_new, dk[hi:]], axis=0), jnp.concatenate([dv_new, dv[hi:]], axis=0)
                new.append((dk_new, dv_new))
            return tuple(new)

        zero = jnp.zeros((t, LANES), F32)
        carry = tuple((zero, zero) for _ in range(G))
        for d in range(t // dblk):
            carry = step(j * t + d * dblk, dblk, carry, shift=d * dblk)
        carry = lax.fori_loop((j + 1) * (t // blk), S // blk, lambda i, c: step(i * blk, blk, c), carry)
        for e in range(G):
            dk_ref[0, e], dv_ref[0, e] = carry[e]
        if comm:
            comm.end(rider, last)

    full = pl.BlockSpec((1, G, S, LANES), lambda b, h, j: (b, h, 0, 0))
    tile = pl.BlockSpec((1, G, t, LANES), lambda b, h, j: (b, h, j, 0))
    shape = jax.ShapeDtypeStruct((B, H, S, LANES), F32)
    return _pcall(
        body, name="attn_bwd", grid=grid,
        **_with_comm(comm, [full, tile, tile, full], [full, tile, tile], [shape, shape, shape], []),
    )(qb, ka, va, doa, *(comm.operands() if comm else ()))


def _attn_post_bwd(dq, dk, dv, z, zf, bfp, gq, gk, *, B, S, H, ts):
    FW = H * HEAD_DIM
    nts = S // ts
    scale = HEAD_DIM ** -0.5

    def body(dq_ref, dk_ref, dv_ref, zq_ref, zk_ref, zf_ref, bf_ref, gq_ref, gk_ref,
             dzq_ref, dzk_ref, dzv_ref, dzf_ref, small_ref, carry):
        first = (pl.program_id(0) == 0) & (pl.program_id(1) == 0)

        @pl.when(first)
        def _():
            small_ref[...] = jnp.zeros_like(small_ref)

        @pl.when(pl.program_id(1) == 0)
        def _():
            carry[...] = jnp.zeros_like(carry)

        lane = _lane_iota(ts)
        dcs = jnp.zeros((ts, LANES), F32)
        for p in range(H // 2):
            cols = slice(p * LANES, (p + 1) * LANES)
            for d_ref, z_ref, g_ref, o_ref, srow, mul in ((dq_ref, zq_ref, gq_ref, dzq_ref, 0, scale),
                                                          (dk_ref, zk_ref, gk_ref, dzk_ref, 1, 1.0)):
                dy = jnp.where(lane < HEAD_DIM, d_ref[0, 2 * p], pltpu.roll(d_ref[0, 2 * p + 1], HEAD_DIM, 1)) * mul
                xv, g = z_ref[:, cols], g_ref[:, cols]
                r = lax.rsqrt(_half_sums(xv * xv) * (1.0 / HEAD_DIM) + EPS)
                dxh = dy * g
                mm = _half_sums(dxh * xv) * (1.0 / HEAD_DIM)
                o_ref[:, cols] = (r * (dxh - xv * (r * r) * mm)).astype(BF16)
                small_ref[8 * srow:8 * srow + 1, cols] += jnp.sum(dy * xv * r, axis=0, keepdims=True)
            dzv_ref[:, cols] = jnp.where(lane < HEAD_DIM, dv_ref[0, 2 * p], pltpu.roll(dv_ref[0, 2 * p + 1], HEAD_DIM, 1)).astype(BF16)
            for e in range(2):
                h = 2 * p + e
                dc = jnp.sum(jnp.where(lane == C0, dq_ref[0, h], 0.0) - jnp.where(lane == K0, dk_ref[0, h], 0.0),
                             axis=-1, keepdims=True)
                dcs = jnp.where(lane == h, dc, dcs)
        ri = lax.broadcasted_iota(jnp.int32, (ts, ts), 0)
        ci = lax.broadcasted_iota(jnp.int32, (ts, ts), 1)
        tri = (ci >= ri).astype(BF16)
        dl = carry[...]
        for piece in _split3(dcs):
            dl = dl + _dot(tri, piece)
        carry[...] = dl[0:1, :]
        xf = zf_ref[...] + bf_ref[...]
        df = jnp.where(lane < H, dl * _sig(-xf), 0.0)
        dzf_ref[...] = df.astype(BF16)
        small_ref[16:17, 0:LANES] += jnp.sum(df, axis=0, keepdims=True)

    rev = lambda b, i: (b, 0, nts - 1 - i, 0)
    head = pl.BlockSpec((1, H, ts, LANES), rev)
    tokrow = lambda b, i: b * nts + nts - 1 - i
    sec = lambda n: pl.BlockSpec((ts, FW), lambda b, i: (tokrow(b, i), n))
    narrow = pl.BlockSpec((ts, LANES), lambda b, i: (tokrow(b, i), 0))
    row = lambda w: pl.BlockSpec((1, w), lambda b, i: (0, 0))
    return _pcall(
        body, name="attn_post_bwd", grid=(B, nts),
        in_specs=[head, head, head, sec(0), sec(1), narrow, row(LANES), row(FW), row(FW)],
        out_specs=[sec(0), sec(0), sec(0), narrow, pl.BlockSpec((24, FW), lambda b, i: (0, 0))],
        out_shape=[jax.ShapeDtypeStruct((B * S, FW), BF16)] * 3 + [jax.ShapeDtypeStruct((B * S, LANES), BF16),
                                                                   jax.ShapeDtypeStruct((24, FW), F32)],
        scratch_shapes=[pltpu.VMEM((1, LANES), F32)],
    )(dq, dk, dv, z, z, zf, bfp, gq, gk)


def _inproj_bwd_x(dzs, dzf, wmt, wft, x, g, dout, *, tm, comm=None):
    T, D = x.shape
    nj = len(dzs)
    tn = dzs[0].shape[1]
    grid = (T // tm, nj)

    def body(*refs):
        ins, (gx_ref, dg_ref), _, rider = _split_refs(refs, nj + 6, 2, comm)
        acc = gx_ref
        dz_refs = ins[:nj]
        dzf_ref, wm_ref, wf_ref, x_ref, g_ref, dout_ref = ins[nj:]
        first, last = _grid_ends(grid)
        if comm:
            comm.begin(rider, first)
        i, j = pl.program_id(0), pl.program_id(1)

        @pl.when((i == 0) & (j == 0))
        def _():
            dg_ref[...] = jnp.zeros_like(dg_ref)

        @pl.when(j == 0)
        def _():
            acc[...] = _dot(dzf_ref[...], wf_ref[...])

        for s in range(nj):
            @pl.when(j == s)
            def _():
                acc[...] += _dot(dz_refs[s][...], wm_ref[...])

        @pl.when(j == nj - 1)
        def _():
            dh, xv, g = acc[...], x_ref[...], g_ref[...]
            r = lax.rsqrt(jnp.mean(xv * xv, axis=-1, keepdims=True) + EPS)
            dxh = dh * g
            mm = jnp.mean(dxh * xv, axis=-1, keepdims=True)
            gx_ref[...] = dout_ref[...] + r * (dxh - xv * (r * r) * mm)
            dg_ref[0:1, :] += jnp.sum(dh * xv * r, axis=0, keepdims=True)

        if comm:
            comm.end(rider, last)

    tok = pl.BlockSpec((tm, D), lambda i, j: (i, 0))
    sec = pl.BlockSpec((tm, tn), lambda i, j: (i, 0))
    return _pcall(
        body, name="inproj_bwd_x", grid=grid,
        **_with_comm(
            comm,
            [sec] * nj + [pl.BlockSpec((tm, LANES), lambda i, j: (i, 0)), pl.BlockSpec((tn, D), lambda i, j: (j, 0)),
                          pl.BlockSpec((LANES, D), lambda i, j: (0, 0)), tok, pl.BlockSpec((1, D), lambda i, j: (0, 0)), tok],
            [tok, pl.BlockSpec((8, D), lambda i, j: (0, 0))],
            [jax.ShapeDtypeStruct((T, D), F32), jax.ShapeDtypeStruct((8, D), F32)],
            []),
    )(*dzs, dzf, wmt, wft, x, g, dout, *(comm.operands() if comm else ()))


def _inproj_bwd_w(h, dz, *, tm, name):
    T, D = h.shape
    W = dz.shape[1]
    ni = T // tm

    def body(h_ref, dz_ref, dw_ref, acc):
        i = pl.program_id(0)

        @pl.when(i == 0)
        def _():
            acc[...] = jnp.zeros_like(acc)

        acc[...] += _dot_tn(dz_ref[...], h_ref[...])

        @pl.when(i == ni - 1)
        def _():
            dw_ref[...] = acc[...].astype(BF16)

    return _pcall(
        body, name=name, grid=(ni,),
        in_specs=[pl.BlockSpec((tm, D), lambda i: (i, 0)), pl.BlockSpec((tm, W), lambda i: (i, 0))],
        out_specs=pl.BlockSpec((W, D), lambda i: (0, 0)), out_shape=jax.ShapeDtypeStruct((W, D), BF16),
        scratch_shapes=[pltpu.VMEM((W, D), F32)],
    )(h, dz)


def _adamw(w, g, m, v):
    m = ADAM_B1 * m + (1.0 - ADAM_B1) * g
    v = ADAM_B2 * v + (1.0 - ADAM_B2) * (g * g)
    m_hat = m / (1.0 - ADAM_B1 ** ADAM_STEP)
    v_hat = v / (1.0 - ADAM_B2 ** ADAM_STEP)
    return -ADAM_LR * (m_hat / (jnp.sqrt(v_hat) + ADAM_EPS) + ADAM_WD * w), m, v


def _adam_update(w, m, v, parts, *, name, tr):
    R, Cc = w.shape[0], w.shape[-1]

    def body(w_ref, m_ref, v_ref, p_ref, g_ref, d_ref, nm_ref, nv_ref):
        g = p_ref[0].astype(F32)
        for k in range(1, N_DEV):
            g = g + p_ref[k].astype(F32)
        g_ref[...] = g
        d_ref[...], nm_ref[...], nv_ref[...] = _adamw(w_ref[...], g, m_ref[...], v_ref[...])

    blk = pl.BlockSpec((tr, Cc), lambda i: (i, 0)) if w.ndim == 2 else pl.BlockSpec((tr, None, Cc), lambda i: (i, 0, 0))
    shape = jax.ShapeDtypeStruct(w.shape, F32)
    return _pcall(
        body, name=name, grid=(pl.cdiv(R, tr),),
        in_specs=[blk, blk, blk, pl.BlockSpec((N_DEV, tr, Cc), lambda i: (0, i, 0))],
        out_specs=[blk, blk, blk, blk], out_shape=[shape, shape, shape, shape],
    )(w, m, v, parts)


MESH = pl.DeviceIdType.MESH
ANY = pl.BlockSpec(memory_space=pl.ANY)


def _flip(v, bit):
    return 1 - v if bit else v


def _allgather(shards):
    n = len(shards)

    def body(*refs):
        srcs, outs = refs[:n], refs[n:2 * n]
        send_sems, recv_sems, local_sems = refs[2 * n:]
        x, y, c = lax.axis_index("x"), lax.axis_index("y"), lax.axis_index("c")
        sibling = (x, y, 1 - c)
        chips = [(1 - x, y), (x, 1 - y), (1 - x, 1 - y)]
        slot = lambda px, py, pc: 4 * px + 2 * py + pc

        def copy(a, k, block, to, src=None):
            dst = outs[a].at[slot(*block)]
            return pltpu.make_async_remote_copy(src_ref=dst if src is None else src, dst_ref=dst, send_sem=send_sems.at[7 * a + k],
                                                recv_sem=recv_sems.at[7 * a + k], device_id=to, device_id_type=MESH)

        mine = [pltpu.make_async_copy(srcs[a], outs[a].at[slot(x, y, c)], local_sems.at[a]) for a in range(n)]
        sends = []
        for a in range(n):
            mine[a].start()
            sends.append(copy(a, 0, (x, y, c), sibling, src=srcs[a]))
            sends += [copy(a, 1 + j, (x, y, c), (*chip, c), src=srcs[a]) for j, chip in enumerate(chips)]
        for cp in sends:
            cp.start()
        for a in range(n):
            for j, chip in enumerate(chips):
                copy(a, 1 + j, (*chip, c), (x, y, c)).wait_recv()
                fwd = copy(a, 4 + j, (*chip, c), sibling)
                fwd.start()
                sends.append(fwd)
        for a in range(n):
            copy(a, 0, (x, y, 1 - c), (x, y, c)).wait_recv()
            for j, chip in enumerate(chips):
                copy(a, 4 + j, (*chip, 1 - c), (x, y, c)).wait_recv()
        for cp in sends:
            cp.wait_send()
        for cp in mine:
            cp.wait()

    return pl.pallas_call(
        body, name="allgather_weights", in_specs=[ANY] * n, out_specs=[ANY] * n,
        out_shape=[jax.ShapeDtypeStruct((N_DEV,) + s.shape, s.dtype) for s in shards],
        scratch_shapes=[pltpu.SemaphoreType.DMA((7 * n,)), pltpu.SemaphoreType.DMA((7 * n,)), pltpu.SemaphoreType.DMA((n,))],
    )(*shards)


class _Comm:
    def __init__(self, arrays, owners, into=None):
        self.arrays, self.owners, self.n = list(arrays), list(owners), len(arrays)
        self.into = list(into) if into else [None] * self.n
        self.carried = [t for t in self.into if t is not None]

    def operands(self):
        return self.arrays + self.carried

    def aliases(self, n_in, n_out):
        out, pos = {}, 0
        for a, t in enumerate(self.into):
            if t is not None:
                out[n_in + self.n + pos] = n_out + a
                pos += 1
        return out

    def out_shape(self):
        return [jax.ShapeDtypeStruct((N_DEV,) + (a.shape if o is None else a.shape[1:]), a.dtype)
                for a, o in zip(self.arrays, self.owners)]

    def scratch(self):
        return [pltpu.SemaphoreType.DMA((7 * self.n,)), pltpu.SemaphoreType.DMA((7 * self.n,)), pltpu.SemaphoreType.DMA((self.n,))]

    def ops(self, srcs, outs, send_sems, recv_sems, local_sems):
        x, y, c = lax.axis_index("x"), lax.axis_index("y"), lax.axis_index("c")
        me = 4 * x + 2 * y + c
        when = lambda p, f: (lambda: pl.when(p)(f))
        starts, waits = [], []
        for a, owners in enumerate(self.owners):
            src, out = srcs[a], outs[a]

            def remote(piece, slot, rel, to, a=a, out=out):
                sem = 7 * a + rel - 1
                return pltpu.make_async_remote_copy(src_ref=piece, dst_ref=out.at[slot], send_sem=send_sems.at[sem],
                                                    recv_sem=recv_sems.at[sem], device_id=to, device_id_type=MESH)

            peers = [(d, (_flip(x, d & 4), _flip(y, d & 2), _flip(c, d & 1))) for d in range(1, N_DEV)]
            if owners is None:
                local = pltpu.make_async_copy(src, out.at[me], local_sems.at[a])
                starts.append(local.start)
                waits.append(local.wait)
                for d, peer in peers:
                    cp = remote(src, me, d, peer)
                    starts.append(cp.start)
                    waits.append(cp.wait_send)
                receives = True
            else:
                for s, dest in enumerate(owners):
                    to = (dest >> 2 & 1, dest >> 1 & 1, dest & 1)
                    rel = 4 * _flip(x, to[0]) + 2 * _flip(y, to[1]) + _flip(c, to[2])
                    cp = remote(src.at[s], me, jnp.maximum(rel, 1), to)
                    local = pltpu.make_async_copy(src.at[s], out.at[me], local_sems.at[a])
                    starts += [when(rel != 0, cp.start), when(rel == 0, local.start)]
                    waits += [when(rel != 0, cp.wait_send), when(rel == 0, local.wait)]
                receives = me == owners[0]
                for dest in owners[1:]:
                    receives = receives | (me == dest)
            piece = src if owners is None else src.at[0]
            for d, peer in peers:
                def arrival(d=d, peer=peer, piece=piece, remote=remote):
                    remote(piece, 4 * peer[0] + 2 * peer[1] + peer[2], d, peer).wait_recv()

                waits.append(arrival if receives is True else when(receives, arrival))
        return starts, waits

    def _phase(self, which, rider, cond):
        srcs, outs, sems = rider

        def go():
            for op in self.ops(srcs, outs, *sems)[which]:
                op()

        go() if cond is True else pl.when(cond)(go)

    def begin(self, rider, first=True):
        self._phase(0, rider, first)

    def end(self, rider, last=True):
        self._phase(1, rider, last)


def _split_refs(refs, n_in, n_out, comm):
    k = comm.n if comm else 0
    o0 = n_in + k + (len(comm.carried) if comm else 0)
    ins, c_src = refs[:n_in], refs[n_in:n_in + k]
    outs, c_out = refs[o0:o0 + n_out], refs[o0 + n_out:o0 + n_out + k]
    rest = refs[o0 + n_out + k:]
    scratch, c_sem = (rest[:-3], rest[-3:]) if comm else (rest, ())
    return ins, outs, scratch, (c_src, c_out, c_sem)


def _grid_ends(grid):
    first = functools.reduce(jnp.logical_and, [pl.program_id(a) == 0 for a in range(len(grid))])
    last = functools.reduce(jnp.logical_and, [pl.program_id(a) == g - 1 for a, g in enumerate(grid)])
    return first, last


def _with_comm(comm, in_specs, out_specs, out_shape, scratch):
    if comm is None:
        return dict(in_specs=in_specs, out_specs=out_specs, out_shape=out_shape, scratch_shapes=scratch)
    return dict(in_specs=in_specs + [ANY] * len(comm.operands()), out_specs=out_specs + [ANY] * comm.n,
                out_shape=out_shape + comm.out_shape(), scratch_shapes=scratch + comm.scratch(),
                input_output_aliases=comm.aliases(len(in_specs), len(out_specs)))


def _exchange(comm, *, name):
    n = comm.n

    def body(*refs):
        rider = (refs[:n], refs[n:2 * n], refs[2 * n:])
        comm.begin(rider)
        comm.end(rider)

    return pl.pallas_call(body, name=name, in_specs=[ANY] * n, out_specs=[ANY] * n, out_shape=comm.out_shape(),
                          scratch_shapes=comm.scratch())(*comm.arrays)


def kernel(x, norm_g, w_in, b_forget, q_norm_g, k_norm_g, conv_w, conv_b, conv_ln_g, conv_ln_b, w_out, loss_target, m_norm_g, m_w_in, m_b_forget, m_q_norm_g, m_k_norm_g, m_conv_w, m_conv_b, m_conv_ln_g, m_conv_ln_b, m_w_out, v_norm_g, v_w_in, v_b_forget, v_q_norm_g, v_k_norm_g, v_conv_w, v_conv_b, v_conv_ln_g, v_conv_ln_b, v_w_out):
    B, S, D = x.shape
    H = q_norm_g.shape[1]
    FW = H * HEAD_DIM
    assert q_norm_g.shape[2] == HEAD_DIM and FW == D and conv_b.shape[1] == D and H <= LANES // 3
    ncol = w_in.shape[2]
    T = B * S

    to_t = lambda t: jnp.transpose(t, (2, 0, 1))
    from_t = lambda t: jnp.transpose(t, (1, 2, 0))
    wt, mt, vt = to_t(w_in), to_t(m_w_in), to_t(v_w_in)

    x2, tgt = x.reshape(T, D), loss_target.reshape(T, D)
    tm, ts, rc = min(512, T), min(256, S), min(64, S)
    tw = min(1024, T)
    everyone = list(range(N_DEV))
    bfp = jnp.pad(b_forget, ((0, 0), (0, LANES - H)))
    gq, gk = q_norm_g.reshape(1, FW), k_norm_g.reshape(1, FW)

    (wg,) = _allgather([wt.reshape(ncol, D).astype(BF16)])
    wmt, wft = _w_relayout(wg, D=D, H=H)
    z, zf, h, wog, cwg = _inproj_fwd(x2, norm_g, wmt, wft, tm=min(1024, T), tn=FW,
                                     comm=_Comm([w_out[0].astype(BF16), conv_w[0]], [None, None]))
    wo = wog.reshape(N_DEV * wog.shape[1], D)
    cw = jnp.transpose(cwg, (1, 0, 2)).reshape(CONV_K, D)
    qa, ka, va = _attn_prep(z, zf, bfp, gq, gk, B=B, S=S, H=H, ts=ts)
    a, qb = _attn_fwd(qa, ka, va, t=min(512, S), blk=min(512, S), dblk=min(512, S), G=4)
    uc = _conv_fwd(z, cw, conv_b, B=B, S=S, sec_w=FW, rc=rc)
    uo = _conv_post(uc, z, conv_ln_g, conv_ln_b, tm=ts, sec_w=FW)
    dout, dy, dwo, loss = _outproj(a, z, uo, x2, tgt, wo, tm=ts, sec_w=FW)

    dgc, duc, small_c = _conv_post_bwd(dy, uc, z, conv_ln_g, conv_ln_b, tm=ts, sec_w=FW)
    dga, dgb, dcw = _conv_bwd(duc, z, cw, B=B, S=S, sec_w=FW, rc=rc)
    dgf, doa = _attn_bwd_prep(dy, a, z, B=B, S=S, H=H, ts=ts, sec_w=FW)
    _, f1, _ = _section_rows(D, H)
    late_rows = [k for k in everyone if k * ncol >= f1]
    early_rows = [k for k in everyone if k not in late_rows]
    first_late_sec = (late_rows[0] * ncol - H) // FW
    dws = {s: _inproj_bwd_w(h, dz, tm=tw, name=f"inproj_bwd_w{s}") for s, dz in ((3, dgf), (4, dga), (5, dgb), (6, dgc))}
    dw_late = _dw_relayout({s: dws[s] for s in range(first_late_sec, 7)}, None, R=ncol, H=H, shards=late_rows, name="dw_relayout_late")
    dwo_parts = dwo.reshape(N_DEV, dwo.shape[0] // N_DEV, D).astype(BF16)
    dcw_parts = jnp.transpose(dcw.reshape(CONV_K, N_DEV, D // N_DEV), (1, 0, 2))
    dq, dk, dv, r_w, r_wo, r_cw = _attn_bwd(qb, ka, va, doa, t=min(512, S), blk=min(512, S), dblk=min(512, S), G=4,
                                            comm=_Comm([dw_late, dwo_parts, dcw_parts], [late_rows, everyone, everyone]))
    dzq, dzk, dzv, dzf, small_a = _attn_post_bwd(dq, dk, dv, z, zf, bfp, gq, gk, B=B, S=S, H=H, ts=ts)
    dws.update({s: _inproj_bwd_w(h, dz, tm=tw, name=f"inproj_bwd_w{s}") for s, dz in ((0, dzq), (1, dzk), (2, dzv))})
    dwf = _inproj_bwd_w(h, dzf, tm=tw, name="inproj_bwd_wf")
    last_early_sec = (early_rows[-1] * ncol + ncol - 1 - H) // FW
    dw_early = _dw_relayout({s: dws[s] for s in range(last_early_sec + 1)}, dwf, R=ncol, H=H, shards=early_rows, name="dw_relayout_early")
    gx, dg, r_w = _inproj_bwd_x([dzq, dzk, dzv, dgf, dga, dgb, dgc], dzf, wmt, wft, x2, norm_g, dout, tm=tm,
                                comm=_Comm([dw_early], [early_rows], into=[r_w]))
    small = jnp.concatenate([dg, small_a[16:24], small_a[0:8], small_a[8:16], small_c[16:24], small_c[0:8], small_c[8:16],
                             jnp.pad(loss, ((0, 0), (0, D - LANES)))], axis=0)
    (r_small,) = _exchange(_Comm([small], [None]), name="exchange_small")

    g_w, d_w, nm_w, nv_w = [from_t(t) for t in _adam_update(wt, mt, vt, r_w, name="adam_w_in", tr=min(128, D))]
    g_wo, d_wo, nm_wo, nv_wo = [t[None] for t in _adam_update(w_out[0], m_w_out[0], v_w_out[0], r_wo, name="adam_w_out",
                                                               tr=min(128, w_out.shape[1]))]
    g_cw, d_cw, nm_cw, nv_cw = [t[None] for t in _adam_update(conv_w[0], m_conv_w[0], v_conv_w[0], r_cw, name="adam_conv_w", tr=CONV_K)]
    tile = lambda t: jnp.pad(t.reshape(1, -1), ((0, 7), (0, D - t.size)))
    pack = lambda *ps: jnp.concatenate([tile(p) for p in ps] + [jnp.zeros((8, D), F32)], axis=0)
    small_w = pack(norm_g, b_forget, q_norm_g, k_norm_g, conv_b, conv_ln_g, conv_ln_b)
    small_m = pack(m_norm_g, m_b_forget, m_q_norm_g, m_k_norm_g, m_conv_b, m_conv_ln_g, m_conv_ln_b)
    small_v = pack(v_norm_g, v_b_forget, v_q_norm_g, v_k_norm_g, v_conv_b, v_conv_ln_g, v_conv_ln_b)
    g_s, d_s, nm_s, nv_s = _adam_update(small_w, small_m, small_v, r_small, name="adam_small", tr=small_w.shape[0])

    def leaves(small_t, w_t, cw_t, wo_t):
        row = lambda r, like: small_t[8 * r:8 * r + 1, :like.size].reshape(like.shape)
        return [row(0, norm_g), w_t, row(1, b_forget), row(2, q_norm_g), row(3, k_norm_g), cw_t, row(4, conv_b),
                row(5, conv_ln_g), row(6, conv_ln_b), wo_t]

    return (g_s[56, 0], gx.reshape(B, S, D), *leaves(g_s, g_w, g_cw, g_wo), *leaves(d_s, d_w, d_cw, d_wo),
            *leaves(nm_s, nm_w, nm_cw, nm_wo), *leaves(nv_s, nv_w, nv_cw, nv_wo))
```

```python
import functools

import jax
import jax.numpy as jnp
from jax import lax
from jax.experimental import pallas as pl
from jax.experimental.pallas import tpu as pltpu

F32, BF16 = jnp.float32, jnp.bfloat16
EPS = 1e-6
NEG_INF = -1e30
CONV_K = 31
HEAD_DIM = 64
ADAM_LR, ADAM_B1, ADAM_B2, ADAM_EPS, ADAM_WD, ADAM_STEP = 0.001, 0.9, 0.999, 1e-08, 0.01, 10

LANES = 128
HALO = 32
N_DEV = 8
VMEM_LIMIT = 56 * 1024 * 1024

C0, K0, L0 = HEAD_DIM, HEAD_DIM + 3, HEAD_DIM + 6


def _pcall(body, *, name, vmem=VMEM_LIMIT, **kw):
    return pl.pallas_call(body, name=name, compiler_params=pltpu.CompilerParams(vmem_limit_bytes=vmem), **kw)


def _dot(a, b):
    return jnp.dot(a, b, preferred_element_type=F32)


def _dot_nt(a, b):
    return lax.dot_general(a, b, (((1,), (1,)), ((), ())), preferred_element_type=F32)


def _dot_tn(a, b):
    return lax.dot_general(a, b, (((0,), (0,)), ((), ())), preferred_element_type=F32)


def _split3(x):
    a = x.astype(BF16)
    r = x - a.astype(F32)
    b = r.astype(BF16)
    c = (r - b.astype(F32)).astype(BF16)
    return a, b, c


def _sig(x):
    return jax.nn.sigmoid(x)


def _dsilu(x, s):
    return s * (1.0 + x * (1.0 - s))


def _lane_iota(rows):
    return lax.broadcasted_iota(jnp.int32, (rows, LANES), 1)


def _half_sums(v):
    r = lax.broadcasted_iota(jnp.int32, (LANES, LANES), 0) < HEAD_DIM
    c = lax.broadcasted_iota(jnp.int32, (LANES, LANES), 1) < HEAD_DIM
    ones = (r == c).astype(BF16)
    a, b, d = _split3(v)
    return _dot(a, ones) + _dot(b, ones) + _dot(d, ones)


def _head_to_low(vb, e):
    r = lax.broadcasted_iota(jnp.int32, (LANES, LANES), 0)
    c = lax.broadcasted_iota(jnp.int32, (LANES, LANES), 1)
    return _dot(vb, ((r == c + e * HEAD_DIM) & (c < HEAD_DIM)).astype(BF16))


def _head_to_low_roll(v, e, lane):
    if e:
        v = pltpu.roll(v, HEAD_DIM, 1)
    return jnp.where(lane < HEAD_DIM, v, 0.0)


def _put3(base, lane, first, pieces):
    for n, p in enumerate(pieces):
        base = jnp.where(lane == first + n, p.astype(F32), base)
    return base


def _section_rows(D, H):
    FW = H * HEAD_DIM
    return 3 * FW, 3 * FW + H, 7 * FW + H


def _row_pieces(a, b, R, step):
    out = []
    while a < b:
        k = a // R
        lo = a - k * R
        hi = min(R, lo + min(step, b - a))
        out.append((k, lo, hi))
        a += hi - lo
    return out


def _w_relayout(wg, *, D, H):
    R = wg.shape[1]
    FW = H * HEAD_DIM
    f0, f1, end = _section_rows(D, H)

    def body(wg_ref, wm_ref, wf_ref):
        def put(dst_ref, d0, a, b):
            for k, lo, hi in _row_pieces(a, b, R, 256):
                dst_ref[d0:d0 + hi - lo, :] = wg_ref[k, lo:hi, :]
                d0 += hi - lo

        put(wm_ref, 0, 0, f0)
        put(wm_ref, f0, f1, end)
        wf_ref[...] = jnp.zeros_like(wf_ref)
        put(wf_ref, 0, f0, f1)

    return _pcall(body, name="w_relayout",
                  out_shape=[jax.ShapeDtypeStruct((7 * FW, D), BF16), jax.ShapeDtypeStruct((LANES, D), BF16)])(wg)


def _dw_relayout(secs, dwf, *, R, H, shards, name):
    FW, D = next(iter(secs.values())).shape
    f0, f1, end = _section_rows(D, H)
    order = sorted(secs)

    def body(*refs):
        sec_refs = dict(zip(order, refs))
        dwf_ref, out_ref = (refs[-2] if dwf is not None else None), refs[-1]

        def src(g0, g1):
            if f0 <= g0 < f1:
                return dwf_ref[g0 - f0:g1 - f0, :]
            s0 = g0 if g0 < f0 else g0 - H
            return sec_refs[s0 // FW][s0 % FW:s0 % FW + (g1 - g0), :]

        cuts = sorted({0, end, f0, f1} | {s * FW for s in range(4)} | {f1 + s * FW for s in range(5)})
        for slot, k in enumerate(shards):
            g = k * R
            while g < (k + 1) * R:
                nxt = min(min(c for c in cuts if c > g), (k + 1) * R, g + 256)
                out_ref[slot, g - k * R:nxt - k * R, :] = src(g, nxt)
                g = nxt

    args = [secs[s] for s in order] + ([dwf] if dwf is not None else [])
    return _pcall(body, name=name, out_shape=jax.ShapeDtypeStruct((len(shards), R, D), BF16))(*args)


def _inproj_fwd(x, g, wmt, wft, *, tm, tn, comm=None):
    T, D = x.shape
    nj = wmt.shape[0] // tn
    grid = (T // tm, nj)

    def body(*refs):
        (x_ref, g_ref, wm_ref, wf_ref), (z_ref, zf_ref, h_ref), (h_scr,), rider = _split_refs(refs, 4, 3, comm)
        first, last = _grid_ends(grid)
        if comm:
            comm.begin(rider, first)

        @pl.when(pl.program_id(1) == 0)
        def _():
            xv = x_ref[...]
            r = lax.rsqrt(jnp.mean(xv * xv, axis=-1, keepdims=True) + EPS)
            h = ((xv * r) * g_ref[...]).astype(BF16)
            h_scr[...] = h
            h_ref[...] = h
            zf_ref[...] = _dot_nt(h, wf_ref[...])

        z_ref[...] = _dot_nt(h_scr[...], wm_ref[...])
        if comm:
            comm.end(rider, last)

    return _pcall(
        body, name="inproj_fwd", grid=grid,
        **_with_comm(
            comm,
            [pl.BlockSpec((tm, D), lambda i, j: (i, 0)), pl.BlockSpec((1, D), lambda i, j: (0, 0)),
             pl.BlockSpec((tn, D), lambda i, j: (j, 0)), pl.BlockSpec((LANES, D), lambda i, j: (0, 0))],
            [pl.BlockSpec((tm, tn), lambda i, j: (i, j)), pl.BlockSpec((tm, LANES), lambda i, j: (i, 0)),
             pl.BlockSpec((tm, D), lambda i, j: (i, 0))],
            [jax.ShapeDtypeStruct((T, wmt.shape[0]), F32), jax.ShapeDtypeStruct((T, LANES), F32), jax.ShapeDtypeStruct((T, D), BF16)],
            [pltpu.VMEM((tm, D), BF16)]),
    )(x, g, wmt, wft, *(comm.operands() if comm else ()))


def _attn_prep(z, zf, bfp, gq, gk, *, B, S, H, ts):
    FW = H * HEAD_DIM
    nts = S // ts
    scale = HEAD_DIM ** -0.5

    def body(zq_ref, zk_ref, zv_ref, zf_ref, bf_ref, gq_ref, gk_ref, qa_ref, ka_ref, va_ref, carry):
        @pl.when(pl.program_id(1) == 0)
        def _():
            carry[...] = jnp.zeros_like(carry)

        lane = _lane_iota(ts)
        xf = zf_ref[...] + bf_ref[...]
        logf = jnp.minimum(xf, 0.0) - jnp.log1p(jnp.exp(-jnp.abs(xf)))
        logf = jnp.where(lane < H, logf, 0.0)
        ri = lax.broadcasted_iota(jnp.int32, (ts, ts), 0)
        ci = lax.broadcasted_iota(jnp.int32, (ts, ts), 1)
        tri = (ri >= ci).astype(BF16)
        c = carry[...]
        for piece in _split3(logf):
            c = c + _dot(tri, piece)
        carry[...] = c[ts - 1:ts, :]
        c1, c2, c3 = _split3(c)
        cpk = c1.astype(F32) + pltpu.roll(c2.astype(F32), H, 1) + pltpu.roll(c3.astype(F32), 2 * H, 1)
        cq, ck = cpk.astype(BF16), (-cpk).astype(BF16)
        prow = lax.broadcasted_iota(jnp.int32, (LANES, LANES), 0)
        pcol = lax.broadcasted_iota(jnp.int32, (LANES, LANES), 1)
        ones_q = jnp.where((lane >= K0) & (lane < K0 + 3), 1.0, 0.0)
        ones_k = jnp.where(((lane >= C0) & (lane < C0 + 3)) | ((lane >= L0) & (lane < L0 + 3)), 1.0, 0.0)
        ones_v = jnp.where((lane >= C0) & (lane < C0 + 3), 1.0, 0.0)
        for h in range(H):
            p, e = divmod(h, 2)
            cols = slice(p * LANES, (p + 1) * LANES)
            piece = jnp.where(prow == h, 0, jnp.where(prow == H + h, 1, jnp.where(prow == 2 * H + h, 2, -LANES)))
            aug_q = _dot(cq, (pcol == C0 + piece).astype(BF16)) + ones_q
            aug_k = _dot(ck, (pcol == K0 + piece).astype(BF16)) + ones_k
            for z_ref, g_ref, o_ref, aug, mul in ((zq_ref, gq_ref, qa_ref, aug_q, scale), (zk_ref, gk_ref, ka_ref, aug_k, 1.0)):
                blk = z_ref[:, cols]
                r = lax.rsqrt(_half_sums(blk * blk) * (1.0 / HEAD_DIM) + EPS)
                y = ((blk * r) * g_ref[:, cols]).astype(BF16).astype(F32) * mul
                o_ref[0, h] = (_head_to_low_roll(y, e, lane) + aug).astype(BF16)
            va_ref[0, h] = (_head_to_low_roll(zv_ref[:, cols], e, lane) + ones_v).astype(BF16)

    sec = lambda n: pl.BlockSpec((ts, FW), lambda b, i: (b * nts + i, n))
    row = lambda w: pl.BlockSpec((1, w), lambda b, i: (0, 0))
    head = pl.BlockSpec((1, H, ts, LANES), lambda b, i: (b, 0, i, 0))
    hshape = jax.ShapeDtypeStruct((B, H, S, LANES), BF16)
    return _pcall(
        body, name="attn_prep", grid=(B, nts),
        in_specs=[sec(0), sec(1), sec(2), pl.BlockSpec((ts, LANES), lambda b, i: (b * nts + i, 0)), row(LANES), row(FW), row(FW)],
        out_specs=[head, head, head], out_shape=[hshape, hshape, hshape],
        scratch_shapes=[pltpu.VMEM((1, LANES), F32)],
    )(z, z, z, zf, bfp, gq, gk)


def _attn_fwd(qa, ka, va, *, t, blk, dblk, G):
    B, H, S, _ = qa.shape
    n = S // t
    assert t % blk == 0

    def body(q_ref, k_ref, v_ref, a_ref, qb_ref):
        i = pl.program_id(2)
        lane = _lane_iota(t)
        qs = [q_ref[0, e] for e in range(G)]

        def step(start, width, carry, q0=None):
            rows = pl.ds(pl.multiple_of(start, dblk), width)
            lo = q0 or 0
            new = []
            for e in range(G):
                m, acc = carry[e]
                s = _dot_nt(qs[e][lo:], k_ref[0, e, rows, :])
                if q0 is not None:
                    row = lax.broadcasted_iota(jnp.int32, (t - lo, width), 0)
                    col = lax.broadcasted_iota(jnp.int32, (t - lo, width), 1)
                    s = jnp.where(col <= row, s, NEG_INF)
                m_new = jnp.maximum(m[lo:], jnp.max(s, axis=-1, keepdims=True))
                p = jnp.exp(s - m_new)
                acc_new = jnp.exp(m[lo:] - m_new) * acc[lo:] + _dot(p.astype(BF16), v_ref[0, e, rows, :])
                if lo:
                    m_new, acc_new = jnp.concatenate([m[:lo], m_new], axis=0), jnp.concatenate([acc[:lo], acc_new], axis=0)
                new.append((m_new, acc_new))
            return tuple(new)

        carry = tuple((jnp.full((t, 1), NEG_INF, F32), jnp.zeros((t, LANES), F32)) for _ in range(G))
        carry = lax.fori_loop(0, i * (t // blk), lambda j, c: step(j * blk, blk, c), carry)
        for d in range(t // dblk):
            carry = step(i * t + d * dblk, dblk, carry, q0=d * dblk)
        outs = []
        for e in range(G):
            m, acc = carry[e]
            l = jnp.sum(jnp.where(lane == C0, acc, 0.0), axis=-1, keepdims=True)
            outs.append(acc / l)
            qb_ref[0, e] = _put3(qs[e].astype(F32), lane, L0, _split3(-(m + jnp.log(l)))).astype(BF16)
        for pp in range(G // 2):
            a_ref[:, pp * LANES:(pp + 1) * LANES] = jnp.where(lane < HEAD_DIM, outs[2 * pp], pltpu.roll(outs[2 * pp + 1], HEAD_DIM, 1))

    return _pcall(
        body, name="attn_fwd", grid=(B, H // G, n),
        in_specs=[pl.BlockSpec((1, G, t, LANES), lambda b, p, i: (b, p, i, 0)),
                  pl.BlockSpec((1, G, S, LANES), lambda b, p, i: (b, p, 0, 0)),
                  pl.BlockSpec((1, G, S, LANES), lambda b, p, i: (b, p, 0, 0))],
        out_specs=[pl.BlockSpec((t, G * HEAD_DIM), lambda b, p, i: (b * n + i, p)),
                   pl.BlockSpec((1, G, t, LANES), lambda b, p, i: (b, p, i, 0))],
        out_shape=[jax.ShapeDtypeStruct((B * S, H * HEAD_DIM), F32), jax.ShapeDtypeStruct((B, H, S, LANES), BF16)],
    )(qa, ka, va)


def _conv_fwd(z, cw, cb, *, B, S, sec_w, rc):
    C = cw.shape[1]
    nct = C // LANES
    off_a, off_b = 4 * sec_w // LANES, 5 * sec_w // LANES

    def body(ga_ref, gb_ref, w_ref, b_ref, uc_ref, buf):
        buf[0:HALO, :] = jnp.zeros((HALO, LANES), F32)
        buf[HALO:HALO + S, :] = ga_ref[...] * _sig(gb_ref[...])

        def chunk(cidx, carry):
            r0 = pl.multiple_of(cidx * rc, rc)
            acc = jnp.zeros((rc, LANES), F32)
            for j in range(CONV_K):
                acc = acc + w_ref[j:j + 1, :] * buf[pl.ds(r0 + (HALO - CONV_K + 1 + j), rc), :]
            uc_ref[pl.ds(r0, rc), :] = acc + b_ref[...]
            return carry

        lax.fori_loop(0, S // rc, chunk, 0)

    return _pcall(
        body, name="conv_fwd", grid=(B, nct),
        in_specs=[pl.BlockSpec((S, LANES), lambda b, c: (b, off_a + c)), pl.BlockSpec((S, LANES), lambda b, c: (b, off_b + c)),
                  pl.BlockSpec((CONV_K, LANES), lambda b, c: (0, c)), pl.BlockSpec((1, LANES), lambda b, c: (0, c))],
        out_specs=pl.BlockSpec((S, LANES), lambda b, c: (b, c)),
        out_shape=jax.ShapeDtypeStruct((B * S, C), F32),
        scratch_shapes=[pltpu.VMEM((HALO + S, LANES), F32)],
    )(z, z, cw, cb)


def _ln_parts(uc, lg, lb):
    mu = jnp.mean(uc, axis=-1, keepdims=True)
    d = uc - mu
    rs = lax.rsqrt(jnp.mean(d * d, axis=-1, keepdims=True) + EPS)
    un = d * rs
    return rs, un, un * lg + lb


def _conv_post(uc, z, lg, lb, *, tm, sec_w):
    T, C = uc.shape
    off_c = 6 * sec_w // C

    def body(uc_ref, gc_ref, lg_ref, lb_ref, uo_ref):
        _, _, ul = _ln_parts(uc_ref[...], lg_ref[...], lb_ref[...])
        gc = gc_ref[...]
        uo_ref[...] = ((ul * _sig(ul)) * (gc * _sig(gc))).astype(BF16)

    return _pcall(
        body, name="conv_post", grid=(T // tm,),
        in_specs=[pl.BlockSpec((tm, C), lambda i: (i, 0)), pl.BlockSpec((tm, C), lambda i: (i, off_c)),
                  pl.BlockSpec((1, C), lambda i: (0, 0)), pl.BlockSpec((1, C), lambda i: (0, 0))],
        out_specs=pl.BlockSpec((tm, C), lambda i: (i, 0)), out_shape=jax.ShapeDtypeStruct((T, C), BF16),
    )(uc, z, lg, lb)


def _outproj(a, z, uo, x, tgt, wo, *, tm, sec_w):
    T, D = x.shape
    FW = a.shape[1]
    C = uo.shape[1]

    def body(a_ref, gf_ref, uo_ref, x_ref, t_ref, wo_ref, dout_ref, dy_ref, dwo_ref, loss_ref):
        @pl.when(pl.program_id(0) == 0)
        def _():
            dwo_ref[...] = jnp.zeros_like(dwo_ref)
            loss_ref[...] = jnp.zeros_like(loss_ref)

        gf = gf_ref[...]
        ao = (a_ref[...] * (gf * _sig(gf))).astype(BF16)
        uo = uo_ref[...]
        out = x_ref[...] + (_dot(ao, wo_ref[0:FW, :]) + _dot(uo, wo_ref[FW:FW + C, :]))
        err = out - t_ref[...]
        loss_ref[...] += jnp.sum(err * err) * (0.5 / D)
        dout = err * (1.0 / D)
        dout_ref[...] = dout
        db = dout.astype(BF16)
        dy_ref[:, 0:FW] = _dot_nt(db, wo_ref[0:FW, :])
        dy_ref[:, FW:FW + C] = _dot_nt(db, wo_ref[FW:FW + C, :])
        dwo_ref[0:FW, :] += _dot_tn(ao, db)
        dwo_ref[FW:FW + C, :] += _dot_tn(uo, db)

    tok = lambda w, n=0: pl.BlockSpec((tm, w), lambda i: (i, n))
    return _pcall(
        body, name="outproj", grid=(T // tm,),
        in_specs=[tok(FW), tok(sec_w, 3), tok(C), tok(D), tok(D), pl.BlockSpec((FW + C, D), lambda i: (0, 0))],
        out_specs=[tok(D), tok(FW + C), pl.BlockSpec((FW + C, D), lambda i: (0, 0)), pl.BlockSpec((8, LANES), lambda i: (0, 0))],
        out_shape=[jax.ShapeDtypeStruct((T, D), F32), jax.ShapeDtypeStruct((T, FW + C), F32),
                   jax.ShapeDtypeStruct((FW + C, D), F32), jax.ShapeDtypeStruct((8, LANES), F32)],
    )(a, z, uo, x, tgt, wo)


def _conv_post_bwd(dy, uc, z, lg, lb, *, tm, sec_w):
    T, C = uc.shape
    FW = dy.shape[1] - C
    off_c = 6 * sec_w // C

    def body(dy_ref, uc_ref, gc_ref, lg_ref, lb_ref, dgc_ref, duc_ref, small_ref):
        @pl.when(pl.program_id(0) == 0)
        def _():
            small_ref[...] = jnp.zeros_like(small_ref)

        lg = lg_ref[...]
        rs, un, ul = _ln_parts(uc_ref[...], lg, lb_ref[...])
        s_ul = _sig(ul)
        gc = gc_ref[...]
        s_gc = _sig(gc)
        duo = dy_ref[...]
        dgc_ref[...] = (duo * (ul * s_ul) * _dsilu(gc, s_gc)).astype(BF16)
        dul = duo * (gc * s_gc) * _dsilu(ul, s_ul)
        dun = dul * lg
        duc = rs * (dun - jnp.mean(dun, axis=-1, keepdims=True) - un * jnp.mean(dun * un, axis=-1, keepdims=True))
        duc_ref[...] = duc
        small_ref[0:1, :] += jnp.sum(dul * un, axis=0, keepdims=True)
        small_ref[8:9, :] += jnp.sum(dul, axis=0, keepdims=True)
        small_ref[16:17, :] += jnp.sum(duc, axis=0, keepdims=True)

    tok = lambda n: pl.BlockSpec((tm, C), lambda i: (i, n))
    row = pl.BlockSpec((1, C), lambda i: (0, 0))
    return _pcall(
        body, name="conv_post_bwd", grid=(T // tm,),
        in_specs=[tok(FW // C), tok(0), tok(off_c), row, row],
        out_specs=[tok(0), tok(0), pl.BlockSpec((24, C), lambda i: (0, 0))],
        out_shape=[jax.ShapeDtypeStruct((T, C), BF16), jax.ShapeDtypeStruct((T, C), F32), jax.ShapeDtypeStruct((24, C), F32)],
    )(dy, uc, z, lg, lb)


def _conv_bwd(duc, z, cw, *, B, S, sec_w, rc):
    C = cw.shape[1]
    nct = C // LANES
    off_a, off_b = 4 * sec_w // LANES, 5 * sec_w // LANES

    def body(duc_ref, ga_ref, gb_ref, w_ref, dga_ref, dgb_ref, dw_ref, ubuf, dbuf):
        @pl.when(pl.program_id(1) == 0)
        def _():
            dw_ref[...] = jnp.zeros_like(dw_ref)

        ubuf[0:HALO, :] = jnp.zeros((HALO, LANES), F32)
        ubuf[HALO:HALO + S, :] = ga_ref[...] * _sig(gb_ref[...])
        dbuf[0:S, :] = duc_ref[...]
        dbuf[S:S + HALO, :] = jnp.zeros((HALO, LANES), F32)

        def chunk(cidx, carry):
            r0 = pl.multiple_of(cidx * rc, rc)
            acc = jnp.zeros((rc, LANES), F32)
            for j in range(CONV_K):
                acc = acc + w_ref[j:j + 1, :] * dbuf[pl.ds(r0 + (CONV_K - 1 - j), rc), :]
            rows = pl.ds(r0, rc)
            ga, sg = ga_ref[rows, :], _sig(gb_ref[rows, :])
            dga_ref[rows, :] = (acc * sg).astype(BF16)
            dgb_ref[rows, :] = (acc * ga * sg * (1.0 - sg)).astype(BF16)
            return carry

        lax.fori_loop(0, S // rc, chunk, 0)
        d = duc_ref[...]
        for j in range(CONV_K):
            dw_ref[j:j + 1, :] += jnp.sum(d * ubuf[pl.ds(HALO - CONV_K + 1 + j, S), :], axis=0, keepdims=True)

    return _pcall(
        body, name="conv_bwd", grid=(nct, B),
        in_specs=[pl.BlockSpec((S, LANES), lambda c, b: (b, c)), pl.BlockSpec((S, LANES), lambda c, b: (b, off_a + c)),
                  pl.BlockSpec((S, LANES), lambda c, b: (b, off_b + c)), pl.BlockSpec((CONV_K, LANES), lambda c, b: (0, c))],
        out_specs=[pl.BlockSpec((S, LANES), lambda c, b: (b, c)), pl.BlockSpec((S, LANES), lambda c, b: (b, c)),
                   pl.BlockSpec((CONV_K, LANES), lambda c, b: (0, c))],
        out_shape=[jax.ShapeDtypeStruct((B * S, C), BF16), jax.ShapeDtypeStruct((B * S, C), BF16),
                   jax.ShapeDtypeStruct((CONV_K, C), F32)],
        scratch_shapes=[pltpu.VMEM((HALO + S, LANES), F32), pltpu.VMEM((S + HALO, LANES), F32)],
    )(duc, z, z, cw)


def _attn_bwd_prep(dy, a, z, *, B, S, H, ts, sec_w):
    FW = H * HEAD_DIM
    nts = S // ts

    def body(dy_ref, a_ref, gf_ref, dgf_ref, do_ref):
        lane = _lane_iota(ts)
        for p in range(H // 2):
            cols = slice(p * LANES, (p + 1) * LANES)
            gf, av, dya = gf_ref[:, cols], a_ref[:, cols], dy_ref[:, cols]
            sg = _sig(gf)
            dgf_ref[:, cols] = (dya * av * _dsilu(gf, sg)).astype(BF16)
            da = dya * (gf * sg)
            nd = -_half_sums(da * av)
            dab = da.astype(BF16)
            for e in range(2):
                nd_e = nd[:, e * HEAD_DIM:e * HEAD_DIM + 1]
                do_ref[0, 2 * p + e] = _put3(_head_to_low(dab, e), lane, C0, _split3(nd_e)).astype(BF16)

    tok = lambda n: pl.BlockSpec((ts, FW), lambda b, i: (b * nts + i, n))
    return _pcall(
        body, name="attn_bwd_prep", grid=(B, nts),
        in_specs=[tok(0), tok(0), pl.BlockSpec((ts, sec_w), lambda b, i: (b * nts + i, 3))],
        out_specs=[tok(0), pl.BlockSpec((1, H, ts, LANES), lambda b, i: (b, 0, i, 0))],
        out_shape=[jax.ShapeDtypeStruct((B * S, FW), BF16), jax.ShapeDtypeStruct((B, H, S, LANES), BF16)],
    )(dy, a, z)


def _attn_bwd(qb, ka, va, doa, *, t, blk, dblk, G, comm=None):
    B, H, S, _ = qb.shape
    n = S // t
    assert t % blk == 0
    grid = (B, H // G, n)

    def body(*refs):
        (q_ref, k_ref, v_ref, do_ref), (dq_ref, dk_ref, dv_ref), _, rider = _split_refs(refs, 4, 3, comm)
        first, last = _grid_ends(grid)
        if comm:
            comm.begin(rider, first)
        j = pl.program_id(2)

        @pl.when(j == 0)
        def _():
            dq_ref[...] = jnp.zeros_like(dq_ref)

        ks = [k_ref[0, e] for e in range(G)]
        vs = [v_ref[0, e] for e in range(G)]

        def step(start, width, carry, shift=None):
            rows = pl.ds(pl.multiple_of(start, dblk), width)
            hi = t if shift is None else shift + width
            new = []
            for e in range(G):
                dk, dv = carry[e]
                k, v = ks[e][:hi], vs[e][:hi]
                q, do = q_ref[0, e, rows, :], do_ref[0, e, rows, :]
                st = _dot_nt(k, q)
                dpt = _dot_nt(v, do)
                if shift is not None:
                    row = lax.broadcasted_iota(jnp.int32, (hi, width), 0)
                    col = lax.broadcasted_iota(jnp.int32, (hi, width), 1)
                    st = jnp.where(row <= col + shift, st, NEG_INF)
                pt = jnp.exp(st)
                dsb = (pt * dpt).astype(BF16)
                dq_ref[0, e, rows, :] += _dot_tn(dsb, k)
                dk_new, dv_new = dk[:hi] + _dot(dsb, q), dv[:hi] + _dot(pt.astype(BF16), do)
                if hi < t:
                    dk_new, dv_new = jnp.concatenate([dk_new, dk[hi:]], axis=0), jnp.concatenate([dv_new, dv[hi:]], axis=0)
                new.append((dk_new, dv_new))
            return tuple(new)

        zero = jnp.zeros((t, LANES), F32)
        carry = tuple((zero, zero) for _ in range(G))
        for d in range(t // dblk):
            carry = step(j * t + d * dblk, dblk, carry, shift=d * dblk)
        carry = lax.fori_loop((j + 1) * (t // blk), S // blk, lambda i, c: step(i * blk, blk, c), carry)
        for e in range(G):
            dk_ref[0, e], dv_ref[0, e] = carry[e]
        if comm:
            comm.end(rider, last)

    full = pl.BlockSpec((1, G, S, LANES), lambda b, h, j: (b, h, 0, 0))
    tile = pl.BlockSpec((1, G, t, LANES), lambda b, h, j: (b, h, j, 0))
    shape = jax.ShapeDtypeStruct((B, H, S, LANES), F32)
    return _pcall(
        body, name="attn_bwd", grid=grid,
        **_with_comm(comm, [full, tile, tile, full], [full, tile, tile], [shape, shape, shape], []),
    )(qb, ka, va, doa, *(comm.operands() if comm else ()))


def _attn_post_bwd(dq, dk, dv, z, zf, bfp, gq, gk, *, B, S, H, ts):
    FW = H * HEAD_DIM
    nts = S // ts
    scale = HEAD_DIM ** -0.5

    def body(dq_ref, dk_ref, dv_ref, zq_ref, zk_ref, zf_ref, bf_ref, gq_ref, gk_ref,
             dzq_ref, dzk_ref, dzv_ref, dzf_ref, small_ref, carry):
        first = (pl.program_id(0) == 0) & (pl.program_id(1) == 0)

        @pl.when(first)
        def _():
            small_ref[...] = jnp.zeros_like(small_ref)

        @pl.when(pl.program_id(1) == 0)
        def _():
            carry[...] = jnp.zeros_like(carry)

        lane = _lane_iota(ts)
        dcs = jnp.zeros((ts, LANES), F32)
        for p in range(H // 2):
            cols = slice(p * LANES, (p + 1) * LANES)
            for d_ref, z_ref, g_ref, o_ref, srow, mul in ((dq_ref, zq_ref, gq_ref, dzq_ref, 0, scale),
                                                          (dk_ref, zk_ref, gk_ref, dzk_ref, 1, 1.0)):
                dy = jnp.where(lane < HEAD_DIM, d_ref[0, 2 * p], pltpu.roll(d_ref[0, 2 * p + 1], HEAD_DIM, 1)) * mul
                xv, g = z_ref[:, cols], g_ref[:, cols]
                r = lax.rsqrt(_half_sums(xv * xv) * (1.0 / HEAD_DIM) + EPS)
                dxh = dy * g
                mm = _half_sums(dxh * xv) * (1.0 / HEAD_DIM)
                o_ref[:, cols] = (r * (dxh - xv * (r * r) * mm)).astype(BF16)
                small_ref[8 * srow:8 * srow + 1, cols] += jnp.sum(dy * xv * r, axis=0, keepdims=True)
            dzv_ref[:, cols] = jnp.where(lane < HEAD_DIM, dv_ref[0, 2 * p], pltpu.roll(dv_ref[0, 2 * p + 1], HEAD_DIM, 1)).astype(BF16)
            for e in range(2):
                h = 2 * p + e
                dc = jnp.sum(jnp.where(lane == C0, dq_ref[0, h], 0.0) - jnp.where(lane == K0, dk_ref[0, h], 0.0),
                             axis=-1, keepdims=True)
                dcs = jnp.where(lane == h, dc, dcs)
        ri = lax.broadcasted_iota(jnp.int32, (ts, ts), 0)
        ci = lax.broadcasted_iota(jnp.int32, (ts, ts), 1)
        tri = (ci >= ri).astype(BF16)
        dl = carry[...]
        for piece in _split3(dcs):
            dl = dl + _dot(tri, piece)
        carry[...] = dl[0:1, :]
        xf = zf_ref[...] + bf_ref[...]
        df = jnp.where(lane < H, dl * _sig(-xf), 0.0)
        dzf_ref[...] = df.astype(BF16)
        small_ref[16:17, 0:LANES] += jnp.sum(df, axis=0, keepdims=True)

    rev = lambda b, i: (b, 0, nts - 1 - i, 0)
    head = pl.BlockSpec((1, H, ts, LANES), rev)
    tokrow = lambda b, i: b * nts + nts - 1 - i
    sec = lambda n: pl.BlockSpec((ts, FW), lambda b, i: (tokrow(b, i), n))
    narrow = pl.BlockSpec((ts, LANES), lambda b, i: (tokrow(b, i), 0))
    row = lambda w: pl.BlockSpec((1, w), lambda b, i: (0, 0))
    return _pcall(
        body, name="attn_post_bwd", grid=(B, nts),
        in_specs=[head, head, head, sec(0), sec(1), narrow, row(LANES), row(FW), row(FW)],
        out_specs=[sec(0), sec(0), sec(0), narrow, pl.BlockSpec((24, FW), lambda b, i: (0, 0))],
        out_shape=[jax.ShapeDtypeStruct((B * S, FW), BF16)] * 3 + [jax.ShapeDtypeStruct((B * S, LANES), BF16),
                                                                   jax.ShapeDtypeStruct((24, FW), F32)],
        scratch_shapes=[pltpu.VMEM((1, LANES), F32)],
    )(dq, dk, dv, z, z, zf, bfp, gq, gk)


def _inproj_bwd_x(dzs, dzf, wmt, wft, x, g, dout, *, tm, comm=None):
    T, D = x.shape
    nj = len(dzs)
    tn = dzs[0].shape[1]
    grid = (T // tm, nj)

    def body(*refs):
        ins, (gx_ref, dg_ref), _, rider = _split_refs(refs, nj + 6, 2, comm)
        acc = gx_ref
        dz_refs = ins[:nj]
        dzf_ref, wm_ref, wf_ref, x_ref, g_ref, dout_ref = ins[nj:]
        first, last = _grid_ends(grid)
        if comm:
            comm.begin(rider, first)
        i, j = pl.program_id(0), pl.program_id(1)

        @pl.when((i == 0) & (j == 0))
        def _():
            dg_ref[...] = jnp.zeros_like(dg_ref)

        @pl.when(j == 0)
        def _():
            acc[...] = _dot(dzf_ref[...], wf_ref[...])

        for s in range(nj):
            @pl.when(j == s)
            def _():
                acc[...] += _dot(dz_refs[s][...], wm_ref[...])

        @pl.when(j == nj - 1)
        def _():
            dh, xv, g = acc[...], x_ref[...], g_ref[...]
            r = lax.rsqrt(jnp.mean(xv * xv, axis=-1, keepdims=True) + EPS)
            dxh = dh * g
            mm = jnp.mean(dxh * xv, axis=-1, keepdims=True)
            gx_ref[...] = dout_ref[...] + r * (dxh - xv * (r * r) * mm)
            dg_ref[0:1, :] += jnp.sum(dh * xv * r, axis=0, keepdims=True)

        if comm:
            comm.end(rider, last)

    tok = pl.BlockSpec((tm, D), lambda i, j: (i, 0))
    sec = pl.BlockSpec((tm, tn), lambda i, j: (i, 0))
    return _pcall(
        body, name="inproj_bwd_x", grid=grid,
        **_with_comm(
            comm,
            [sec] * nj + [pl.BlockSpec((tm, LANES), lambda i, j: (i, 0)), pl.BlockSpec((tn, D), lambda i, j: (j, 0)),
                          pl.BlockSpec((LANES, D), lambda i, j: (0, 0)), tok, pl.BlockSpec((1, D), lambda i, j: (0, 0)), tok],
            [tok, pl.BlockSpec((8, D), lambda i, j: (0, 0))],
            [jax.ShapeDtypeStruct((T, D), F32), jax.ShapeDtypeStruct((8, D), F32)],
            []),
    )(*dzs, dzf, wmt, wft, x, g, dout, *(comm.operands() if comm else ()))


def _inproj_bwd_w(h, dz, *, tm, name):
    T, D = h.shape
    W = dz.shape[1]
    ni = T // tm

    def body(h_ref, dz_ref, dw_ref, acc):
        i = pl.program_id(0)

        @pl.when(i == 0)
        def _():
            acc[...] = jnp.zeros_like(acc)

        acc[...] += _dot_tn(dz_ref[...], h_ref[...])

        @pl.when(i == ni - 1)
        def _():
            dw_ref[...] = acc[...].astype(BF16)

    return _pcall(
        body, name=name, grid=(ni,),
        in_specs=[pl.BlockSpec((tm, D), lambda i: (i, 0)), pl.BlockSpec((tm, W), lambda i: (i, 0))],
        out_specs=pl.BlockSpec((W, D), lambda i: (0, 0)), out_shape=jax.ShapeDtypeStruct((W, D), BF16),
        scratch_shapes=[pltpu.VMEM((W, D), F32)],
    )(h, dz)


def _adamw(w, g, m, v):
    m = ADAM_B1 * m + (1.0 - ADAM_B1) * g
    v = ADAM_B2 * v + (1.0 - ADAM_B2) * (g * g)
    m_hat = m / (1.0 - ADAM_B1 ** ADAM_STEP)
    v_hat = v / (1.0 - ADAM_B2 ** ADAM_STEP)
    return -ADAM_LR * (m_hat / (jnp.sqrt(v_hat) + ADAM_EPS) + ADAM_WD * w), m, v


def _adam_update(w, m, v, parts, *, name, tr):
    R, Cc = w.shape[0], w.shape[-1]

    def body(w_ref, m_ref, v_ref, p_ref, g_ref, d_ref, nm_ref, nv_ref):
        g = p_ref[0].astype(F32)
        for k in range(1, N_DEV):
            g = g + p_ref[k].astype(F32)
        g_ref[...] = g
        d_ref[...], nm_ref[...], nv_ref[...] = _adamw(w_ref[...], g, m_ref[...], v_ref[...])

    blk = pl.BlockSpec((tr, Cc), lambda i: (i, 0)) if w.ndim == 2 else pl.BlockSpec((tr, None, Cc), lambda i: (i, 0, 0))
    shape = jax.ShapeDtypeStruct(w.shape, F32)
    return _pcall(
        body, name=name, grid=(pl.cdiv(R, tr),),
        in_specs=[blk, blk, blk, pl.BlockSpec((N_DEV, tr, Cc), lambda i: (0, i, 0))],
        out_specs=[blk, blk, blk, blk], out_shape=[shape, shape, shape, shape],
    )(w, m, v, parts)


MESH = pl.DeviceIdType.MESH
ANY = pl.BlockSpec(memory_space=pl.ANY)


def _flip(v, bit):
    return 1 - v if bit else v


def _allgather(shards):
    n = len(shards)

    def body(*refs):
        srcs, outs = refs[:n], refs[n:2 * n]
        send_sems, recv_sems, local_sems = refs[2 * n:]
        x, y, c = lax.axis_index("x"), lax.axis_index("y"), lax.axis_index("c")
        sibling = (x, y, 1 - c)
        chips = [(1 - x, y), (x, 1 - y), (1 - x, 1 - y)]
        slot = lambda px, py, pc: 4 * px + 2 * py + pc

        def copy(a, k, block, to, src=None):
            dst = outs[a].at[slot(*block)]
            return pltpu.make_async_remote_copy(src_ref=dst if src is None else src, dst_ref=dst, send_sem=send_sems.at[7 * a + k],
                                                recv_sem=recv_sems.at[7 * a + k], device_id=to, device_id_type=MESH)

        mine = [pltpu.make_async_copy(srcs[a], outs[a].at[slot(x, y, c)], local_sems.at[a]) for a in range(n)]
        sends = []
        for a in range(n):
            mine[a].start()
            sends.append(copy(a, 0, (x, y, c), sibling, src=srcs[a]))
            sends += [copy(a, 1 + j, (x, y, c), (*chip, c), src=srcs[a]) for j, chip in enumerate(chips)]
        for cp in sends:
            cp.start()
        for a in range(n):
            for j, chip in enumerate(chips):
                copy(a, 1 + j, (*chip, c), (x, y, c)).wait_recv()
                fwd = copy(a, 4 + j, (*chip, c), sibling)
                fwd.start()
                sends.append(fwd)
        for a in range(n):
            copy(a, 0, (x, y, 1 - c), (x, y, c)).wait_recv()
            for j, chip in enumerate(chips):
                copy(a, 4 + j, (*chip, 1 - c), (x, y, c)).wait_recv()
        for cp in sends:
            cp.wait_send()
        for cp in mine:
            cp.wait()

    return pl.pallas_call(
        body, name="allgather_weights", in_specs=[ANY] * n, out_specs=[ANY] * n,
        out_shape=[jax.ShapeDtypeStruct((N_DEV,) + s.shape, s.dtype) for s in shards],
        scratch_shapes=[pltpu.SemaphoreType.DMA((7 * n,)), pltpu.SemaphoreType.DMA((7 * n,)), pltpu.SemaphoreType.DMA((n,))],
    )(*shards)


class _Comm:
    def __init__(self, arrays, owners, into=None):
        self.arrays, self.owners, self.n = list(arrays), list(owners), len(arrays)
        self.into = list(into) if into else [None] * self.n
        self.carried = [t for t in self.into if t is not None]

    def operands(self):
        return self.arrays + self.carried

    def aliases(self, n_in, n_out):
        out, pos = {}, 0
        for a, t in enumerate(self.into):
            if t is not None:
                out[n_in + self.n + pos] = n_out + a
                pos += 1
        return out

    def out_shape(self):
        return [jax.ShapeDtypeStruct((N_DEV,) + (a.shape if o is None else a.shape[1:]), a.dtype)
                for a, o in zip(self.arrays, self.owners)]

    def scratch(self):
        return [pltpu.SemaphoreType.DMA((7 * self.n,)), pltpu.SemaphoreType.DMA((7 * self.n,)), pltpu.SemaphoreType.DMA((self.n,))]

    def ops(self, srcs, outs, send_sems, recv_sems, local_sems):
        x, y, c = lax.axis_index("x"), lax.axis_index("y"), lax.axis_index("c")
        me = 4 * x + 2 * y + c
        when = lambda p, f: (lambda: pl.when(p)(f))
        starts, waits = [], []
        for a, owners in enumerate(self.owners):
            src, out = srcs[a], outs[a]

            def remote(piece, slot, rel, to, a=a, out=out):
                sem = 7 * a + rel - 1
                return pltpu.make_async_remote_copy(src_ref=piece, dst_ref=out.at[slot], send_sem=send_sems.at[sem],
                                                    recv_sem=recv_sems.at[sem], device_id=to, device_id_type=MESH)

            peers = [(d, (_flip(x, d & 4), _flip(y, d & 2), _flip(c, d & 1))) for d in range(1, N_DEV)]
            if owners is None:
                local = pltpu.make_async_copy(src, out.at[me], local_sems.at[a])
                starts.append(local.start)
                waits.append(local.wait)
                for d, peer in peers:
                    cp = remote(src, me, d, peer)
                    starts.append(cp.start)
                    waits.append(cp.wait_send)
                receives = True
            else:
                for s, dest in enumerate(owners):
                    to = (dest >> 2 & 1, dest >> 1 & 1, dest & 1)
                    rel = 4 * _flip(x, to[0]) + 2 * _flip(y, to[1]) + _flip(c, to[2])
                    cp = remote(src.at[s], me, jnp.maximum(rel, 1), to)
                    local = pltpu.make_async_copy(src.at[s], out.at[me], local_sems.at[a])
                    starts += [when(rel != 0, cp.start), when(rel == 0, local.start)]
                    waits += [when(rel != 0, cp.wait_send), when(rel == 0, local.wait)]
                receives = me == owners[0]
                for dest in owners[1:]:
                    receives = receives | (me == dest)
            piece = src if owners is None else src.at[0]
            for d, peer in peers:
                def arrival(d=d, peer=peer, piece=piece, remote=remote):
                    remote(piece, 4 * peer[0] + 2 * peer[1] + peer[2], d, peer).wait_recv()

                waits.append(arrival if receives is True else when(receives, arrival))
        return starts, waits

    def _phase(self, which, rider, cond):
        srcs, outs, sems = rider

        def go():
            for op in self.ops(srcs, outs, *sems)[which]:
                op()

        go() if cond is True else pl.when(cond)(go)

    def begin(self, rider, first=True):
        self._phase(0, rider, first)

    def end(self, rider, last=True):
        self._phase(1, rider, last)


def _split_refs(refs, n_in, n_out, comm):
    k = comm.n if comm else 0
    o0 = n_in + k + (len(comm.carried) if comm else 0)
    ins, c_src = refs[:n_in], refs[n_in:n_in + k]
    outs, c_out = refs[o0:o0 + n_out], refs[o0 + n_out:o0 + n_out + k]
    rest = refs[o0 + n_out + k:]
    scratch, c_sem = (rest[:-3], rest[-3:]) if comm else (rest, ())
    return ins, outs, scratch, (c_src, c_out, c_sem)


def _grid_ends(grid):
    first = functools.reduce(jnp.logical_and, [pl.program_id(a) == 0 for a in range(len(grid))])
    last = functools.reduce(jnp.logical_and, [pl.program_id(a) == g - 1 for a, g in enumerate(grid)])
    return first, last


def _with_comm(comm, in_specs, out_specs, out_shape, scratch):
    if comm is None:
        return dict(in_specs=in_specs, out_specs=out_specs, out_shape=out_shape, scratch_shapes=scratch)
    return dict(in_specs=in_specs + [ANY] * len(comm.operands()), out_specs=out_specs + [ANY] * comm.n,
                out_shape=out_shape + comm.out_shape(), scratch_shapes=scratch + comm.scratch(),
                input_output_aliases=comm.aliases(len(in_specs), len(out_specs)))


def _exchange(comm, *, name):
    n = comm.n

    def body(*refs):
        rider = (refs[:n], refs[n:2 * n], refs[2 * n:])
        comm.begin(rider)
        comm.end(rider)

    return pl.pallas_call(body, name=name, in_specs=[ANY] * n, out_specs=[ANY] * n, out_shape=comm.out_shape(),
                          scratch_shapes=comm.scratch())(*comm.arrays)


def kernel(x, norm_g, w_in, b_forget, q_norm_g, k_norm_g, conv_w, conv_b, conv_ln_g, conv_ln_b, w_out, loss_target, m_norm_g, m_w_in, m_b_forget, m_q_norm_g, m_k_norm_g, m_conv_w, m_conv_b, m_conv_ln_g, m_conv_ln_b, m_w_out, v_norm_g, v_w_in, v_b_forget, v_q_norm_g, v_k_norm_g, v_conv_w, v_conv_b, v_conv_ln_g, v_conv_ln_b, v_w_out):
    B, S, D = x.shape
    H = q_norm_g.shape[1]
    FW = H * HEAD_DIM
    assert q_norm_g.shape[2] == HEAD_DIM and FW == D and conv_b.shape[1] == D and H <= LANES // 3
    ncol = w_in.shape[2]
    T = B * S

    to_t = lambda t: jnp.transpose(t, (2, 0, 1))
    from_t = lambda t: jnp.transpose(t, (1, 2, 0))
    wt, mt, vt = to_t(w_in), to_t(m_w_in), to_t(v_w_in)

    x2, tgt = x.reshape(T, D), loss_target.reshape(T, D)
    tm, ts, rc = min(512, T), min(256, S), min(64, S)
    tw = min(1024, T)
    everyone = list(range(N_DEV))
    bfp = jnp.pad(b_forget, ((0, 0), (0, LANES - H)))
    gq, gk = q_norm_g.reshape(1, FW), k_norm_g.reshape(1, FW)

    (wg,) = _allgather([wt.reshape(ncol, D).astype(BF16)])
    wmt, wft = _w_relayout(wg, D=D, H=H)
    z, zf, h, wog, cwg = _inproj_fwd(x2, norm_g, wmt, wft, tm=min(1024, T), tn=FW,
                                     comm=_Comm([w_out[0].astype(BF16), conv_w[0]], [None, None]))
    wo = wog.reshape(N_DEV * wog.shape[1], D)
    cw = jnp.transpose(cwg, (1, 0, 2)).reshape(CONV_K, D)
    qa, ka, va = _attn_prep(z, zf, bfp, gq, gk, B=B, S=S, H=H, ts=ts)
    a, qb = _attn_fwd(qa, ka, va, t=min(512, S), blk=min(512, S), dblk=min(512, S), G=4)
    uc = _conv_fwd(z, cw, conv_b, B=B, S=S, sec_w=FW, rc=rc)
    uo = _conv_post(uc, z, conv_ln_g, conv_ln_b, tm=ts, sec_w=FW)
    dout, dy, dwo, loss = _outproj(a, z, uo, x2, tgt, wo, tm=ts, sec_w=FW)

    dgc, duc, small_c = _conv_post_bwd(dy, uc, z, conv_ln_g, conv_ln_b, tm=ts, sec_w=FW)
    dga, dgb, dcw = _conv_bwd(duc, z, cw, B=B, S=S, sec_w=FW, rc=rc)
    dgf, doa = _attn_bwd_prep(dy, a, z, B=B, S=S, H=H, ts=ts, sec_w=FW)
    _, f1, _ = _section_rows(D, H)
    late_rows = [k for k in everyone if k * ncol >= f1]
    early_rows = [k for k in everyone if k not in late_rows]
    first_late_sec = (late_rows[0] * ncol - H) // FW
    dws = {s: _inproj_bwd_w(h, dz, tm=tw, name=f"inproj_bwd_w{s}") for s, dz in ((3, dgf), (4, dga), (5, dgb), (6, dgc))}
    dw_late = _dw_relayout({s: dws[s] for s in range(first_late_sec, 7)}, None, R=ncol, H=H, shards=late_rows, name="dw_relayout_late")
    dwo_parts = dwo.reshape(N_DEV, dwo.shape[0] // N_DEV, D).astype(BF16)
    dcw_parts = jnp.transpose(dcw.reshape(CONV_K, N_DEV, D // N_DEV), (1, 0, 2))
    dq, dk, dv, r_w, r_wo, r_cw = _attn_bwd(qb, ka, va, doa, t=min(1024, S), blk=min(512, S), dblk=min(512, S), G=4,
                                            comm=_Comm([dw_late, dwo_parts, dcw_parts], [late_rows, everyone, everyone]))
    dzq, dzk, dzv, dzf, small_a = _attn_post_bwd(dq, dk, dv, z, zf, bfp, gq, gk, B=B, S=S, H=H, ts=ts)
    dws.update({s: _inproj_bwd_w(h, dz, tm=tw, name=f"inproj_bwd_w{s}") for s, dz in ((0, dzq), (1, dzk), (2, dzv))})
    dwf = _inproj_bwd_w(h, dzf, tm=tw, name="inproj_bwd_wf")
    last_early_sec = (early_rows[-1] * ncol + ncol - 1 - H) // FW
    dw_early = _dw_relayout({s: dws[s] for s in range(last_early_sec + 1)}, dwf, R=ncol, H=H, shards=early_rows, name="dw_relayout_early")
    gx, dg, r_w = _inproj_bwd_x([dzq, dzk, dzv, dgf, dga, dgb, dgc], dzf, wmt, wft, x2, norm_g, dout, tm=tm,
                                comm=_Comm([dw_early], [early_rows], into=[r_w]))
    small = jnp.concatenate([dg, small_a[16:24], small_a[0:8], small_a[8:16], small_c[16:24], small_c[0:8], small_c[8:16],
                             jnp.pad(loss, ((0, 0), (0, D - LANES)))], axis=0)
    (r_small,) = _exchange(_Comm([small], [None]), name="exchange_small")

    g_w, d_w, nm_w, nv_w = [from_t(t) for t in _adam_update(wt, mt, vt, r_w, name="adam_w_in", tr=min(128, D))]
    g_wo, d_wo, nm_wo, nv_wo = [t[None] for t in _adam_update(w_out[0], m_w_out[0], v_w_out[0], r_wo, name="adam_w_out",
                                                               tr=min(128, w_out.shape[1]))]
    g_cw, d_cw, nm_cw, nv_cw = [t[None] for t in _adam_update(conv_w[0], m_conv_w[0], v_conv_w[0], r_cw, name="adam_conv_w", tr=CONV_K)]
    tile = lambda t: jnp.pad(t.reshape(1, -1), ((0, 7), (0, D - t.size)))
    pack = lambda *ps: jnp.concatenate([tile(p) for p in ps] + [jnp.zeros((8, D), F32)], axis=0)
    small_w = pack(norm_g, b_forget, q_norm_g, k_norm_g, conv_b, conv_ln_g, conv_ln_b)
    small_m = pack(m_norm_g, m_b_forget, m_q_norm_g, m_k_norm_g, m_conv_b, m_conv_ln_g, m_conv_ln_b)
    small_v = pack(v_norm_g, v_b_forget, v_q_norm_g, v_k_norm_g, v_conv_b, v_conv_ln_g, v_conv_ln_b)
    g_s, d_s, nm_s, nv_s = _adam_update(small_w, small_m, small_v, r_small, name="adam_small", tr=small_w.shape[0])

    def leaves(small_t, w_t, cw_t, wo_t):
        row = lambda r, like: small_t[8 * r:8 * r + 1, :like.size].reshape(like.shape)
        return [row(0, norm_g), w_t, row(1, b_forget), row(2, q_norm_g), row(3, k_norm_g), cw_t, row(4, conv_b),
                row(5, conv_ln_g), row(6, conv_ln_b), wo_t]

    return (g_s[56, 0], gx.reshape(B, S, D), *leaves(g_s, g_w, g_cw, g_wo), *leaves(d_s, d_w, d_cw, d_wo),
            *leaves(nm_s, nm_w, nm_cw, nm_wo), *leaves(nv_s, nv_w, nv_cw, nv_wo))
```

```python
import functools

import jax
import jax.numpy as jnp
from jax import lax
from jax.experimental import pallas as pl
from jax.experimental.pallas import tpu as pltpu

F32, BF16 = jnp.float32, jnp.bfloat16
EPS = 1e-6
NEG_INF = -1e30
CONV_K = 31
HEAD_DIM = 64
ADAM_LR, ADAM_B1, ADAM_B2, ADAM_EPS, ADAM_WD, ADAM_STEP = 0.001, 0.9, 0.999, 1e-08, 0.01, 10

LANES = 128
HALO = 32
N_DEV = 8
VMEM_LIMIT = 56 * 1024 * 1024

C0, K0, L0 = HEAD_DIM, HEAD_DIM + 3, HEAD_DIM + 6


def _pcall(body, *, name, vmem=VMEM_LIMIT, **kw):
    return pl.pallas_call(body, name=name, compiler_params=pltpu.CompilerParams(vmem_limit_bytes=vmem), **kw)


def _dot(a, b):
    return jnp.dot(a, b, preferred_element_type=F32)


def _dot_nt(a, b):
    return lax.dot_general(a, b, (((1,), (1,)), ((), ())), preferred_element_type=F32)


def _dot_tn(a, b):
    return lax.dot_general(a, b, (((0,), (0,)), ((), ())), preferred_element_type=F32)


def _split3(x):
    a = x.astype(BF16)
    r = x - a.astype(F32)
    b = r.astype(BF16)
    c = (r - b.astype(F32)).astype(BF16)
    return a, b, c


def _sig(x):
    return jax.nn.sigmoid(x)


def _dsilu(x, s):
    return s * (1.0 + x * (1.0 - s))


def _lane_iota(rows):
    return lax.broadcasted_iota(jnp.int32, (rows, LANES), 1)


def _half_sums(v):
    r = lax.broadcasted_iota(jnp.int32, (LANES, LANES), 0) < HEAD_DIM
    c = lax.broadcasted_iota(jnp.int32, (LANES, LANES), 1) < HEAD_DIM
    ones = (r == c).astype(BF16)
    a, b, d = _split3(v)
    return _dot(a, ones) + _dot(b, ones) + _dot(d, ones)


def _head_to_low(vb, e):
    r = lax.broadcasted_iota(jnp.int32, (LANES, LANES), 0)
    c = lax.broadcasted_iota(jnp.int32, (LANES, LANES), 1)
    return _dot(vb, ((r == c + e * HEAD_DIM) & (c < HEAD_DIM)).astype(BF16))


def _head_to_low_roll(v, e, lane):
    if e:
        v = pltpu.roll(v, HEAD_DIM, 1)
    return jnp.where(lane < HEAD_DIM, v, 0.0)


def _put3(base, lane, first, pieces):
    for n, p in enumerate(pieces):
        base = jnp.where(lane == first + n, p.astype(F32), base)
    return base


def _section_rows(D, H):
    FW = H * HEAD_DIM
    return 3 * FW, 3 * FW + H, 7 * FW + H


def _row_pieces(a, b, R, step):
    out = []
    while a < b:
        k = a // R
        lo = a - k * R
        hi = min(R, lo + min(step, b - a))
        out.append((k, lo, hi))
        a += hi - lo
    return out


def _w_relayout(wg, *, D, H):
    R = wg.shape[1]
    FW = H * HEAD_DIM
    f0, f1, end = _section_rows(D, H)

    def body(wg_ref, wm_ref, wf_ref):
        def put(dst_ref, d0, a, b):
            for k, lo, hi in _row_pieces(a, b, R, 256):
                dst_ref[d0:d0 + hi - lo, :] = wg_ref[k, lo:hi, :]
                d0 += hi - lo

        put(wm_ref, 0, 0, f0)
        put(wm_ref, f0, f1, end)
        wf_ref[...] = jnp.zeros_like(wf_ref)
        put(wf_ref, 0, f0, f1)

    return _pcall(body, name="w_relayout",
                  out_shape=[jax.ShapeDtypeStruct((7 * FW, D), BF16), jax.ShapeDtypeStruct((LANES, D), BF16)])(wg)


def _dw_relayout(secs, dwf, *, R, H, shards, name):
    FW, D = next(iter(secs.values())).shape
    f0, f1, end = _section_rows(D, H)
    order = sorted(secs)

    def body(*refs):
        sec_refs = dict(zip(order, refs))
        dwf_ref, out_ref = (refs[-2] if dwf is not None else None), refs[-1]

        def src(g0, g1):
            if f0 <= g0 < f1:
                return dwf_ref[g0 - f0:g1 - f0, :]
            s0 = g0 if g0 < f0 else g0 - H
            return sec_refs[s0 // FW][s0 % FW:s0 % FW + (g1 - g0), :]

        cuts = sorted({0, end, f0, f1} | {s * FW for s in range(4)} | {f1 + s * FW for s in range(5)})
        for slot, k in enumerate(shards):
            g = k * R
            while g < (k + 1) * R:
                nxt = min(min(c for c in cuts if c > g), (k + 1) * R, g + 256)
                out_ref[slot, g - k * R:nxt - k * R, :] = src(g, nxt)
                g = nxt

    args = [secs[s] for s in order] + ([dwf] if dwf is not None else [])
    return _pcall(body, name=name, out_shape=jax.ShapeDtypeStruct((len(shards), R, D), BF16))(*args)


def _inproj_fwd(x, g, wmt, wft, *, tm, tn, comm=None):
    T, D = x.shape
    nj = wmt.shape[0] // tn
    grid = (T // tm, nj)

    def body(*refs):
        (x_ref, g_ref, wm_ref, wf_ref), (z_ref, zf_ref, h_ref), (h_scr,), rider = _split_refs(refs, 4, 3, comm)
        first, last = _grid_ends(grid)
        if comm:
            comm.begin(rider, first)

        @pl.when(pl.program_id(1) == 0)
        def _():
            xv = x_ref[...]
            r = lax.rsqrt(jnp.mean(xv * xv, axis=-1, keepdims=True) + EPS)
            h = ((xv * r) * g_ref[...]).astype(BF16)
            h_scr[...] = h
            h_ref[...] = h
            zf_ref[...] = _dot_nt(h, wf_ref[...])

        z_ref[...] = _dot_nt(h_scr[...], wm_ref[...])
        if comm:
            comm.end(rider, last)

    return _pcall(
        body, name="inproj_fwd", grid=grid,
        **_with_comm(
            comm,
            [pl.BlockSpec((tm, D), lambda i, j: (i, 0)), pl.BlockSpec((1, D), lambda i, j: (0, 0)),
             pl.BlockSpec((tn, D), lambda i, j: (j, 0)), pl.BlockSpec((LANES, D), lambda i, j: (0, 0))],
            [pl.BlockSpec((tm, tn), lambda i, j: (i, j)), pl.BlockSpec((tm, LANES), lambda i, j: (i, 0)),
             pl.BlockSpec((tm, D), lambda i, j: (i, 0))],
            [jax.ShapeDtypeStruct((T, wmt.shape[0]), F32), jax.ShapeDtypeStruct((T, LANES), F32), jax.ShapeDtypeStruct((T, D), BF16)],
            [pltpu.VMEM((tm, D), BF16)]),
    )(x, g, wmt, wft, *(comm.operands() if comm else ()))


def _attn_prep(z, zf, bfp, gq, gk, *, B, S, H, ts):
    FW = H * HEAD_DIM
    nts = S // ts
    scale = HEAD_DIM ** -0.5

    def body(zq_ref, zk_ref, zv_ref, zf_ref, bf_ref, gq_ref, gk_ref, qa_ref, ka_ref, va_ref, carry):
        @pl.when(pl.program_id(1) == 0)
        def _():
            carry[...] = jnp.zeros_like(carry)

        lane = _lane_iota(ts)
        xf = zf_ref[...] + bf_ref[...]
        logf = jnp.minimum(xf, 0.0) - jnp.log1p(jnp.exp(-jnp.abs(xf)))
        logf = jnp.where(lane < H, logf, 0.0)
        ri = lax.broadcasted_iota(jnp.int32, (ts, ts), 0)
        ci = lax.broadcasted_iota(jnp.int32, (ts, ts), 1)
        tri = (ri >= ci).astype(BF16)
        c = carry[...]
        for piece in _split3(logf):
            c = c + _dot(tri, piece)
        carry[...] = c[ts - 1:ts, :]
        c1, c2, c3 = _split3(c)
        cpk = c1.astype(F32) + pltpu.roll(c2.astype(F32), H, 1) + pltpu.roll(c3.astype(F32), 2 * H, 1)
        cq, ck = cpk.astype(BF16), (-cpk).astype(BF16)
        prow = lax.broadcasted_iota(jnp.int32, (LANES, LANES), 0)
        pcol = lax.broadcasted_iota(jnp.int32, (LANES, LANES), 1)
        ones_q = jnp.where((lane >= K0) & (lane < K0 + 3), 1.0, 0.0)
        ones_k = jnp.where(((lane >= C0) & (lane < C0 + 3)) | ((lane >= L0) & (lane < L0 + 3)), 1.0, 0.0)
        ones_v = jnp.where((lane >= C0) & (lane < C0 + 3), 1.0, 0.0)
        for h in range(H):
            p, e = divmod(h, 2)
            cols = slice(p * LANES, (p + 1) * LANES)
            piece = jnp.where(prow == h, 0, jnp.where(prow == H + h, 1, jnp.where(prow == 2 * H + h, 2, -LANES)))
            aug_q = _dot(cq, (pcol == C0 + piece).astype(BF16)) + ones_q
            aug_k = _dot(ck, (pcol == K0 + piece).astype(BF16)) + ones_k
            for z_ref, g_ref, o_ref, aug, mul in ((zq_ref, gq_ref, qa_ref, aug_q, scale), (zk_ref, gk_ref, ka_ref, aug_k, 1.0)):
                blk = z_ref[:, cols]
                r = lax.rsqrt(_half_sums(blk * blk) * (1.0 / HEAD_DIM) + EPS)
                y = ((blk * r) * g_ref[:, cols]).astype(BF16).astype(F32) * mul
                o_ref[0, h] = (_head_to_low_roll(y, e, lane) + aug).astype(BF16)
            va_ref[0, h] = (_head_to_low_roll(zv_ref[:, cols], e, lane) + ones_v).astype(BF16)

    sec = lambda n: pl.BlockSpec((ts, FW), lambda b, i: (b * nts + i, n))
    row = lambda w: pl.BlockSpec((1, w), lambda b, i: (0, 0))
    head = pl.BlockSpec((1, H, ts, LANES), lambda b, i: (b, 0, i, 0))
    hshape = jax.ShapeDtypeStruct((B, H, S, LANES), BF16)
    return _pcall(
        body, name="attn_prep", grid=(B, nts),
        in_specs=[sec(0), sec(1), sec(2), pl.BlockSpec((ts, LANES), lambda b, i: (b * nts + i, 0)), row(LANES), row(FW), row(FW)],
        out_specs=[head, head, head], out_shape=[hshape, hshape, hshape],
        scratch_shapes=[pltpu.VMEM((1, LANES), F32)],
    )(z, z, z, zf, bfp, gq, gk)


def _attn_fwd(qa, ka, va, *, t, blk, dblk, G):
    B, H, S, _ = qa.shape
    n = S // t
    assert t % blk == 0

    def body(q_ref, k_ref, v_ref, a_ref, qb_ref):
        i = pl.program_id(2)
        lane = _lane_iota(t)
        qs = [q_ref[0, e] for e in range(G)]

        def step(start, width, carry, q0=None):
            rows = pl.ds(pl.multiple_of(start, dblk), width)
            lo = q0 or 0
            new = []
            for e in range(G):
                m, acc = carry[e]
                s = _dot_nt(qs[e][lo:], k_ref[0, e, rows, :])
                if q0 is not None:
                    row = lax.broadcasted_iota(jnp.int32, (t - lo, width), 0)
                    col = lax.broadcasted_iota(jnp.int32, (t - lo, width), 1)
                    s = jnp.where(col <= row, s, NEG_INF)
                m_new = jnp.maximum(m[lo:], jnp.max(s, axis=-1, keepdims=True))
                p = jnp.exp(s - m_new)
                acc_new = jnp.exp(m[lo:] - m_new) * acc[lo:] + _dot(p.astype(BF16), v_ref[0, e, rows, :])
                if lo:
                    m_new, acc_new = jnp.concatenate([m[:lo], m_new], axis=0), jnp.concatenate([acc[:lo], acc_new], axis=0)
                new.append((m_new, acc_new))
            return tuple(new)

        carry = tuple((jnp.full((t, 1), NEG_INF, F32), jnp.zeros((t, LANES), F32)) for _ in range(G))
        carry = lax.fori_loop(0, i * (t // blk), lambda j, c: step(j * blk, blk, c), carry)
        for d in range(t // dblk):
            carry = step(i * t + d * dblk, dblk, carry, q0=d * dblk)
        outs = []
        for e in range(G):
            m, acc = carry[e]
            l = jnp.sum(jnp.where(lane == C0, acc, 0.0), axis=-1, keepdims=True)
            outs.append(acc / l)
            qb_ref[0, e] = _put3(qs[e].astype(F32), lane, L0, _split3(-(m + jnp.log(l)))).astype(BF16)
        for pp in range(G // 2):
            a_ref[:, pp * LANES:(pp + 1) * LANES] = jnp.where(lane < HEAD_DIM, outs[2 * pp], pltpu.roll(outs[2 * pp + 1], HEAD_DIM, 1))

    return _pcall(
        body, name="attn_fwd", grid=(B, H // G, n),
        in_specs=[pl.BlockSpec((1, G, t, LANES), lambda b, p, i: (b, p, i, 0)),
                  pl.BlockSpec((1, G, S, LANES), lambda b, p, i: (b, p, 0, 0)),
                  pl.BlockSpec((1, G, S, LANES), lambda b, p, i: (b, p, 0, 0))],
        out_specs=[pl.BlockSpec((t, G * HEAD_DIM), lambda b, p, i: (b * n + i, p)),
                   pl.BlockSpec((1, G, t, LANES), lambda b, p, i: (b, p, i, 0))],
        out_shape=[jax.ShapeDtypeStruct((B * S, H * HEAD_DIM), F32), jax.ShapeDtypeStruct((B, H, S, LANES), BF16)],
    )(qa, ka, va)


def _conv_fwd(z, cw, cb, *, B, S, sec_w, rc):
    C = cw.shape[1]
    nct = C // LANES
    off_a, off_b = 4 * sec_w // LANES, 5 * sec_w // LANES

    def body(ga_ref, gb_ref, w_ref, b_ref, uc_ref, buf):
        buf[0:HALO, :] = jnp.zeros((HALO, LANES), F32)
        buf[HALO:HALO + S, :] = ga_ref[...] * _sig(gb_ref[...])

        def chunk(cidx, carry):
            r0 = pl.multiple_of(cidx * rc, rc)
            acc = jnp.zeros((rc, LANES), F32)
            for j in range(CONV_K):
                acc = acc + w_ref[j:j + 1, :] * buf[pl.ds(r0 + (HALO - CONV_K + 1 + j), rc), :]
            uc_ref[pl.ds(r0, rc), :] = acc + b_ref[...]
            return carry

        lax.fori_loop(0, S // rc, chunk, 0)

    return _pcall(
        body, name="conv_fwd", grid=(B, nct),
        in_specs=[pl.BlockSpec((S, LANES), lambda b, c: (b, off_a + c)), pl.BlockSpec((S, LANES), lambda b, c: (b, off_b + c)),
                  pl.BlockSpec((CONV_K, LANES), lambda b, c: (0, c)), pl.BlockSpec((1, LANES), lambda b, c: (0, c))],
        out_specs=pl.BlockSpec((S, LANES), lambda b, c: (b, c)),
        out_shape=jax.ShapeDtypeStruct((B * S, C), F32),
        scratch_shapes=[pltpu.VMEM((HALO + S, LANES), F32)],
    )(z, z, cw, cb)


def _ln_parts(uc, lg, lb):
    mu = jnp.mean(uc, axis=-1, keepdims=True)
    d = uc - mu
    rs = lax.rsqrt(jnp.mean(d * d, axis=-1, keepdims=True) + EPS)
    un = d * rs
    return rs, un, un * lg + lb


def _conv_post(uc, z, lg, lb, *, tm, sec_w):
    T, C = uc.shape
    off_c = 6 * sec_w // C

    def body(uc_ref, gc_ref, lg_ref, lb_ref, uo_ref):
        _, _, ul = _ln_parts(uc_ref[...], lg_ref[...], lb_ref[...])
        gc = gc_ref[...]
        uo_ref[...] = ((ul * _sig(ul)) * (gc * _sig(gc))).astype(BF16)

    return _pcall(
        body, name="conv_post", grid=(T // tm,),
        in_specs=[pl.BlockSpec((tm, C), lambda i: (i, 0)), pl.BlockSpec((tm, C), lambda i: (i, off_c)),
                  pl.BlockSpec((1, C), lambda i: (0, 0)), pl.BlockSpec((1, C), lambda i: (0, 0))],
        out_specs=pl.BlockSpec((tm, C), lambda i: (i, 0)), out_shape=jax.ShapeDtypeStruct((T, C), BF16),
    )(uc, z, lg, lb)


def _outproj(a, z, uo, x, tgt, wo, *, tm, sec_w):
    T, D = x.shape
    FW = a.shape[1]
    C = uo.shape[1]

    def body(a_ref, gf_ref, uo_ref, x_ref, t_ref, wo_ref, dout_ref, dy_ref, dwo_ref, loss_ref):
        @pl.when(pl.program_id(0) == 0)
        def _():
            dwo_ref[...] = jnp.zeros_like(dwo_ref)
            loss_ref[...] = jnp.zeros_like(loss_ref)

        gf = gf_ref[...]
        ao = (a_ref[...] * (gf * _sig(gf))).astype(BF16)
        uo = uo_ref[...]
        out = x_ref[...] + (_dot(ao, wo_ref[0:FW, :]) + _dot(uo, wo_ref[FW:FW + C, :]))
        err = out - t_ref[...]
        loss_ref[...] += jnp.sum(err * err) * (0.5 / D)
        dout = err * (1.0 / D)
        dout_ref[...] = dout
        db = dout.astype(BF16)
        dy_ref[:, 0:FW] = _dot_nt(db, wo_ref[0:FW, :])
        dy_ref[:, FW:FW + C] = _dot_nt(db, wo_ref[FW:FW + C, :])
        dwo_ref[0:FW, :] += _dot_tn(ao, db)
        dwo_ref[FW:FW + C, :] += _dot_tn(uo, db)

    tok = lambda w, n=0: pl.BlockSpec((tm, w), lambda i: (i, n))
    return _pcall(
        body, name="outproj", grid=(T // tm,),
        in_specs=[tok(FW), tok(sec_w, 3), tok(C), tok(D), tok(D), pl.BlockSpec((FW + C, D), lambda i: (0, 0))],
        out_specs=[tok(D), tok(FW + C), pl.BlockSpec((FW + C, D), lambda i: (0, 0)), pl.BlockSpec((8, LANES), lambda i: (0, 0))],
        out_shape=[jax.ShapeDtypeStruct((T, D), F32), jax.ShapeDtypeStruct((T, FW + C), F32),
                   jax.ShapeDtypeStruct((FW + C, D), F32), jax.ShapeDtypeStruct((8, LANES), F32)],
    )(a, z, uo, x, tgt, wo)


def _conv_post_bwd(dy, uc, z, lg, lb, *, tm, sec_w):
    T, C = uc.shape
    FW = dy.shape[1] - C
    off_c = 6 * sec_w // C

    def body(dy_ref, uc_ref, gc_ref, lg_ref, lb_ref, dgc_ref, duc_ref, small_ref):
        @pl.when(pl.program_id(0) == 0)
        def _():
            small_ref[...] = jnp.zeros_like(small_ref)

        lg = lg_ref[...]
        rs, un, ul = _ln_parts(uc_ref[...], lg, lb_ref[...])
        s_ul = _sig(ul)
        gc = gc_ref[...]
        s_gc = _sig(gc)
        duo = dy_ref[...]
        dgc_ref[...] = (duo * (ul * s_ul) * _dsilu(gc, s_gc)).astype(BF16)
        dul = duo * (gc * s_gc) * _dsilu(ul, s_ul)
        dun = dul * lg
        duc = rs * (dun - jnp.mean(dun, axis=-1, keepdims=True) - un * jnp.mean(dun * un, axis=-1, keepdims=True))
        duc_ref[...] = duc
        small_ref[0:1, :] += jnp.sum(dul * un, axis=0, keepdims=True)
        small_ref[8:9, :] += jnp.sum(dul, axis=0, keepdims=True)
        small_ref[16:17, :] += jnp.sum(duc, axis=0, keepdims=True)

    tok = lambda n: pl.BlockSpec((tm, C), lambda i: (i, n))
    row = pl.BlockSpec((1, C), lambda i: (0, 0))
    return _pcall(
        body, name="conv_post_bwd", grid=(T // tm,),
        in_specs=[tok(FW // C), tok(0), tok(off_c), row, row],
        out_specs=[tok(0), tok(0), pl.BlockSpec((24, C), lambda i: (0, 0))],
        out_shape=[jax.ShapeDtypeStruct((T, C), BF16), jax.ShapeDtypeStruct((T, C), F32), jax.ShapeDtypeStruct((24, C), F32)],
    )(dy, uc, z, lg, lb)


def _conv_bwd(duc, z, cw, *, B, S, sec_w, rc):
    C = cw.shape[1]
    nct = C // LANES
    off_a, off_b = 4 * sec_w // LANES, 5 * sec_w // LANES

    def body(duc_ref, ga_ref, gb_ref, w_ref, dga_ref, dgb_ref, dw_ref, ubuf, dbuf):
        @pl.when(pl.program_id(1) == 0)
        def _():
            dw_ref[...] = jnp.zeros_like(dw_ref)

        ubuf[0:HALO, :] = jnp.zeros((HALO, LANES), F32)
        ubuf[HALO:HALO + S, :] = ga_ref[...] * _sig(gb_ref[...])
        dbuf[0:S, :] = duc_ref[...]
        dbuf[S:S + HALO, :] = jnp.zeros((HALO, LANES), F32)

        def chunk(cidx, carry):
            r0 = pl.multiple_of(cidx * rc, rc)
            acc = jnp.zeros((rc, LANES), F32)
            for j in range(CONV_K):
                acc = acc + w_ref[j:j + 1, :] * dbuf[pl.ds(r0 + (CONV_K - 1 - j), rc), :]
            rows = pl.ds(r0, rc)
            ga, sg = ga_ref[rows, :], _sig(gb_ref[rows, :])
            dga_ref[rows, :] = (acc * sg).astype(BF16)
            dgb_ref[rows, :] = (acc * ga * sg * (1.0 - sg)).astype(BF16)
            return carry

        lax.fori_loop(0, S // rc, chunk, 0)
        d = duc_ref[...]
        for j in range(CONV_K):
            dw_ref[j:j + 1, :] += jnp.sum(d * ubuf[pl.ds(HALO - CONV_K + 1 + j, S), :], axis=0, keepdims=True)

    return _pcall(
        body, name="conv_bwd", grid=(nct, B),
        in_specs=[pl.BlockSpec((S, LANES), lambda c, b: (b, c)), pl.BlockSpec((S, LANES), lambda c, b: (b, off_a + c)),
                  pl.BlockSpec((S, LANES), lambda c, b: (b, off_b + c)), pl.BlockSpec((CONV_K, LANES), lambda c, b: (0, c))],
        out_specs=[pl.BlockSpec((S, LANES), lambda c, b: (b, c)), pl.BlockSpec((S, LANES), lambda c, b: (b, c)),
                   pl.BlockSpec((CONV_K, LANES), lambda c, b: (0, c))],
        out_shape=[jax.ShapeDtypeStruct((B * S, C), BF16), jax.ShapeDtypeStruct((B * S, C), BF16),
                   jax.ShapeDtypeStruct((CONV_K, C), F32)],
        scratch_shapes=[pltpu.VMEM((HALO + S, LANES), F32), pltpu.VMEM((S + HALO, LANES), F32)],
    )(duc, z, z, cw)


def _attn_bwd_prep(dy, a, z, *, B, S, H, ts, sec_w):
    FW = H * HEAD_DIM
    nts = S // ts

    def body(dy_ref, a_ref, gf_ref, dgf_ref, do_ref):
        lane = _lane_iota(ts)
        for p in range(H // 2):
            cols = slice(p * LANES, (p + 1) * LANES)
            gf, av, dya = gf_ref[:, cols], a_ref[:, cols], dy_ref[:, cols]
            sg = _sig(gf)
            dgf_ref[:, cols] = (dya * av * _dsilu(gf, sg)).astype(BF16)
            da = dya * (gf * sg)
            nd = -_half_sums(da * av)
            dab = da.astype(BF16)
            for e in range(2):
                nd_e = nd[:, e * HEAD_DIM:e * HEAD_DIM + 1]
                do_ref[0, 2 * p + e] = _put3(_head_to_low(dab, e), lane, C0, _split3(nd_e)).astype(BF16)

    tok = lambda n: pl.BlockSpec((ts, FW), lambda b, i: (b * nts + i, n))
    return _pcall(
        body, name="attn_bwd_prep", grid=(B, nts),
        in_specs=[tok(0), tok(0), pl.BlockSpec((ts, sec_w), lambda b, i: (b * nts + i, 3))],
        out_specs=[tok(0), pl.BlockSpec((1, H, ts, LANES), lambda b, i: (b, 0, i, 0))],
        out_shape=[jax.ShapeDtypeStruct((B * S, FW), BF16), jax.ShapeDtypeStruct((B, H, S, LANES), BF16)],
    )(dy, a, z)


def _attn_bwd(qb, ka, va, doa, *, t, blk, dblk, G, comm=None):
    B, H, S, _ = qb.shape
    n = S // t
    assert t % blk == 0
    grid = (B, H // G, n)

    def body(*refs):
        (q_ref, k_ref, v_ref, do_ref), (dq_ref, dk_ref, dv_ref), _, rider = _split_refs(refs, 4, 3, comm)
        first, last = _grid_ends(grid)
        if comm:
            comm.begin(rider, first)
        j = pl.program_id(2)

        @pl.when(j == 0)
        def _():
            dq_ref[...] = jnp.zeros_like(dq_ref)

        ks = [k_ref[0, e] for e in range(G)]
        vs = [v_ref[0, e] for e in range(G)]

        def step(start, width, carry, shift=None):
            rows = pl.ds(pl.multiple_of(start, dblk), width)
            hi = t if shift is None else shift + width
            new = []
            for e in range(G):
                dk, dv = carry[e]
                k, v = ks[e][:hi], vs[e][:hi]
                q, do = q_ref[0, e, rows, :], do_ref[0, e, rows, :]
                st = _dot_nt(k, q)
                dpt = _dot_nt(v, do)
                if shift is not None:
                    row = lax.broadcasted_iota(jnp.int32, (hi, width), 0)
                    col = lax.broadcasted_iota(jnp.int32, (hi, width), 1)
                    st = jnp.where(row <= col + shift, st, NEG_INF)
                pt = jnp.exp(st)
                dsb = (pt * dpt).astype(BF16)
                dq_ref[0, e, rows, :] += _dot_tn(dsb, k)
                dk_new, dv_new = dk[:hi] + _dot(dsb, q), dv[:hi] + _dot(pt.astype(BF16), do)
                if hi < t:
                    dk_new, dv_new = jnp.concatenate([dk_new, dk[hi:]], axis=0), jnp.concatenate([dv_new, dv[hi:]], axis=0)
                new.append((dk_new, dv_new))
            return tuple(new)

        zero = jnp.zeros((t, LANES), F32)
        carry = tuple((zero, zero) for _ in range(G))
        for d in range(t // dblk):
            carry = step(j * t + d * dblk, dblk, carry, shift=d * dblk)
        carry = lax.fori_loop((j + 1) * (t // blk), S // blk, lambda i, c: step(i * blk, blk, c), carry)
        for e in range(G):
            dk_ref[0, e], dv_ref[0, e] = carry[e]
        if comm:
            comm.end(rider, last)

    full = pl.BlockSpec((1, G, S, LANES), lambda b, h, j: (b, h, 0, 0))
    tile = pl.BlockSpec((1, G, t, LANES), lambda b, h, j: (b, h, j, 0))
    shape = jax.ShapeDtypeStruct((B, H, S, LANES), F32)
    return _pcall(
        body, name="attn_bwd", grid=grid,
        **_with_comm(comm, [full, tile, tile, full], [full, tile, tile], [shape, shape, shape], []),
    )(qb, ka, va, doa, *(comm.operands() if comm else ()))


def _attn_post_bwd(dq, dk, dv, z, zf, bfp, gq, gk, *, B, S, H, ts):
    FW = H * HEAD_DIM
    nts = S // ts
    scale = HEAD_DIM ** -0.5

    def body(dq_ref, dk_ref, dv_ref, zq_ref, zk_ref, zf_ref, bf_ref, gq_ref, gk_ref,
             dzq_ref, dzk_ref, dzv_ref, dzf_ref, small_ref, carry):
        first = (pl.program_id(0) == 0) & (pl.program_id(1) == 0)

        @pl.when(first)
        def _():
            small_ref[...] = jnp.zeros_like(small_ref)

        @pl.when(pl.program_id(1) == 0)
        def _():
            carry[...] = jnp.zeros_like(carry)

        lane = _lane_iota(ts)
        dcs = jnp.zeros((ts, LANES), F32)
        for p in range(H // 2):
            cols = slice(p * LANES, (p + 1) * LANES)
            for d_ref, z_ref, g_ref, o_ref, srow, mul in ((dq_ref, zq_ref, gq_ref, dzq_ref, 0, scale),
                                                          (dk_ref, zk_ref, gk_ref, dzk_ref, 1, 1.0)):
                dy = jnp.where(lane < HEAD_DIM, d_ref[0, 2 * p], pltpu.roll(d_ref[0, 2 * p + 1], HEAD_DIM, 1)) * mul
                xv, g = z_ref[:, cols], g_ref[:, cols]
                r = lax.rsqrt(_half_sums(xv * xv) * (1.0 / HEAD_DIM) + EPS)
                dxh = dy * g
                mm = _half_sums(dxh * xv) * (1.0 / HEAD_DIM)
                o_ref[:, cols] = (r * (dxh - xv * (r * r) * mm)).astype(BF16)
                small_ref[8 * srow:8 * srow + 1, cols] += jnp.sum(dy * xv * r, axis=0, keepdims=True)
            dzv_ref[:, cols] = jnp.where(lane < HEAD_DIM, dv_ref[0, 2 * p], pltpu.roll(dv_ref[0, 2 * p + 1], HEAD_DIM, 1)).astype(BF16)
            for e in range(2):
                h = 2 * p + e
                dc = jnp.sum(jnp.where(lane == C0, dq_ref[0, h], 0.0) - jnp.where(lane == K0, dk_ref[0, h], 0.0),
                             axis=-1, keepdims=True)
                dcs = jnp.where(lane == h, dc, dcs)
        ri = lax.broadcasted_iota(jnp.int32, (ts, ts), 0)
        ci = lax.broadcasted_iota(jnp.int32, (ts, ts), 1)
        tri = (ci >= ri).astype(BF16)
        dl = carry[...]
        for piece in _split3(dcs):
            dl = dl + _dot(tri, piece)
        carry[...] = dl[0:1, :]
        xf = zf_ref[...] + bf_ref[...]
        df = jnp.where(lane < H, dl * _sig(-xf), 0.0)
        dzf_ref[...] = df.astype(BF16)
        small_ref[16:17, 0:LANES] += jnp.sum(df, axis=0, keepdims=True)

    rev = lambda b, i: (b, 0, nts - 1 - i, 0)
    head = pl.BlockSpec((1, H, ts, LANES), rev)
    tokrow = lambda b, i: b * nts + nts - 1 - i
    sec = lambda n: pl.BlockSpec((ts, FW), lambda b, i: (tokrow(b, i), n))
    narrow = pl.BlockSpec((ts, LANES), lambda b, i: (tokrow(b, i), 0))
    row = lambda w: pl.BlockSpec((1, w), lambda b, i: (0, 0))
    return _pcall(
        body, name="attn_post_bwd", grid=(B, nts),
        in_specs=[head, head, head, sec(0), sec(1), narrow, row(LANES), row(FW), row(FW)],
        out_specs=[sec(0), sec(0), sec(0), narrow, pl.BlockSpec((24, FW), lambda b, i: (0, 0))],
        out_shape=[jax.ShapeDtypeStruct((B * S, FW), BF16)] * 3 + [jax.ShapeDtypeStruct((B * S, LANES), BF16),
                                                                   jax.ShapeDtypeStruct((24, FW), F32)],
        scratch_shapes=[pltpu.VMEM((1, LANES), F32)],
    )(dq, dk, dv, z, z, zf, bfp, gq, gk)


def _inproj_bwd_x(dzs, dzf, wmt, wft, x, g, dout, *, tm, comm=None):
    T, D = x.shape
    nj = len(dzs)
    tn = dzs[0].shape[1]
    grid = (T // tm, nj)

    def body(*refs):
        ins, (gx_ref, dg_ref), _, rider = _split_refs(refs, nj + 6, 2, comm)
        acc = gx_ref
        dz_refs = ins[:nj]
        dzf_ref, wm_ref, wf_ref, x_ref, g_ref, dout_ref = ins[nj:]
        first, last = _grid_ends(grid)
        if comm:
            comm.begin(rider, first)
        i, j = pl.program_id(0), pl.program_id(1)

        @pl.when((i == 0) & (j == 0))
        def _():
            dg_ref[...] = jnp.zeros_like(dg_ref)

        @pl.when(j == 0)
        def _():
            acc[...] = _dot(dzf_ref[...], wf_ref[...])

        for s in range(nj):
            @pl.when(j == s)
            def _():
                acc[...] += _dot(dz_refs[s][...], wm_ref[...])

        @pl.when(j == nj - 1)
        def _():
            dh, xv, g = acc[...], x_ref[...], g_ref[...]
            r = lax.rsqrt(jnp.mean(xv * xv, axis=-1, keepdims=True) + EPS)
            dxh = dh * g
            mm = jnp.mean(dxh * xv, axis=-1, keepdims=True)
            gx_ref[...] = dout_ref[...] + r * (dxh - xv * (r * r) * mm)
            dg_ref[0:1, :] += jnp.sum(dh * xv * r, axis=0, keepdims=True)

        if comm:
            comm.end(rider, last)

    tok = pl.BlockSpec((tm, D), lambda i, j: (i, 0))
    sec = pl.BlockSpec((tm, tn), lambda i, j: (i, 0))
    return _pcall(
        body, name="inproj_bwd_x", grid=grid,
        **_with_comm(
            comm,
            [sec] * nj + [pl.BlockSpec((tm, LANES), lambda i, j: (i, 0)), pl.BlockSpec((tn, D), lambda i, j: (j, 0)),
                          pl.BlockSpec((LANES, D), lambda i, j: (0, 0)), tok, pl.BlockSpec((1, D), lambda i, j: (0, 0)), tok],
            [tok, pl.BlockSpec((8, D), lambda i, j: (0, 0))],
            [jax.ShapeDtypeStruct((T, D), F32), jax.ShapeDtypeStruct((8, D), F32)],
            []),
    )(*dzs, dzf, wmt, wft, x, g, dout, *(comm.operands() if comm else ()))


def _inproj_bwd_w(h, dz, *, tm, name):
    T, D = h.shape
    W = dz.shape[1]
    ni = T // tm

    def body(h_ref, dz_ref, dw_ref, acc):
        i = pl.program_id(0)

        @pl.when(i == 0)
        def _():
            acc[...] = jnp.zeros_like(acc)

        acc[...] += _dot_tn(dz_ref[...], h_ref[...])

        @pl.when(i == ni - 1)
        def _():
            dw_ref[...] = acc[...].astype(BF16)

    return _pcall(
        body, name=name, grid=(ni,),
        in_specs=[pl.BlockSpec((tm, D), lambda i: (i, 0)), pl.BlockSpec((tm, W), lambda i: (i, 0))],
        out_specs=pl.BlockSpec((W, D), lambda i: (0, 0)), out_shape=jax.ShapeDtypeStruct((W, D), BF16),
        scratch_shapes=[pltpu.VMEM((W, D), F32)],
    )(h, dz)


def _adamw(w, g, m, v):
    m = ADAM_B1 * m + (1.0 - ADAM_B1) * g
    v = ADAM_B2 * v + (1.0 - ADAM_B2) * (g * g)
    m_hat = m / (1.0 - ADAM_B1 ** ADAM_STEP)
    v_hat = v / (1.0 - ADAM_B2 ** ADAM_STEP)
    return -ADAM_LR * (m_hat / (jnp.sqrt(v_hat) + ADAM_EPS) + ADAM_WD * w), m, v


def _adam_update(w, m, v, parts, *, name, tr):
    R, Cc = w.shape[0], w.shape[-1]

    def body(w_ref, m_ref, v_ref, p_ref, g_ref, d_ref, nm_ref, nv_ref):
        g = p_ref[0].astype(F32)
        for k in range(1, N_DEV):
            g = g + p_ref[k].astype(F32)
        g_ref[...] = g
        d_ref[...], nm_ref[...], nv_ref[...] = _adamw(w_ref[...], g, m_ref[...], v_ref[...])

    blk = pl.BlockSpec((tr, Cc), lambda i: (i, 0)) if w.ndim == 2 else pl.BlockSpec((tr, None, Cc), lambda i: (i, 0, 0))
    shape = jax.ShapeDtypeStruct(w.shape, F32)
    return _pcall(
        body, name=name, grid=(pl.cdiv(R, tr),),
        in_specs=[blk, blk, blk, pl.BlockSpec((N_DEV, tr, Cc), lambda i: (0, i, 0))],
        out_specs=[blk, blk, blk, blk], out_shape=[shape, shape, shape, shape],
    )(w, m, v, parts)


MESH = pl.DeviceIdType.MESH
ANY = pl.BlockSpec(memory_space=pl.ANY)


def _flip(v, bit):
    return 1 - v if bit else v


def _allgather(shards):
    n = len(shards)

    def body(*refs):
        srcs, outs = refs[:n], refs[n:2 * n]
        send_sems, recv_sems, local_sems = refs[2 * n:]
        x, y, c = lax.axis_index("x"), lax.axis_index("y"), lax.axis_index("c")
        sibling = (x, y, 1 - c)
        chips = [(1 - x, y), (x, 1 - y), (1 - x, 1 - y)]
        slot = lambda px, py, pc: 4 * px + 2 * py + pc

        def copy(a, k, block, to, src=None):
            dst = outs[a].at[slot(*block)]
            return pltpu.make_async_remote_copy(src_ref=dst if src is None else src, dst_ref=dst, send_sem=send_sems.at[7 * a + k],
                                                recv_sem=recv_sems.at[7 * a + k], device_id=to, device_id_type=MESH)

        mine = [pltpu.make_async_copy(srcs[a], outs[a].at[slot(x, y, c)], local_sems.at[a]) for a in range(n)]
        sends = []
        for a in range(n):
            mine[a].start()
            sends.append(copy(a, 0, (x, y, c), sibling, src=srcs[a]))
            sends += [copy(a, 1 + j, (x, y, c), (*chip, c), src=srcs[a]) for j, chip in enumerate(chips)]
        for cp in sends:
            cp.start()
        for a in range(n):
            for j, chip in enumerate(chips):
                copy(a, 1 + j, (*chip, c), (x, y, c)).wait_recv()
                fwd = copy(a, 4 + j, (*chip, c), sibling)
                fwd.start()
                sends.append(fwd)
        for a in range(n):
            copy(a, 0, (x, y, 1 - c), (x, y, c)).wait_recv()
            for j, chip in enumerate(chips):
                copy(a, 4 + j, (*chip, 1 - c), (x, y, c)).wait_recv()
        for cp in sends:
            cp.wait_send()
        for cp in mine:
            cp.wait()

    return pl.pallas_call(
        body, name="allgather_weights", in_specs=[ANY] * n, out_specs=[ANY] * n,
        out_shape=[jax.ShapeDtypeStruct((N_DEV,) + s.shape, s.dtype) for s in shards],
        scratch_shapes=[pltpu.SemaphoreType.DMA((7 * n,)), pltpu.SemaphoreType.DMA((7 * n,)), pltpu.SemaphoreType.DMA((n,))],
    )(*shards)


class _Comm:
    def __init__(self, arrays, owners, into=None):
        self.arrays, self.owners, self.n = list(arrays), list(owners), len(arrays)
        self.into = list(into) if into else [None] * self.n
        self.carried = [t for t in self.into if t is not None]

    def operands(self):
        return self.arrays + self.carried

    def aliases(self, n_in, n_out):
        out, pos = {}, 0
        for a, t in enumerate(self.into):
            if t is not None:
                out[n_in + self.n + pos] = n_out + a
                pos += 1
        return out

    def out_shape(self):
        return [jax.ShapeDtypeStruct((N_DEV,) + (a.shape if o is None else a.shape[1:]), a.dtype)
                for a, o in zip(self.arrays, self.owners)]

    def scratch(self):
        return [pltpu.SemaphoreType.DMA((7 * self.n,)), pltpu.SemaphoreType.DMA((7 * self.n,)), pltpu.SemaphoreType.DMA((self.n,))]

    def ops(self, srcs, outs, send_sems, recv_sems, local_sems):
        x, y, c = lax.axis_index("x"), lax.axis_index("y"), lax.axis_index("c")
        me = 4 * x + 2 * y + c
        when = lambda p, f: (lambda: pl.when(p)(f))
        starts, waits = [], []
        for a, owners in enumerate(self.owners):
            src, out = srcs[a], outs[a]

            def remote(piece, slot, rel, to, a=a, out=out):
                sem = 7 * a + rel - 1
                return pltpu.make_async_remote_copy(src_ref=piece, dst_ref=out.at[slot], send_sem=send_sems.at[sem],
                                                    recv_sem=recv_sems.at[sem], device_id=to, device_id_type=MESH)

            peers = [(d, (_flip(x, d & 4), _flip(y, d & 2), _flip(c, d & 1))) for d in range(1, N_DEV)]
            if owners is None:
                local = pltpu.make_async_copy(src, out.at[me], local_sems.at[a])
                starts.append(local.start)
                waits.append(local.wait)
                for d, peer in peers:
                    cp = remote(src, me, d, peer)
                    starts.append(cp.start)
                    waits.append(cp.wait_send)
                receives = True
            else:
                for s, dest in enumerate(owners):
                    to = (dest >> 2 & 1, dest >> 1 & 1, dest & 1)
                    rel = 4 * _flip(x, to[0]) + 2 * _flip(y, to[1]) + _flip(c, to[2])
                    cp = remote(src.at[s], me, jnp.maximum(rel, 1), to)
                    local = pltpu.make_async_copy(src.at[s], out.at[me], local_sems.at[a])
                    starts += [when(rel != 0, cp.start), when(rel == 0, local.start)]
                    waits += [when(rel != 0, cp.wait_send), when(rel == 0, local.wait)]
                receives = me == owners[0]
                for dest in owners[1:]:
                    receives = receives | (me == dest)
            piece = src if owners is None else src.at[0]
            for d, peer in peers:
                def arrival(d=d, peer=peer, piece=piece, remote=remote):
                    remote(piece, 4 * peer[0] + 2 * peer[1] + peer[2], d, peer).wait_recv()

                waits.append(arrival if receives is True else when(receives, arrival))
        return starts, waits

    def _phase(self, which, rider, cond):
        srcs, outs, sems = rider

        def go():
            for op in self.ops(srcs, outs, *sems)[which]:
                op()

        go() if cond is True else pl.when(cond)(go)

    def begin(self, rider, first=True):
        self._phase(0, rider, first)

    def end(self, rider, last=True):
        self._phase(1, rider, last)


def _split_refs(refs, n_in, n_out, comm):
    k = comm.n if comm else 0
    o0 = n_in + k + (len(comm.carried) if comm else 0)
    ins, c_src = refs[:n_in], refs[n_in:n_in + k]
    outs, c_out = refs[o0:o0 + n_out], refs[o0 + n_out:o0 + n_out + k]
    rest = refs[o0 + n_out + k:]
    scratch, c_sem = (rest[:-3], rest[-3:]) if comm else (rest, ())
    return ins, outs, scratch, (c_src, c_out, c_sem)


def _grid_ends(grid):
    first = functools.reduce(jnp.logical_and, [pl.program_id(a) == 0 for a in range(len(grid))])
    last = functools.reduce(jnp.logical_and, [pl.program_id(a) == g - 1 for a, g in enumerate(grid)])
    return first, last


def _with_comm(comm, in_specs, out_specs, out_shape, scratch):
    if comm is None:
        return dict(in_specs=in_specs, out_specs=out_specs, out_shape=out_shape, scratch_shapes=scratch)
    return dict(in_specs=in_specs + [ANY] * len(comm.operands()), out_specs=out_specs + [ANY] * comm.n,
                out_shape=out_shape + comm.out_shape(), scratch_shapes=scratch + comm.scratch(),
                input_output_aliases=comm.aliases(len(in_specs), len(out_specs)))


def _exchange(comm, *, name):
    n = comm.n

    def body(*refs):
        rider = (refs[:n], refs[n:2 * n], refs[2 * n:])
        comm.begin(rider)
        comm.end(rider)

    return pl.pallas_call(body, name=name, in_specs=[ANY] * n, out_specs=[ANY] * n, out_shape=comm.out_shape(),
                          scratch_shapes=comm.scratch())(*comm.arrays)


def kernel(x, norm_g, w_in, b_forget, q_norm_g, k_norm_g, conv_w, conv_b, conv_ln_g, conv_ln_b, w_out, loss_target, m_norm_g, m_w_in, m_b_forget, m_q_norm_g, m_k_norm_g, m_conv_w, m_conv_b, m_conv_ln_g, m_conv_ln_b, m_w_out, v_norm_g, v_w_in, v_b_forget, v_q_norm_g, v_k_norm_g, v_conv_w, v_conv_b, v_conv_ln_g, v_conv_ln_b, v_w_out):
    B, S, D = x.shape
    H = q_norm_g.shape[1]
    FW = H * HEAD_DIM
    assert q_norm_g.shape[2] == HEAD_DIM and FW == D and conv_b.shape[1] == D and H <= LANES // 3
    ncol = w_in.shape[2]
    T = B * S

    to_t = lambda t: jnp.transpose(t, (2, 0, 1))
    from_t = lambda t: jnp.transpose(t, (1, 2, 0))
    wt, mt, vt = to_t(w_in), to_t(m_w_in), to_t(v_w_in)

    x2, tgt = x.reshape(T, D), loss_target.reshape(T, D)
    tm, ts, rc = min(512, T), min(256, S), min(64, S)
    tw = min(1024, T)
    everyone = list(range(N_DEV))
    bfp = jnp.pad(b_forget, ((0, 0), (0, LANES - H)))
    gq, gk = q_norm_g.reshape(1, FW), k_norm_g.reshape(1, FW)

    (wg,) = _allgather([wt.reshape(ncol, D).astype(BF16)])
    wmt, wft = _w_relayout(wg, D=D, H=H)
    z, zf, h, wog, cwg = _inproj_fwd(x2, norm_g, wmt, wft, tm=min(1024, T), tn=FW,
                                     comm=_Comm([w_out[0].astype(BF16), conv_w[0]], [None, None]))
    wo = wog.reshape(N_DEV * wog.shape[1], D)
    cw = jnp.transpose(cwg, (1, 0, 2)).reshape(CONV_K, D)
    qa, ka, va = _attn_prep(z, zf, bfp, gq, gk, B=B, S=S, H=H, ts=ts)
    a, qb = _attn_fwd(qa, ka, va, t=min(512, S), blk=min(512, S), dblk=min(512, S), G=4)
    uc = _conv_fwd(z, cw, conv_b, B=B, S=S, sec_w=FW, rc=rc)
    uo = _conv_post(uc, z, conv_ln_g, conv_ln_b, tm=ts, sec_w=FW)
    dout, dy, dwo, loss = _outproj(a, z, uo, x2, tgt, wo, tm=ts, sec_w=FW)

    dgc, duc, small_c = _conv_post_bwd(dy, uc, z, conv_ln_g, conv_ln_b, tm=ts, sec_w=FW)
    dga, dgb, dcw = _conv_bwd(duc, z, cw, B=B, S=S, sec_w=FW, rc=rc)
    dgf, doa = _attn_bwd_prep(dy, a, z, B=B, S=S, H=H, ts=ts, sec_w=FW)
    _, f1, _ = _section_rows(D, H)
    late_rows = [k for k in everyone if k * ncol >= f1]
    early_rows = [k for k in everyone if k not in late_rows]
    first_late_sec = (late_rows[0] * ncol - H) // FW
    dws = {s: _inproj_bwd_w(h, dz, tm=tw, name=f"inproj_bwd_w{s}") for s, dz in ((3, dgf), (4, dga), (5, dgb), (6, dgc))}
    dw_late = _dw_relayout({s: dws[s] for s in range(first_late_sec, 7)}, None, R=ncol, H=H, shards=late_rows, name="dw_relayout_late")
    dwo_parts = dwo.reshape(N_DEV, dwo.shape[0] // N_DEV, D).astype(BF16)
    dcw_parts = jnp.transpose(dcw.reshape(CONV_K, N_DEV, D // N_DEV), (1, 0, 2))
    dq, dk, dv, r_w, r_wo, r_cw = _attn_bwd(qb, ka, va, doa, t=min(2048, S), blk=min(512, S), dblk=min(512, S), G=4,
                                            comm=_Comm([dw_late, dwo_parts, dcw_parts], [late_rows, everyone, everyone]))
    dzq, dzk, dzv, dzf, small_a = _attn_post_bwd(dq, dk, dv, z, zf, bfp, gq, gk, B=B, S=S, H=H, ts=ts)
    dws.update({s: _inproj_bwd_w(h, dz, tm=tw, name=f"inproj_bwd_w{s}") for s, dz in ((0, dzq), (1, dzk), (2, dzv))})
    dwf = _inproj_bwd_w(h, dzf, tm=tw, name="inproj_bwd_wf")
    last_early_sec = (early_rows[-1] * ncol + ncol - 1 - H) // FW
    dw_early = _dw_relayout({s: dws[s] for s in range(last_early_sec + 1)}, dwf, R=ncol, H=H, shards=early_rows, name="dw_relayout_early")
    gx, dg, r_w = _inproj_bwd_x([dzq, dzk, dzv, dgf, dga, dgb, dgc], dzf, wmt, wft, x2, norm_g, dout, tm=tm,
                                comm=_Comm([dw_early], [early_rows], into=[r_w]))
    small = jnp.concatenate([dg, small_a[16:24], small_a[0:8], small_a[8:16], small_c[16:24], small_c[0:8], small_c[8:16],
                             jnp.pad(loss, ((0, 0), (0, D - LANES)))], axis=0)
    (r_small,) = _exchange(_Comm([small], [None]), name="exchange_small")

    g_w, d_w, nm_w, nv_w = [from_t(t) for t in _adam_update(wt, mt, vt, r_w, name="adam_w_in", tr=min(128, D))]
    g_wo, d_wo, nm_wo, nv_wo = [t[None] for t in _adam_update(w_out[0], m_w_out[0], v_w_out[0], r_wo, name="adam_w_out",
                                                               tr=min(128, w_out.shape[1]))]
    g_cw, d_cw, nm_cw, nv_cw = [t[None] for t in _adam_update(conv_w[0], m_conv_w[0], v_conv_w[0], r_cw, name="adam_conv_w", tr=CONV_K)]
    tile = lambda t: jnp.pad(t.reshape(1, -1), ((0, 7), (0, D - t.size)))
    pack = lambda *ps: jnp.concatenate([tile(p) for p in ps] + [jnp.zeros((8, D), F32)], axis=0)
    small_w = pack(norm_g, b_forget, q_norm_g, k_norm_g, conv_b, conv_ln_g, conv_ln_b)
    small_m = pack(m_norm_g, m_b_forget, m_q_norm_g, m_k_norm_g, m_conv_b, m_conv_ln_g, m_conv_ln_b)
    small_v = pack(v_norm_g, v_b_forget, v_q_norm_g, v_k_norm_g, v_conv_b, v_conv_ln_g, v_conv_ln_b)
    g_s, d_s, nm_s, nv_s = _adam_update(small_w, small_m, small_v, r_small, name="adam_small", tr=small_w.shape[0])

    def leaves(small_t, w_t, cw_t, wo_t):
        row = lambda r, like: small_t[8 * r:8 * r + 1, :like.size].reshape(like.shape)
        return [row(0, norm_g), w_t, row(1, b_forget), row(2, q_norm_g), row(3, k_norm_g), cw_t, row(4, conv_b),
                row(5, conv_ln_g), row(6, conv_ln_b), wo_t]

    return (g_s[56, 0], gx.reshape(B, S, D), *leaves(g_s, g_w, g_cw, g_wo), *leaves(d_s, d_w, d_cw, d_wo),
            *leaves(nm_s, nm_w, nm_cw, nm_wo), *leaves(nv_s, nv_w, nv_cw, nv_wo))
```

```python
import functools

import jax
import jax.numpy as jnp
from jax import lax
from jax.experimental import pallas as pl
from jax.experimental.pallas import tpu as pltpu

F32, BF16 = jnp.float32, jnp.bfloat16
EPS = 1e-6
NEG_INF = -1e30
CONV_K = 31
HEAD_DIM = 64
ADAM_LR, ADAM_B1, ADAM_B2, ADAM_EPS, ADAM_WD, ADAM_STEP = 0.001, 0.9, 0.999, 1e-08, 0.01, 10

LANES = 128
HALO = 32
N_DEV = 8
VMEM_LIMIT = 56 * 1024 * 1024

C0, K0, L0 = HEAD_DIM, HEAD_DIM + 3, HEAD_DIM + 6


def _pcall(body, *, name, vmem=VMEM_LIMIT, **kw):
    return pl.pallas_call(body, name=name, compiler_params=pltpu.CompilerParams(vmem_limit_bytes=vmem), **kw)


def _dot(a, b):
    return jnp.dot(a, b, preferred_element_type=F32)


def _dot_nt(a, b):
    return lax.dot_general(a, b, (((1,), (1,)), ((), ())), preferred_element_type=F32)


def _dot_tn(a, b):
    return lax.dot_general(a, b, (((0,), (0,)), ((), ())), preferred_element_type=F32)


def _split3(x):
    a = x.astype(BF16)
    r = x - a.astype(F32)
    b = r.astype(BF16)
    c = (r - b.astype(F32)).astype(BF16)
    return a, b, c


def _sig(x):
    return jax.nn.sigmoid(x)


def _dsilu(x, s):
    return s * (1.0 + x * (1.0 - s))


def _lane_iota(rows):
    return lax.broadcasted_iota(jnp.int32, (rows, LANES), 1)


def _half_sums(v):
    r = lax.broadcasted_iota(jnp.int32, (LANES, LANES), 0) < HEAD_DIM
    c = lax.broadcasted_iota(jnp.int32, (LANES, LANES), 1) < HEAD_DIM
    ones = (r == c).astype(BF16)
    a, b, d = _split3(v)
    return _dot(a, ones) + _dot(b, ones) + _dot(d, ones)


def _head_to_low(vb, e):
    r = lax.broadcasted_iota(jnp.int32, (LANES, LANES), 0)
    c = lax.broadcasted_iota(jnp.int32, (LANES, LANES), 1)
    return _dot(vb, ((r == c + e * HEAD_DIM) & (c < HEAD_DIM)).astype(BF16))


def _head_to_low_roll(v, e, lane):
    if e:
        v = pltpu.roll(v, HEAD_DIM, 1)
    return jnp.where(lane < HEAD_DIM, v, 0.0)


def _put3(base, lane, first, pieces):
    for n, p in enumerate(pieces):
        base = jnp.where(lane == first + n, p.astype(F32), base)
    return base


def _section_rows(D, H):
    FW = H * HEAD_DIM
    return 3 * FW, 3 * FW + H, 7 * FW + H


def _row_pieces(a, b, R, step):
    out = []
    while a < b:
        k = a // R
        lo = a - k * R
        hi = min(R, lo + min(step, b - a))
        out.append((k, lo, hi))
        a += hi - lo
    return out


def _w_relayout(wg, *, D, H):
    R = wg.shape[1]
    FW = H * HEAD_DIM
    f0, f1, end = _section_rows(D, H)

    def body(wg_ref, wm_ref, wf_ref):
        def put(dst_ref, d0, a, b):
            for k, lo, hi in _row_pieces(a, b, R, 256):
                dst_ref[d0:d0 + hi - lo, :] = wg_ref[k, lo:hi, :]
                d0 += hi - lo

        put(wm_ref, 0, 0, f0)
        put(wm_ref, f0, f1, end)
        wf_ref[...] = jnp.zeros_like(wf_ref)
        put(wf_ref, 0, f0, f1)

    return _pcall(body, name="w_relayout",
                  out_shape=[jax.ShapeDtypeStruct((7 * FW, D), BF16), jax.ShapeDtypeStruct((LANES, D), BF16)])(wg)


def _dw_relayout(secs, dwf, *, R, H, shards, name):
    FW, D = next(iter(secs.values())).shape
    f0, f1, end = _section_rows(D, H)
    order = sorted(secs)

    def body(*refs):
        sec_refs = dict(zip(order, refs))
        dwf_ref, out_ref = (refs[-2] if dwf is not None else None), refs[-1]

        def src(g0, g1):
            if f0 <= g0 < f1:
                return dwf_ref[g0 - f0:g1 - f0, :]
            s0 = g0 if g0 < f0 else g0 - H
            return sec_refs[s0 // FW][s0 % FW:s0 % FW + (g1 - g0), :]

        cuts = sorted({0, end, f0, f1} | {s * FW for s in range(4)} | {f1 + s * FW for s in range(5)})
        for slot, k in enumerate(shards):
            g = k * R
            while g < (k + 1) * R:
                nxt = min(min(c for c in cuts if c > g), (k + 1) * R, g + 256)
                out_ref[slot, g - k * R:nxt - k * R, :] = src(g, nxt)
                g = nxt

    args = [secs[s] for s in order] + ([dwf] if dwf is not None else [])
    return _pcall(body, name=name, out_shape=jax.ShapeDtypeStruct((len(shards), R, D), BF16))(*args)


def _inproj_fwd(x, g, wmt, wft, *, tm, tn, comm=None):
    T, D = x.shape
    nj = wmt.shape[0] // tn
    grid = (T // tm, nj)

    def body(*refs):
        (x_ref, g_ref, wm_ref, wf_ref), (z_ref, zf_ref, h_ref), (h_scr,), rider = _split_refs(refs, 4, 3, comm)
        first, last = _grid_ends(grid)
        if comm:
            comm.begin(rider, first)

        @pl.when(pl.program_id(1) == 0)
        def _():
            xv = x_ref[...]
            r = lax.rsqrt(jnp.mean(xv * xv, axis=-1, keepdims=True) + EPS)
            h = ((xv * r) * g_ref[...]).astype(BF16)
            h_scr[...] = h
            h_ref[...] = h
            zf_ref[...] = _dot_nt(h, wf_ref[...])

        z_ref[...] = _dot_nt(h_scr[...], wm_ref[...])
        if comm:
            comm.end(rider, last)

    return _pcall(
        body, name="inproj_fwd", grid=grid,
        **_with_comm(
            comm,
            [pl.BlockSpec((tm, D), lambda i, j: (i, 0)), pl.BlockSpec((1, D), lambda i, j: (0, 0)),
             pl.BlockSpec((tn, D), lambda i, j: (j, 0)), pl.BlockSpec((LANES, D), lambda i, j: (0, 0))],
            [pl.BlockSpec((tm, tn), lambda i, j: (i, j)), pl.BlockSpec((tm, LANES), lambda i, j: (i, 0)),
             pl.BlockSpec((tm, D), lambda i, j: (i, 0))],
            [jax.ShapeDtypeStruct((T, wmt.shape[0]), F32), jax.ShapeDtypeStruct((T, LANES), F32), jax.ShapeDtypeStruct((T, D), BF16)],
            [pltpu.VMEM((tm, D), BF16)]),
    )(x, g, wmt, wft, *(comm.operands() if comm else ()))


def _attn_prep(z, zf, bfp, gq, gk, *, B, S, H, ts):
    FW = H * HEAD_DIM
    nts = S // ts
    scale = HEAD_DIM ** -0.5

    def body(zq_ref, zk_ref, zv_ref, zf_ref, bf_ref, gq_ref, gk_ref, qa_ref, ka_ref, va_ref, carry):
        @pl.when(pl.program_id(1) == 0)
        def _():
            carry[...] = jnp.zeros_like(carry)

        lane = _lane_iota(ts)
        xf = zf_ref[...] + bf_ref[...]
        logf = jnp.minimum(xf, 0.0) - jnp.log1p(jnp.exp(-jnp.abs(xf)))
        logf = jnp.where(lane < H, logf, 0.0)
        ri = lax.broadcasted_iota(jnp.int32, (ts, ts), 0)
        ci = lax.broadcasted_iota(jnp.int32, (ts, ts), 1)
        tri = (ri >= ci).astype(BF16)
        c = carry[...]
        for piece in _split3(logf):
            c = c + _dot(tri, piece)
        carry[...] = c[ts - 1:ts, :]
        c1, c2, c3 = _split3(c)
        cpk = c1.astype(F32) + pltpu.roll(c2.astype(F32), H, 1) + pltpu.roll(c3.astype(F32), 2 * H, 1)
        cq, ck = cpk.astype(BF16), (-cpk).astype(BF16)
        prow = lax.broadcasted_iota(jnp.int32, (LANES, LANES), 0)
        pcol = lax.broadcasted_iota(jnp.int32, (LANES, LANES), 1)
        ones_q = jnp.where((lane >= K0) & (lane < K0 + 3), 1.0, 0.0)
        ones_k = jnp.where(((lane >= C0) & (lane < C0 + 3)) | ((lane >= L0) & (lane < L0 + 3)), 1.0, 0.0)
        ones_v = jnp.where((lane >= C0) & (lane < C0 + 3), 1.0, 0.0)
        for h in range(H):
            p, e = divmod(h, 2)
            cols = slice(p * LANES, (p + 1) * LANES)
            piece = jnp.where(prow == h, 0, jnp.where(prow == H + h, 1, jnp.where(prow == 2 * H + h, 2, -LANES)))
            aug_q = _dot(cq, (pcol == C0 + piece).astype(BF16)) + ones_q
            aug_k = _dot(ck, (pcol == K0 + piece).astype(BF16)) + ones_k
            for z_ref, g_ref, o_ref, aug, mul in ((zq_ref, gq_ref, qa_ref, aug_q, scale), (zk_ref, gk_ref, ka_ref, aug_k, 1.0)):
                blk = z_ref[:, cols]
                r = lax.rsqrt(_half_sums(blk * blk) * (1.0 / HEAD_DIM) + EPS)
                y = ((blk * r) * g_ref[:, cols]).astype(BF16).astype(F32) * mul
                o_ref[0, h] = (_head_to_low_roll(y, e, lane) + aug).astype(BF16)
            va_ref[0, h] = (_head_to_low_roll(zv_ref[:, cols], e, lane) + ones_v).astype(BF16)

    sec = lambda n: pl.BlockSpec((ts, FW), lambda b, i: (b * nts + i, n))
    row = lambda w: pl.BlockSpec((1, w), lambda b, i: (0, 0))
    head = pl.BlockSpec((1, H, ts, LANES), lambda b, i: (b, 0, i, 0))
    hshape = jax.ShapeDtypeStruct((B, H, S, LANES), BF16)
    return _pcall(
        body, name="attn_prep", grid=(B, nts),
        in_specs=[sec(0), sec(1), sec(2), pl.BlockSpec((ts, LANES), lambda b, i: (b * nts + i, 0)), row(LANES), row(FW), row(FW)],
        out_specs=[head, head, head], out_shape=[hshape, hshape, hshape],
        scratch_shapes=[pltpu.VMEM((1, LANES), F32)],
    )(z, z, z, zf, bfp, gq, gk)


def _attn_fwd(qa, ka, va, *, t, blk, dblk, G):
    B, H, S, _ = qa.shape
    n = S // t
    assert t % blk == 0

    def body(q_ref, k_ref, v_ref, a_ref, qb_ref):
        i = pl.program_id(2)
        lane = _lane_iota(t)
        qs = [q_ref[0, e] for e in range(G)]

        def step(start, width, carry, q0=None):
            rows = pl.ds(pl.multiple_of(start, dblk), width)
            lo = q0 or 0
            new = []
            for e in range(G):
                m, acc = carry[e]
                s = _dot_nt(qs[e][lo:], k_ref[0, e, rows, :])
                if q0 is not None:
                    row = lax.broadcasted_iota(jnp.int32, (t - lo, width), 0)
                    col = lax.broadcasted_iota(jnp.int32, (t - lo, width), 1)
                    s = jnp.where(col <= row, s, NEG_INF)
                m_new = jnp.maximum(m[lo:], jnp.max(s, axis=-1, keepdims=True))
                p = jnp.exp(s - m_new)
                acc_new = jnp.exp(m[lo:] - m_new) * acc[lo:] + _dot(p.astype(BF16), v_ref[0, e, rows, :])
                if lo:
                    m_new, acc_new = jnp.concatenate([m[:lo], m_new], axis=0), jnp.concatenate([acc[:lo], acc_new], axis=0)
                new.append((m_new, acc_new))
            return tuple(new)

        carry = tuple((jnp.full((t, 1), NEG_INF, F32), jnp.zeros((t, LANES), F32)) for _ in range(G))
        carry = lax.fori_loop(0, i * (t // blk), lambda j, c: step(j * blk, blk, c), carry)
        for d in range(t // dblk):
            carry = step(i * t + d * dblk, dblk, carry, q0=d * dblk)
        outs = []
        for e in range(G):
            m, acc = carry[e]
            l = jnp.sum(jnp.where(lane == C0, acc, 0.0), axis=-1, keepdims=True)
            outs.append(acc / l)
            qb_ref[0, e] = _put3(qs[e].astype(F32), lane, L0, _split3(-(m + jnp.log(l)))).astype(BF16)
        for pp in range(G // 2):
            a_ref[:, pp * LANES:(pp + 1) * LANES] = jnp.where(lane < HEAD_DIM, outs[2 * pp], pltpu.roll(outs[2 * pp + 1], HEAD_DIM, 1))

    return _pcall(
        body, name="attn_fwd", grid=(B, H // G, n),
        in_specs=[pl.BlockSpec((1, G, t, LANES), lambda b, p, i: (b, p, i, 0)),
                  pl.BlockSpec((1, G, S, LANES), lambda b, p, i: (b, p, 0, 0)),
                  pl.BlockSpec((1, G, S, LANES), lambda b, p, i: (b, p, 0, 0))],
        out_specs=[pl.BlockSpec((t, G * HEAD_DIM), lambda b, p, i: (b * n + i, p)),
                   pl.BlockSpec((1, G, t, LANES), lambda b, p, i: (b, p, i, 0))],
        out_shape=[jax.ShapeDtypeStruct((B * S, H * HEAD_DIM), F32), jax.ShapeDtypeStruct((B, H, S, LANES), BF16)],
    )(qa, ka, va)


def _conv_fwd(z, cw, cb, *, B, S, sec_w, rc):
    C = cw.shape[1]
    nct = C // LANES
    off_a, off_b = 4 * sec_w // LANES, 5 * sec_w // LANES

    def body(ga_ref, gb_ref, w_ref, b_ref, uc_ref, buf):
        buf[0:HALO, :] = jnp.zeros((HALO, LANES), F32)
        buf[HALO:HALO + S, :] = ga_ref[...] * _sig(gb_ref[...])

        def chunk(cidx, carry):
            r0 = pl.multiple_of(cidx * rc, rc)
            acc = jnp.zeros((rc, LANES), F32)
            for j in range(CONV_K):
                acc = acc + w_ref[j:j + 1, :] * buf[pl.ds(r0 + (HALO - CONV_K + 1 + j), rc), :]
            uc_ref[pl.ds(r0, rc), :] = acc + b_ref[...]
            return carry

        lax.fori_loop(0, S // rc, chunk, 0)

    return _pcall(
        body, name="conv_fwd", grid=(B, nct),
        in_specs=[pl.BlockSpec((S, LANES), lambda b, c: (b, off_a + c)), pl.BlockSpec((S, LANES), lambda b, c: (b, off_b + c)),
                  pl.BlockSpec((CONV_K, LANES), lambda b, c: (0, c)), pl.BlockSpec((1, LANES), lambda b, c: (0, c))],
        out_specs=pl.BlockSpec((S, LANES), lambda b, c: (b, c)),
        out_shape=jax.ShapeDtypeStruct((B * S, C), F32),
        scratch_shapes=[pltpu.VMEM((HALO + S, LANES), F32)],
    )(z, z, cw, cb)


def _ln_parts(uc, lg, lb):
    mu = jnp.mean(uc, axis=-1, keepdims=True)
    d = uc - mu
    rs = lax.rsqrt(jnp.mean(d * d, axis=-1, keepdims=True) + EPS)
    un = d * rs
    return rs, un, un * lg + lb


def _conv_post(uc, z, lg, lb, *, tm, sec_w):
    T, C = uc.shape
    off_c = 6 * sec_w // C

    def body(uc_ref, gc_ref, lg_ref, lb_ref, uo_ref):
        _, _, ul = _ln_parts(uc_ref[...], lg_ref[...], lb_ref[...])
        gc = gc_ref[...]
        uo_ref[...] = ((ul * _sig(ul)) * (gc * _sig(gc))).astype(BF16)

    return _pcall(
        body, name="conv_post", grid=(T // tm,),
        in_specs=[pl.BlockSpec((tm, C), lambda i: (i, 0)), pl.BlockSpec((tm, C), lambda i: (i, off_c)),
                  pl.BlockSpec((1, C), lambda i: (0, 0)), pl.BlockSpec((1, C), lambda i: (0, 0))],
        out_specs=pl.BlockSpec((tm, C), lambda i: (i, 0)), out_shape=jax.ShapeDtypeStruct((T, C), BF16),
    )(uc, z, lg, lb)


def _outproj(a, z, uo, x, tgt, wo, *, tm, sec_w):
    T, D = x.shape
    FW = a.shape[1]
    C = uo.shape[1]

    def body(a_ref, gf_ref, uo_ref, x_ref, t_ref, wo_ref, dout_ref, dy_ref, dwo_ref, loss_ref):
        @pl.when(pl.program_id(0) == 0)
        def _():
            dwo_ref[...] = jnp.zeros_like(dwo_ref)
            loss_ref[...] = jnp.zeros_like(loss_ref)

        gf = gf_ref[...]
        ao = (a_ref[...] * (gf * _sig(gf))).astype(BF16)
        uo = uo_ref[...]
        out = x_ref[...] + (_dot(ao, wo_ref[0:FW, :]) + _dot(uo, wo_ref[FW:FW + C, :]))
        err = out - t_ref[...]
        loss_ref[...] += jnp.sum(err * err) * (0.5 / D)
        dout = err * (1.0 / D)
        dout_ref[...] = dout
        db = dout.astype(BF16)
        dy_ref[:, 0:FW] = _dot_nt(db, wo_ref[0:FW, :])
        dy_ref[:, FW:FW + C] = _dot_nt(db, wo_ref[FW:FW + C, :])
        dwo_ref[0:FW, :] += _dot_tn(ao, db)
        dwo_ref[FW:FW + C, :] += _dot_tn(uo, db)

    tok = lambda w, n=0: pl.BlockSpec((tm, w), lambda i: (i, n))
    return _pcall(
        body, name="outproj", grid=(T // tm,),
        in_specs=[tok(FW), tok(sec_w, 3), tok(C), tok(D), tok(D), pl.BlockSpec((FW + C, D), lambda i: (0, 0))],
        out_specs=[tok(D), tok(FW + C), pl.BlockSpec((FW + C, D), lambda i: (0, 0)), pl.BlockSpec((8, LANES), lambda i: (0, 0))],
        out_shape=[jax.ShapeDtypeStruct((T, D), F32), jax.ShapeDtypeStruct((T, FW + C), F32),
                   jax.ShapeDtypeStruct((FW + C, D), F32), jax.ShapeDtypeStruct((8, LANES), F32)],
    )(a, z, uo, x, tgt, wo)


def _conv_post_bwd(dy, uc, z, lg, lb, *, tm, sec_w):
    T, C = uc.shape
    FW = dy.shape[1] - C
    off_c = 6 * sec_w // C

    def body(dy_ref, uc_ref, gc_ref, lg_ref, lb_ref, dgc_ref, duc_ref, small_ref):
        @pl.when(pl.program_id(0) == 0)
        def _():
            small_ref[...] = jnp.zeros_like(small_ref)

        lg = lg_ref[...]
        rs, un, ul = _ln_parts(uc_ref[...], lg, lb_ref[...])
        s_ul = _sig(ul)
        gc = gc_ref[...]
        s_gc = _sig(gc)
        duo = dy_ref[...]
        dgc_ref[...] = (duo * (ul * s_ul) * _dsilu(gc, s_gc)).astype(BF16)
        dul = duo * (gc * s_gc) * _dsilu(ul, s_ul)
        dun = dul * lg
        duc = rs * (dun - jnp.mean(dun, axis=-1, keepdims=True) - un * jnp.mean(dun * un, axis=-1, keepdims=True))
        duc_ref[...] = duc
        small_ref[0:1, :] += jnp.sum(dul * un, axis=0, keepdims=True)
        small_ref[8:9, :] += jnp.sum(dul, axis=0, keepdims=True)
        small_ref[16:17, :] += jnp.sum(duc, axis=0, keepdims=True)

    tok = lambda n: pl.BlockSpec((tm, C), lambda i: (i, n))
    row = pl.BlockSpec((1, C), lambda i: (0, 0))
    return _pcall(
        body, name="conv_post_bwd", grid=(T // tm,),
        in_specs=[tok(FW // C), tok(0), tok(off_c), row, row],
        out_specs=[tok(0), tok(0), pl.BlockSpec((24, C), lambda i: (0, 0))],
        out_shape=[jax.ShapeDtypeStruct((T, C), BF16), jax.ShapeDtypeStruct((T, C), F32), jax.ShapeDtypeStruct((24, C), F32)],
    )(dy, uc, z, lg, lb)


def _conv_bwd(duc, z, cw, *, B, S, sec_w, rc):
    C = cw.shape[1]
    nct = C // LANES
    off_a, off_b = 4 * sec_w // LANES, 5 * sec_w // LANES

    def body(duc_ref, ga_ref, gb_ref, w_ref, dga_ref, dgb_ref, dw_ref, ubuf, dbuf):
        @pl.when(pl.program_id(1) == 0)
        def _():
            dw_ref[...] = jnp.zeros_like(dw_ref)

        ubuf[0:HALO, :] = jnp.zeros((HALO, LANES), F32)
        ubuf[HALO:HALO + S, :] = ga_ref[...] * _sig(gb_ref[...])
        dbuf[0:S, :] = duc_ref[...]
        dbuf[S:S + HALO, :] = jnp.zeros((HALO, LANES), F32)

        def chunk(cidx, carry):
            r0 = pl.multiple_of(cidx * rc, rc)
            acc = jnp.zeros((rc, LANES), F32)
            for j in range(CONV_K):
                acc = acc + w_ref[j:j + 1, :] * dbuf[pl.ds(r0 + (CONV_K - 1 - j), rc), :]
            rows = pl.ds(r0, rc)
            ga, sg = ga_ref[rows, :], _sig(gb_ref[rows, :])
            dga_ref[rows, :] = (acc * sg).astype(BF16)
            dgb_ref[rows, :] = (acc * ga * sg * (1.0 - sg)).astype(BF16)
            return carry

        lax.fori_loop(0, S // rc, chunk, 0)
        def dw_chunk(cidx, accs):
            r0 = pl.multiple_of(cidx * 8, 8)
            d = dbuf[pl.ds(r0, 8), :]
            return tuple(accs[j] + d * ubuf[pl.ds(r0 + (HALO - CONV_K + 1 + j), 8), :] for j in range(CONV_K))

        accs = lax.fori_loop(0, S // 8, dw_chunk, tuple(jnp.zeros((8, LANES), F32) for _ in range(CONV_K)))
        for j in range(CONV_K):
            dw_ref[j:j + 1, :] += jnp.sum(accs[j], axis=0, keepdims=True)

    return _pcall(
        body, name="conv_bwd", grid=(nct, B),
        in_specs=[pl.BlockSpec((S, LANES), lambda c, b: (b, c)), pl.BlockSpec((S, LANES), lambda c, b: (b, off_a + c)),
                  pl.BlockSpec((S, LANES), lambda c, b: (b, off_b + c)), pl.BlockSpec((CONV_K, LANES), lambda c, b: (0, c))],
        out_specs=[pl.BlockSpec((S, LANES), lambda c, b: (b, c)), pl.BlockSpec((S, LANES), lambda c, b: (b, c)),
                   pl.BlockSpec((CONV_K, LANES), lambda c, b: (0, c))],
        out_shape=[jax.ShapeDtypeStruct((B * S, C), BF16), jax.ShapeDtypeStruct((B * S, C), BF16),
                   jax.ShapeDtypeStruct((CONV_K, C), F32)],
        scratch_shapes=[pltpu.VMEM((HALO + S, LANES), F32), pltpu.VMEM((S + HALO, LANES), F32)],
    )(duc, z, z, cw)


def _attn_bwd_prep(dy, a, z, *, B, S, H, ts, sec_w):
    FW = H * HEAD_DIM
    nts = S // ts

    def body(dy_ref, a_ref, gf_ref, dgf_ref, do_ref):
        lane = _lane_iota(ts)
        for p in range(H // 2):
            cols = slice(p * LANES, (p + 1) * LANES)
            gf, av, dya = gf_ref[:, cols], a_ref[:, cols], dy_ref[:, cols]
            sg = _sig(gf)
            dgf_ref[:, cols] = (dya * av * _dsilu(gf, sg)).astype(BF16)
            da = dya * (gf * sg)
            nd = -_half_sums(da * av)
            dab = da.astype(BF16)
            for e in range(2):
                nd_e = nd[:, e * HEAD_DIM:e * HEAD_DIM + 1]
                do_ref[0, 2 * p + e] = _put3(_head_to_low(dab, e), lane, C0, _split3(nd_e)).astype(BF16)

    tok = lambda n: pl.BlockSpec((ts, FW), lambda b, i: (b * nts + i, n))
    return _pcall(
        body, name="attn_bwd_prep", grid=(B, nts),
        in_specs=[tok(0), tok(0), pl.BlockSpec((ts, sec_w), lambda b, i: (b * nts + i, 3))],
        out_specs=[tok(0), pl.BlockSpec((1, H, ts, LANES), lambda b, i: (b, 0, i, 0))],
        out_shape=[jax.ShapeDtypeStruct((B * S, FW), BF16), jax.ShapeDtypeStruct((B, H, S, LANES), BF16)],
    )(dy, a, z)


def _attn_bwd(qb, ka, va, doa, *, t, blk, dblk, G, comm=None):
    B, H, S, _ = qb.shape
    n = S // t
    assert t % blk == 0
    grid = (B, H // G, n)

    def body(*refs):
        (q_ref, k_ref, v_ref, do_ref), (dq_ref, dk_ref, dv_ref), _, rider = _split_refs(refs, 4, 3, comm)
        first, last = _grid_ends(grid)
        if comm:
            comm.begin(rider, first)
        j = pl.program_id(2)

        @pl.when(j == 0)
        def _():
            dq_ref[...] = jnp.zeros_like(dq_ref)

        ks = [k_ref[0, e] for e in range(G)]
        vs = [v_ref[0, e] for e in range(G)]

        def step(start, width, carry, shift=None):
            rows = pl.ds(pl.multiple_of(start, dblk), width)
            hi = t if shift is None else shift + width
            new = []
            for e in range(G):
                dk, dv = carry[e]
                k, v = ks[e][:hi], vs[e][:hi]
                q, do = q_ref[0, e, rows, :], do_ref[0, e, rows, :]
                st = _dot_nt(k, q)
                dpt = _dot_nt(v, do)
                if shift is not None:
                    row = lax.broadcasted_iota(jnp.int32, (hi, width), 0)
                    col = lax.broadcasted_iota(jnp.int32, (hi, width), 1)
                    st = jnp.where(row <= col + shift, st, NEG_INF)
                pt = jnp.exp(st)
                dsb = (pt * dpt).astype(BF16)
                dq_ref[0, e, rows, :] += _dot_tn(dsb, k)
                dk_new, dv_new = dk[:hi] + _dot(dsb, q), dv[:hi] + _dot(pt.astype(BF16), do)
                if hi < t:
                    dk_new, dv_new = jnp.concatenate([dk_new, dk[hi:]], axis=0), jnp.concatenate([dv_new, dv[hi:]], axis=0)
                new.append((dk_new, dv_new))
            return tuple(new)

        zero = jnp.zeros((t, LANES), F32)
        carry = tuple((zero, zero) for _ in range(G))
        for d in range(t // dblk):
            carry = step(j * t + d * dblk, dblk, carry, shift=d * dblk)
        carry = lax.fori_loop((j + 1) * (t // blk), S // blk, lambda i, c: step(i * blk, blk, c), carry)
        for e in range(G):
            dk_ref[0, e], dv_ref[0, e] = carry[e]
        if comm:
            comm.end(rider, last)

    full = pl.BlockSpec((1, G, S, LANES), lambda b, h, j: (b, h, 0, 0))
    tile = pl.BlockSpec((1, G, t, LANES), lambda b, h, j: (b, h, j, 0))
    shape = jax.ShapeDtypeStruct((B, H, S, LANES), F32)
    return _pcall(
        body, name="attn_bwd", grid=grid,
        **_with_comm(comm, [full, tile, tile, full], [full, tile, tile], [shape, shape, shape], []),
    )(qb, ka, va, doa, *(comm.operands() if comm else ()))


def _attn_post_bwd(dq, dk, dv, z, zf, bfp, gq, gk, *, B, S, H, ts):
    FW = H * HEAD_DIM
    nts = S // ts
    scale = HEAD_DIM ** -0.5

    def body(dq_ref, dk_ref, dv_ref, zq_ref, zk_ref, zf_ref, bf_ref, gq_ref, gk_ref,
             dzq_ref, dzk_ref, dzv_ref, dzf_ref, small_ref, carry):
        first = (pl.program_id(0) == 0) & (pl.program_id(1) == 0)

        @pl.when(first)
        def _():
            small_ref[...] = jnp.zeros_like(small_ref)

        @pl.when(pl.program_id(1) == 0)
        def _():
            carry[...] = jnp.zeros_like(carry)

        lane = _lane_iota(ts)
        dcs = jnp.zeros((ts, LANES), F32)
        for p in range(H // 2):
            cols = slice(p * LANES, (p + 1) * LANES)
            for d_ref, z_ref, g_ref, o_ref, srow, mul in ((dq_ref, zq_ref, gq_ref, dzq_ref, 0, scale),
                                                          (dk_ref, zk_ref, gk_ref, dzk_ref, 1, 1.0)):
                dy = jnp.where(lane < HEAD_DIM, d_ref[0, 2 * p], pltpu.roll(d_ref[0, 2 * p + 1], HEAD_DIM, 1)) * mul
                xv, g = z_ref[:, cols], g_ref[:, cols]
                r = lax.rsqrt(_half_sums(xv * xv) * (1.0 / HEAD_DIM) + EPS)
                dxh = dy * g
                mm = _half_sums(dxh * xv) * (1.0 / HEAD_DIM)
                o_ref[:, cols] = (r * (dxh - xv * (r * r) * mm)).astype(BF16)
                small_ref[8 * srow:8 * srow + 1, cols] += jnp.sum(dy * xv * r, axis=0, keepdims=True)
            dzv_ref[:, cols] = jnp.where(lane < HEAD_DIM, dv_ref[0, 2 * p], pltpu.roll(dv_ref[0, 2 * p + 1], HEAD_DIM, 1)).astype(BF16)
            for e in range(2):
                h = 2 * p + e
                dc = jnp.sum(jnp.where(lane == C0, dq_ref[0, h], 0.0) - jnp.where(lane == K0, dk_ref[0, h], 0.0),
                             axis=-1, keepdims=True)
                dcs = jnp.where(lane == h, dc, dcs)
        ri = lax.broadcasted_iota(jnp.int32, (ts, ts), 0)
        ci = lax.broadcasted_iota(jnp.int32, (ts, ts), 1)
        tri = (ci >= ri).astype(BF16)
        dl = carry[...]
        for piece in _split3(dcs):
            dl = dl + _dot(tri, piece)
        carry[...] = dl[0:1, :]
        xf = zf_ref[...] + bf_ref[...]
        df = jnp.where(lane < H, dl * _sig(-xf), 0.0)
        dzf_ref[...] = df.astype(BF16)
        small_ref[16:17, 0:LANES] += jnp.sum(df, axis=0, keepdims=True)

    rev = lambda b, i: (b, 0, nts - 1 - i, 0)
    head = pl.BlockSpec((1, H, ts, LANES), rev)
    tokrow = lambda b, i: b * nts + nts - 1 - i
    sec = lambda n: pl.BlockSpec((ts, FW), lambda b, i: (tokrow(b, i), n))
    narrow = pl.BlockSpec((ts, LANES), lambda b, i: (tokrow(b, i), 0))
    row = lambda w: pl.BlockSpec((1, w), lambda b, i: (0, 0))
    return _pcall(
        body, name="attn_post_bwd", grid=(B, nts),
        in_specs=[head, head, head, sec(0), sec(1), narrow, row(LANES), row(FW), row(FW)],
        out_specs=[sec(0), sec(0), sec(0), narrow, pl.BlockSpec((24, FW), lambda b, i: (0, 0))],
        out_shape=[jax.ShapeDtypeStruct((B * S, FW), BF16)] * 3 + [jax.ShapeDtypeStruct((B * S, LANES), BF16),
                                                                   jax.ShapeDtypeStruct((24, FW), F32)],
        scratch_shapes=[pltpu.VMEM((1, LANES), F32)],
    )(dq, dk, dv, z, z, zf, bfp, gq, gk)


def _inproj_bwd_x(dzs, dzf, wmt, wft, x, g, dout, *, tm, comm=None):
    T, D = x.shape
    nj = len(dzs)
    tn = dzs[0].shape[1]
    grid = (T // tm, nj)

    def body(*refs):
        ins, (gx_ref, dg_ref), _, rider = _split_refs(refs, nj + 6, 2, comm)
        acc = gx_ref
        dz_refs = ins[:nj]
        dzf_ref, wm_ref, wf_ref, x_ref, g_ref, dout_ref = ins[nj:]
        first, last = _grid_ends(grid)
        if comm:
            comm.begin(rider, first)
        i, j = pl.program_id(0), pl.program_id(1)

        @pl.when((i == 0) & (j == 0))
        def _():
            dg_ref[...] = jnp.zeros_like(dg_ref)

        @pl.when(j == 0)
        def _():
            acc[...] = _dot(dzf_ref[...], wf_ref[...])

        for s in range(nj):
            @pl.when(j == s)
            def _():
                acc[...] += _dot(dz_refs[s][...], wm_ref[...])

        @pl.when(j == nj - 1)
        def _():
            dh, xv, g = acc[...], x_ref[...], g_ref[...]
            r = lax.rsqrt(jnp.mean(xv * xv, axis=-1, keepdims=True) + EPS)
            dxh = dh * g
            mm = jnp.mean(dxh * xv, axis=-1, keepdims=True)
            gx_ref[...] = dout_ref[...] + r * (dxh - xv * (r * r) * mm)
            dg_ref[0:1, :] += jnp.sum(dh * xv * r, axis=0, keepdims=True)

        if comm:
            comm.end(rider, last)

    tok = pl.BlockSpec((tm, D), lambda i, j: (i, 0))
    sec = pl.BlockSpec((tm, tn), lambda i, j: (i, 0))
    return _pcall(
        body, name="inproj_bwd_x", grid=grid,
        **_with_comm(
            comm,
            [sec] * nj + [pl.BlockSpec((tm, LANES), lambda i, j: (i, 0)), pl.BlockSpec((tn, D), lambda i, j: (j, 0)),
                          pl.BlockSpec((LANES, D), lambda i, j: (0, 0)), tok, pl.BlockSpec((1, D), lambda i, j: (0, 0)), tok],
            [tok, pl.BlockSpec((8, D), lambda i, j: (0, 0))],
            [jax.ShapeDtypeStruct((T, D), F32), jax.ShapeDtypeStruct((8, D), F32)],
            []),
    )(*dzs, dzf, wmt, wft, x, g, dout, *(comm.operands() if comm else ()))


def _inproj_bwd_w(h, dz, *, tm, name):
    T, D = h.shape
    W = dz.shape[1]
    ni = T // tm

    def body(h_ref, dz_ref, dw_ref, acc):
        i = pl.program_id(0)

        @pl.when(i == 0)
        def _():
            acc[...] = jnp.zeros_like(acc)

        acc[...] += _dot_tn(dz_ref[...], h_ref[...])

        @pl.when(i == ni - 1)
        def _():
            dw_ref[...] = acc[...].astype(BF16)

    return _pcall(
        body, name=name, grid=(ni,),
        in_specs=[pl.BlockSpec((tm, D), lambda i: (i, 0)), pl.BlockSpec((tm, W), lambda i: (i, 0))],
        out_specs=pl.BlockSpec((W, D), lambda i: (0, 0)), out_shape=jax.ShapeDtypeStruct((W, D), BF16),
        scratch_shapes=[pltpu.VMEM((W, D), F32)],
    )(h, dz)


def _adamw(w, g, m, v):
    m = ADAM_B1 * m + (1.0 - ADAM_B1) * g
    v = ADAM_B2 * v + (1.0 - ADAM_B2) * (g * g)
    m_hat = m / (1.0 - ADAM_B1 ** ADAM_STEP)
    v_hat = v / (1.0 - ADAM_B2 ** ADAM_STEP)
    return -ADAM_LR * (m_hat / (jnp.sqrt(v_hat) + ADAM_EPS) + ADAM_WD * w), m, v


def _adam_update(w, m, v, parts, *, name, tr):
    R, Cc = w.shape[0], w.shape[-1]

    def body(w_ref, m_ref, v_ref, p_ref, g_ref, d_ref, nm_ref, nv_ref):
        g = p_ref[0].astype(F32)
        for k in range(1, N_DEV):
            g = g + p_ref[k].astype(F32)
        g_ref[...] = g
        d_ref[...], nm_ref[...], nv_ref[...] = _adamw(w_ref[...], g, m_ref[...], v_ref[...])

    blk = pl.BlockSpec((tr, Cc), lambda i: (i, 0)) if w.ndim == 2 else pl.BlockSpec((tr, None, Cc), lambda i: (i, 0, 0))
    shape = jax.ShapeDtypeStruct(w.shape, F32)
    return _pcall(
        body, name=name, grid=(pl.cdiv(R, tr),),
        in_specs=[blk, blk, blk, pl.BlockSpec((N_DEV, tr, Cc), lambda i: (0, i, 0))],
        out_specs=[blk, blk, blk, blk], out_shape=[shape, shape, shape, shape],
    )(w, m, v, parts)


MESH = pl.DeviceIdType.MESH
ANY = pl.BlockSpec(memory_space=pl.ANY)


def _flip(v, bit):
    return 1 - v if bit else v


def _allgather(shards):
    n = len(shards)

    def body(*refs):
        srcs, outs = refs[:n], refs[n:2 * n]
        send_sems, recv_sems, local_sems = refs[2 * n:]
        x, y, c = lax.axis_index("x"), lax.axis_index("y"), lax.axis_index("c")
        sibling = (x, y, 1 - c)
        chips = [(1 - x, y), (x, 1 - y), (1 - x, 1 - y)]
        slot = lambda px, py, pc: 4 * px + 2 * py + pc

        def copy(a, k, block, to, src=None):
            dst = outs[a].at[slot(*block)]
            return pltpu.make_async_remote_copy(src_ref=dst if src is None else src, dst_ref=dst, send_sem=send_sems.at[7 * a + k],
                                                recv_sem=recv_sems.at[7 * a + k], device_id=to, device_id_type=MESH)

        mine = [pltpu.make_async_copy(srcs[a], outs[a].at[slot(x, y, c)], local_sems.at[a]) for a in range(n)]
        sends = []
        for a in range(n):
            mine[a].start()
            sends.append(copy(a, 0, (x, y, c), sibling, src=srcs[a]))
            sends += [copy(a, 1 + j, (x, y, c), (*chip, c), src=srcs[a]) for j, chip in enumerate(chips)]
        for cp in sends:
            cp.start()
        for a in range(n):
            for j, chip in enumerate(chips):
                copy(a, 1 + j, (*chip, c), (x, y, c)).wait_recv()
                fwd = copy(a, 4 + j, (*chip, c), sibling)
                fwd.start()
                sends.append(fwd)
        for a in range(n):
            copy(a, 0, (x, y, 1 - c), (x, y, c)).wait_recv()
            for j, chip in enumerate(chips):
                copy(a, 4 + j, (*chip, 1 - c), (x, y, c)).wait_recv()
        for cp in sends:
            cp.wait_send()
        for cp in mine:
            cp.wait()

    return pl.pallas_call(
        body, name="allgather_weights", in_specs=[ANY] * n, out_specs=[ANY] * n,
        out_shape=[jax.ShapeDtypeStruct((N_DEV,) + s.shape, s.dtype) for s in shards],
        scratch_shapes=[pltpu.SemaphoreType.DMA((7 * n,)), pltpu.SemaphoreType.DMA((7 * n,)), pltpu.SemaphoreType.DMA((n,))],
    )(*shards)


class _Comm:
    def __init__(self, arrays, owners, into=None):
        self.arrays, self.owners, self.n = list(arrays), list(owners), len(arrays)
        self.into = list(into) if into else [None] * self.n
        self.carried = [t for t in self.into if t is not None]

    def operands(self):
        return self.arrays + self.carried

    def aliases(self, n_in, n_out):
        out, pos = {}, 0
        for a, t in enumerate(self.into):
            if t is not None:
                out[n_in + self.n + pos] = n_out + a
                pos += 1
        return out

    def out_shape(self):
        return [jax.ShapeDtypeStruct((N_DEV,) + (a.shape if o is None else a.shape[1:]), a.dtype)
                for a, o in zip(self.arrays, self.owners)]

    def scratch(self):
        return [pltpu.SemaphoreType.DMA((7 * self.n,)), pltpu.SemaphoreType.DMA((7 * self.n,)), pltpu.SemaphoreType.DMA((self.n,))]

    def ops(self, srcs, outs, send_sems, recv_sems, local_sems):
        x, y, c = lax.axis_index("x"), lax.axis_index("y"), lax.axis_index("c")
        me = 4 * x + 2 * y + c
        when = lambda p, f: (lambda: pl.when(p)(f))
        starts, waits = [], []
        for a, owners in enumerate(self.owners):
            src, out = srcs[a], outs[a]

            def remote(piece, slot, rel, to, a=a, out=out):
                sem = 7 * a + rel - 1
                return pltpu.make_async_remote_copy(src_ref=piece, dst_ref=out.at[slot], send_sem=send_sems.at[sem],
                                                    recv_sem=recv_sems.at[sem], device_id=to, device_id_type=MESH)

            peers = [(d, (_flip(x, d & 4), _flip(y, d & 2), _flip(c, d & 1))) for d in range(1, N_DEV)]
            if owners is None:
                local = pltpu.make_async_copy(src, out.at[me], local_sems.at[a])
                starts.append(local.start)
                waits.append(local.wait)
                for d, peer in peers:
                    cp = remote(src, me, d, peer)
                    starts.append(cp.start)
                    waits.append(cp.wait_send)
                receives = True
            else:
                for s, dest in enumerate(owners):
                    to = (dest >> 2 & 1, dest >> 1 & 1, dest & 1)
                    rel = 4 * _flip(x, to[0]) + 2 * _flip(y, to[1]) + _flip(c, to[2])
                    cp = remote(src.at[s], me, jnp.maximum(rel, 1), to)
                    local = pltpu.make_async_copy(src.at[s], out.at[me], local_sems.at[a])
                    starts += [when(rel != 0, cp.start), when(rel == 0, local.start)]
                    waits += [when(rel != 0, cp.wait_send), when(rel == 0, local.wait)]
                receives = me == owners[0]
                for dest in owners[1:]:
                    receives = receives | (me == dest)
            piece = src if owners is None else src.at[0]
            for d, peer in peers:
                def arrival(d=d, peer=peer, piece=piece, remote=remote):
                    remote(piece, 4 * peer[0] + 2 * peer[1] + peer[2], d, peer).wait_recv()

                waits.append(arrival if receives is True else when(receives, arrival))
        return starts, waits

    def _phase(self, which, rider, cond):
        srcs, outs, sems = rider

        def go():
            for op in self.ops(srcs, outs, *sems)[which]:
                op()

        go() if cond is True else pl.when(cond)(go)

    def begin(self, rider, first=True):
        self._phase(0, rider, first)

    def end(self, rider, last=True):
        self._phase(1, rider, last)


def _split_refs(refs, n_in, n_out, comm):
    k = comm.n if comm else 0
    o0 = n_in + k + (len(comm.carried) if comm else 0)
    ins, c_src = refs[:n_in], refs[n_in:n_in + k]
    outs, c_out = refs[o0:o0 + n_out], refs[o0 + n_out:o0 + n_out + k]
    rest = refs[o0 + n_out + k:]
    scratch, c_sem = (rest[:-3], rest[-3:]) if comm else (rest, ())
    return ins, outs, scratch, (c_src, c_out, c_sem)


def _grid_ends(grid):
    first = functools.reduce(jnp.logical_and, [pl.program_id(a) == 0 for a in range(len(grid))])
    last = functools.reduce(jnp.logical_and, [pl.program_id(a) == g - 1 for a, g in enumerate(grid)])
    return first, last


def _with_comm(comm, in_specs, out_specs, out_shape, scratch):
    if comm is None:
        return dict(in_specs=in_specs, out_specs=out_specs, out_shape=out_shape, scratch_shapes=scratch)
    return dict(in_specs=in_specs + [ANY] * len(comm.operands()), out_specs=out_specs + [ANY] * comm.n,
                out_shape=out_shape + comm.out_shape(), scratch_shapes=scratch + comm.scratch(),
                input_output_aliases=comm.aliases(len(in_specs), len(out_specs)))


def _exchange(comm, *, name):
    n = comm.n

    def body(*refs):
        rider = (refs[:n], refs[n:2 * n], refs[2 * n:])
        comm.begin(rider)
        comm.end(rider)

    return pl.pallas_call(body, name=name, in_specs=[ANY] * n, out_specs=[ANY] * n, out_shape=comm.out_shape(),
                          scratch_shapes=comm.scratch())(*comm.arrays)


def kernel(x, norm_g, w_in, b_forget, q_norm_g, k_norm_g, conv_w, conv_b, conv_ln_g, conv_ln_b, w_out, loss_target, m_norm_g, m_w_in, m_b_forget, m_q_norm_g, m_k_norm_g, m_conv_w, m_conv_b, m_conv_ln_g, m_conv_ln_b, m_w_out, v_norm_g, v_w_in, v_b_forget, v_q_norm_g, v_k_norm_g, v_conv_w, v_conv_b, v_conv_ln_g, v_conv_ln_b, v_w_out):
    B, S, D = x.shape
    H = q_norm_g.shape[1]
    FW = H * HEAD_DIM
    assert q_norm_g.shape[2] == HEAD_DIM and FW == D and conv_b.shape[1] == D and H <= LANES // 3
    ncol = w_in.shape[2]
    T = B * S

    to_t = lambda t: jnp.transpose(t, (2, 0, 1))
    from_t = lambda t: jnp.transpose(t, (1, 2, 0))
    wt, mt, vt = to_t(w_in), to_t(m_w_in), to_t(v_w_in)

    x2, tgt = x.reshape(T, D), loss_target.reshape(T, D)
    tm, ts, rc = min(512, T), min(256, S), min(64, S)
    tw = min(1024, T)
    everyone = list(range(N_DEV))
    bfp = jnp.pad(b_forget, ((0, 0), (0, LANES - H)))
    gq, gk = q_norm_g.reshape(1, FW), k_norm_g.reshape(1, FW)

    (wg,) = _allgather([wt.reshape(ncol, D).astype(BF16)])
    wmt, wft = _w_relayout(wg, D=D, H=H)
    z, zf, h, wog, cwg = _inproj_fwd(x2, norm_g, wmt, wft, tm=min(1024, T), tn=FW,
                                     comm=_Comm([w_out[0].astype(BF16), conv_w[0]], [None, None]))
    wo = wog.reshape(N_DEV * wog.shape[1], D)
    cw = jnp.transpose(cwg, (1, 0, 2)).reshape(CONV_K, D)
    qa, ka, va = _attn_prep(z, zf, bfp, gq, gk, B=B, S=S, H=H, ts=ts)
    a, qb = _attn_fwd(qa, ka, va, t=min(512, S), blk=min(512, S), dblk=min(512, S), G=4)
    uc = _conv_fwd(z, cw, conv_b, B=B, S=S, sec_w=FW, rc=rc)
    uo = _conv_post(uc, z, conv_ln_g, conv_ln_b, tm=ts, sec_w=FW)
    dout, dy, dwo, loss = _outproj(a, z, uo, x2, tgt, wo, tm=ts, sec_w=FW)

    dgc, duc, small_c = _conv_post_bwd(dy, uc, z, conv_ln_g, conv_ln_b, tm=ts, sec_w=FW)
    dga, dgb, dcw = _conv_bwd(duc, z, cw, B=B, S=S, sec_w=FW, rc=rc)
    dgf, doa = _attn_bwd_prep(dy, a, z, B=B, S=S, H=H, ts=ts, sec_w=FW)
    _, f1, _ = _section_rows(D, H)
    late_rows = [k for k in everyone if k * ncol >= f1]
    early_rows = [k for k in everyone if k not in late_rows]
    first_late_sec = (late_rows[0] * ncol - H) // FW
    dws = {s: _inproj_bwd_w(h, dz, tm=tw, name=f"inproj_bwd_w{s}") for s, dz in ((3, dgf), (4, dga), (5, dgb), (6, dgc))}
    dw_late = _dw_relayout({s: dws[s] for s in range(first_late_sec, 7)}, None, R=ncol, H=H, shards=late_rows, name="dw_relayout_late")
    dwo_parts = dwo.reshape(N_DEV, dwo.shape[0] // N_DEV, D).astype(BF16)
    dcw_parts = jnp.transpose(dcw.reshape(CONV_K, N_DEV, D // N_DEV), (1, 0, 2))
    dq, dk, dv, r_w, r_wo, r_cw = _attn_bwd(qb, ka, va, doa, t=min(2048, S), blk=min(512, S), dblk=min(512, S), G=4,
                                            comm=_Comm([dw_late, dwo_parts, dcw_parts], [late_rows, everyone, everyone]))
    dzq, dzk, dzv, dzf, small_a = _attn_post_bwd(dq, dk, dv, z, zf, bfp, gq, gk, B=B, S=S, H=H, ts=ts)
    dws.update({s: _inproj_bwd_w(h, dz, tm=tw, name=f"inproj_bwd_w{s}") for s, dz in ((0, dzq), (1, dzk), (2, dzv))})
    dwf = _inproj_bwd_w(h, dzf, tm=tw, name="inproj_bwd_wf")
    last_early_sec = (early_rows[-1] * ncol + ncol - 1 - H) // FW
    dw_early = _dw_relayout({s: dws[s] for s in range(last_early_sec + 1)}, dwf, R=ncol, H=H, shards=early_rows, name="dw_relayout_early")
    gx, dg, r_w = _inproj_bwd_x([dzq, dzk, dzv, dgf, dga, dgb, dgc], dzf, wmt, wft, x2, norm_g, dout, tm=tm,
                                comm=_Comm([dw_early], [early_rows], into=[r_w]))
    small = jnp.concatenate([dg, small_a[16:24], small_a[0:8], small_a[8:16], small_c[16:24], small_c[0:8], small_c[8:16],
                             jnp.pad(loss, ((0, 0), (0, D - LANES)))], axis=0)
    (r_small,) = _exchange(_Comm([small], [None]), name="exchange_small")

    g_w, d_w, nm_w, nv_w = [from_t(t) for t in _adam_update(wt, mt, vt, r_w, name="adam_w_in", tr=min(128, D))]
    g_wo, d_wo, nm_wo, nv_wo = [t[None] for t in _adam_update(w_out[0], m_w_out[0], v_w_out[0], r_wo, name="adam_w_out",
                                                               tr=min(128, w_out.shape[1]))]
    g_cw, d_cw, nm_cw, nv_cw = [t[None] for t in _adam_update(conv_w[0], m_conv_w[0], v_conv_w[0], r_cw, name="adam_conv_w", tr=CONV_K)]
    tile = lambda t: jnp.pad(t.reshape(1, -1), ((0, 7), (0, D - t.size)))
    pack = lambda *ps: jnp.concatenate([tile(p) for p in ps] + [jnp.zeros((8, D), F32)], axis=0)
    small_w = pack(norm_g, b_forget, q_norm_g, k_norm_g, conv_b, conv_ln_g, conv_ln_b)
    small_m = pack(m_norm_g, m_b_forget, m_q_norm_g, m_k_norm_g, m_conv_b, m_conv_ln_g, m_conv_ln_b)
    small_v = pack(v_norm_g, v_b_forget, v_q_norm_g, v_k_norm_g, v_conv_b, v_conv_ln_g, v_conv_ln_b)
    g_s, d_s, nm_s, nv_s = _adam_update(small_w, small_m, small_v, r_small, name="adam_small", tr=small_w.shape[0])

    def leaves(small_t, w_t, cw_t, wo_t):
        row = lambda r, like: small_t[8 * r:8 * r + 1, :like.size].reshape(like.shape)
        return [row(0, norm_g), w_t, row(1, b_forget), row(2, q_norm_g), row(3, k_norm_g), cw_t, row(4, conv_b),
                row(5, conv_ln_g), row(6, conv_ln_b), wo_t]

    return (g_s[56, 0], gx.reshape(B, S, D), *leaves(g_s, g_w, g_cw, g_wo), *leaves(d_s, d_w, d_cw, d_wo),
            *leaves(nm_s, nm_w, nm_cw, nm_wo), *leaves(nv_s, nv_w, nv_cw, nv_wo))
```
